```python
import math
import jax, jax.numpy as jnp
from jax import lax
import numpy as np

D_MODEL = 1024
BATCH = 16
SEQ = 2048
DEPTH = 4

CHUNK = 64
N_A = DEPTH // 2
N_B = DEPTH - N_A
N_DENSE = (DEPTH + 1) // 2
N_MOE = DEPTH // 2

RET_HEADS = 8
RET_DK = D_MODEL // RET_HEADS
RET_DV = 2 * D_MODEL // RET_HEADS
RET_IN = 2 * RET_HEADS * RET_DK + 2 * RET_HEADS * RET_DV
ROPE_BASE = 10000.0

SB_HEADS = 16
SB_DH = D_MODEL // SB_HEADS
Q_BLOCK = 128

FF_DENSE = 2816
N_EXPERTS = 8
TOP_K = 2
FF_EXPERT = 3584
EPS = 1e-6

kernel_name = "yoco_retention_stickbreaking_moe_adaln"


def rms_norm(x, g):
    x32 = x.astype(jnp.float32)
    y = x32 * lax.rsqrt(jnp.mean(x32 * x32, axis=-1, keepdims=True) + EPS)
    return (y * g.astype(jnp.float32)).astype(x.dtype)


def modulate(h, shift, scale):
    return h * (1.0 + scale[:, None, :]) + shift[:, None, :]


def rope(t, cos, sin):
    t1, t2 = jnp.split(t, 2, axis=-1)
    return jnp.concatenate([t1 * cos - t2 * sin, t1 * sin + t2 * cos], axis=-1)


def retention(h, positions, w_in, gn_w, w_out):
    B, S, _ = h.shape
    H, DK, DV, C = RET_HEADS, RET_DK, RET_DV, CHUNK
    nC = S // C
    proj = h @ w_in
    q, k, v, g = jnp.split(proj, [H * DK, 2 * H * DK, 2 * H * DK + H * DV], axis=-1)
    q = q.reshape(B, S, H, DK)
    k = k.reshape(B, S, H, DK)
    v = v.reshape(B, S, H, DV)
    inv_freq = jnp.power(ROPE_BASE, -jnp.arange(0, DK, 2, dtype=jnp.float32) / DK)
    ang = positions.astype(jnp.float32)[..., None] * inv_freq
    cos = jnp.cos(ang)[:, :, None, :].astype(h.dtype)
    sin = jnp.sin(ang)[:, :, None, :].astype(h.dtype)
    q = rope(q, cos, sin)
    k = rope(k, cos, sin) * (DK ** -0.5)
    log_gamma = jnp.log(1.0 - jnp.exp2(-5.0 - jnp.arange(H, dtype=jnp.float32)))
    pos = jnp.arange(C, dtype=jnp.float32)
    d_intra = jnp.exp(jnp.abs(pos[:, None] - pos[None, :])[None] * log_gamma[:, None, None]).astype(h.dtype)
    q_decay = jnp.exp((pos[:, None] + 1.0) * log_gamma[None, :]).astype(h.dtype)
    k_decay = jnp.exp((C - 1.0 - pos)[:, None] * log_gamma[None, :]).astype(h.dtype)
    chunk_decay = jnp.exp(C * log_gamma).astype(h.dtype)

    def to_chunks(t):
        return jnp.moveaxis(t.reshape(B, nC, C, H, t.shape[-1]), 1, 0)

    def step(state, qkv):
        qc, kc, vc = qkv
        scores = jnp.einsum('bchd,bshd->bhcs', qc, kc) * d_intra[None]
        intra = jnp.einsum('bhcs,bshe->bche', scores, vc)
        cross = jnp.einsum('bchd,bhde->bche', qc * q_decay[None, :, :, None], state)
        new_state = state * chunk_decay[None, :, None, None] + jnp.einsum(
            'bshd,bshe->bhde', kc * k_decay[None, :, :, None], vc)
        return new_state, intra + cross

    state0 = jnp.zeros((B, H, DK, DV), dtype=q.dtype)
    _, o = lax.scan(step, state0, (to_chunks(q), to_chunks(k), to_chunks(v)))
    o = jnp.moveaxis(o, 0, 1).reshape(B, S, H, DV)
    o32 = o.astype(jnp.float32)
    mu = jnp.mean(o32, axis=-1, keepdims=True)
    var = jnp.mean(jnp.square(o32 - mu), axis=-1, keepdims=True)
    on = ((o32 - mu) * lax.rsqrt(var + EPS)).reshape(B, S, H * DV) * gn_w.astype(jnp.float32)
    return (jax.nn.silu(g) * on.astype(h.dtype)) @ w_out


def stick_breaking(h, k_sh, v_sh, w_q, w_out):
    B, S, _ = h.shape
    q = (h @ w_q).reshape(B, S, SB_HEADS, SB_DH)
    scale = SB_DH ** -0.5
    outs = []
    for i in range(S // Q_BLOCK):
        end = (i + 1) * Q_BLOCK
        qb = q[:, i * Q_BLOCK:end]
        kb = k_sh[:, :end]
        vb = v_sh[:, :end]
        z = jnp.einsum('bqhd,bkhd->bhqk', qb, kb).astype(jnp.float32) * scale
        t_idx = i * Q_BLOCK + jnp.arange(Q_BLOCK)
        s_idx = jnp.arange(end)
        causal = s_idx[None, :] < t_idx[:, None]
        log_one_minus = jnp.where(causal, jax.nn.log_sigmoid(-z), 0.0)
        tail = lax.cumsum(log_one_minus, axis=3, reverse=True) - log_one_minus
        a = jnp.where(causal, jnp.exp(jax.nn.log_sigmoid(z) + tail), 0.0)
        outs.append(jnp.einsum('bhqk,bkhd->bqhd', a.astype(vb.dtype), vb))
    o = jnp.concatenate(outs, axis=1).reshape(B, S, SB_HEADS * SB_DH)
    return o @ w_out


def swiglu(h, w_in, w_out):
    a, b = jnp.split(h @ w_in, 2, axis=-1)
    return (jax.nn.silu(a) * b) @ w_out


def moe_swiglu(h, router, w_in, w_out):
    B, S, D = h.shape
    hf = h.reshape(B * S, D)
    logits = (hf @ router).astype(jnp.float32)
    top_v, top_i = lax.top_k(logits, TOP_K)
    top_w = jax.nn.softmax(top_v, axis=-1)
    gates = jnp.sum(jax.nn.one_hot(top_i, N_EXPERTS, dtype=jnp.float32) * top_w[..., None], axis=1)
    y = jnp.zeros_like(hf)
    for e in range(N_EXPERTS):
        y = y + gates[:, e:e + 1].astype(hf.dtype) * swiglu(hf, w_in[e], w_out[e])
    return y.reshape(B, S, D)


def setup_inputs(seed: int = 0) -> dict:
    key = jax.random.key(seed)
    ks = jax.random.split(key, 24)
    D = D_MODEL

    def nrm(k, shape, fan_in, s=1.0):
        return jax.random.normal(k, shape, jnp.float32) * (s * fan_in ** -0.5)

    def gain(k, shape):
        return 1.0 + 0.02 * jax.random.normal(k, shape, jnp.float32)

    x = jax.random.normal(ks[0], (BATCH, SEQ, D), jnp.float32)
    c = jax.random.normal(ks[1], (BATCH, D), jnp.float32)
    offset = jax.random.randint(ks[2], (BATCH, 1), 0, 4096, dtype=jnp.int32)
    positions = offset + jnp.arange(SEQ, dtype=jnp.int32)[None, :]
    return {
        "x": x,
        "c": c,
        "positions": positions,
        "ada_w": nrm(ks[3], (DEPTH, D, 6 * D), D, 0.5),
        "ada_b": 0.02 * jax.random.normal(ks[4], (DEPTH, 6 * D), jnp.float32),
        "norm_mix": gain(ks[5], (DEPTH, D)),
        "norm_ff": gain(ks[6], (DEPTH, D)),
        "ret_w_in": nrm(ks[7], (N_A, D, RET_IN), D),
        "ret_gn": gain(ks[8], (N_A, RET_HEADS * RET_DV)),
        "ret_w_out": nrm(ks[9], (N_A, RET_HEADS * RET_DV, D), RET_HEADS * RET_DV),
        "kv_ada_w": nrm(ks[10], (D, 2 * D), D, 0.5),
        "kv_ada_b": 0.02 * jax.random.normal(ks[11], (2 * D,), jnp.float32),
        "kv_norm": gain(ks[12], (D,)),
        "kv_w": nrm(ks[13], (D, 2 * SB_HEADS * SB_DH), D),
        "sb_w_q": nrm(ks[14], (N_B, D, SB_HEADS * SB_DH), D),
        "sb_w_out": nrm(ks[15], (N_B, SB_HEADS * SB_DH, D), SB_HEADS * SB_DH),
        "ff_w_in": nrm(ks[16], (N_DENSE, D, 2 * FF_DENSE), D),
        "ff_w_out": nrm(ks[17], (N_DENSE, FF_DENSE, D), FF_DENSE),
        "moe_router": nrm(ks[18], (N_MOE, D, N_EXPERTS), D),
        "moe_w_in": nrm(ks[19], (N_MOE, N_EXPERTS, D, 2 * FF_EXPERT), D),
        "moe_w_out": nrm(ks[20], (N_MOE, N_EXPERTS, FF_EXPERT, D), FF_EXPERT),
        "final_norm": gain(ks[21], (D,)),
    }


def reference(x, c, positions, ada_w, ada_b, norm_mix, norm_ff, ret_w_in, ret_gn, ret_w_out,
              kv_ada_w, kv_ada_b, kv_norm, kv_w, sb_w_q, sb_w_out, ff_w_in, ff_w_out,
              moe_router, moe_w_in, moe_w_out, final_norm):
    B, S, _ = x.shape
    cond = jax.nn.silu(c)
    k_sh = None
    v_sh = None
    for i in range(DEPTH):
        mod = cond @ ada_w[i] + ada_b[i]
        sh_m, sc_m, g_m, sh_f, sc_f, g_f = jnp.split(mod, 6, axis=-1)
        h = modulate(rms_norm(x, norm_mix[i]), sh_m, sc_m)
        if i < N_A:
            y = retention(h, positions, ret_w_in[i], ret_gn[i], ret_w_out[i])
        else:
            j = i - N_A
            y = stick_breaking(h, k_sh, v_sh, sb_w_q[j], sb_w_out[j])
        x = x + g_m[:, None, :] * y
        h = modulate(rms_norm(x, norm_ff[i]), sh_f, sc_f)
        if i % 2 == 0:
            y = swiglu(h, ff_w_in[i // 2], ff_w_out[i // 2])
        else:
            y = moe_swiglu(h, moe_router[i // 2], moe_w_in[i // 2], moe_w_out[i // 2])
        x = x + g_f[:, None, :] * y
        if i == N_A - 1:
            kv_mod = cond @ kv_ada_w + kv_ada_b
            sh_kv, sc_kv = jnp.split(kv_mod, 2, axis=-1)
            hk = modulate(rms_norm(x, kv_norm), sh_kv, sc_kv)
            kv = (hk @ kv_w).reshape(B, S, 2, SB_HEADS, SB_DH)
            k_sh = kv[:, :, 0]
            v_sh = kv[:, :, 1]
    return rms_norm(x, final_norm)
```

```python
import functools

import jax
import jax.numpy as jnp
from jax import lax
from jax.experimental import pallas as pl
from jax.experimental.pallas import tpu as pltpu

F32 = jnp.float32
BF16 = jnp.bfloat16

D_MODEL = 1024
DEPTH = 4
N_A = DEPTH // 2
CHUNK = 64
RET_HEADS = 8
RET_DK = D_MODEL // RET_HEADS
RET_DV = 2 * D_MODEL // RET_HEADS
ROPE_BASE = 10000.0
SB_HEADS = 16
SB_DH = D_MODEL // SB_HEADS
FF_DENSE = 2816
N_EXPERTS = 8
FF_EXPERT = 3584
EPS = 1e-6

LANES = 128
BF16_ROWS = 16
VMEM_LIMIT = 56 * 1024 * 1024

ROW_TILE = 1024
FF_DENSE_TILE = 1408
FF_EXPERT_TILE = 896
RET_BLOCK = 256
SB_TILE = 128
MOE_BLOCK = 512
MOE_GROUP = BF16_ROWS
MOE_CAP = 2 * MOE_BLOCK + N_EXPERTS * MOE_GROUP
SEG_SIZES = (512, 256, 128, 64, 32, 16)
SB_LOG_FLOOR = -110.0


def _cparams(sem):
    return pltpu.CompilerParams(dimension_semantics=sem, vmem_limit_bytes=VMEM_LIMIT)


def _sigmoid(v):
    return 1.0 / (1.0 + jnp.exp(-v))


def _rms_mod(x, gain, shift, scale):
    y = x * lax.rsqrt(jnp.mean(x * x, axis=-1, keepdims=True) + EPS)
    return (y * gain) * (1.0 + scale) + shift


def _ada_kernel(c_ref, w_ref, b_ref, o_ref):
    c = c_ref[...]
    cond = (c * _sigmoid(c)).astype(BF16)
    o_ref[0] = jnp.dot(cond, w_ref[0].astype(BF16), preferred_element_type=F32) + b_ref[0]


def _ada_mod(c, w, b):
    nl, d, n = w.shape
    bsz = c.shape[0]
    tn = 1024
    return pl.pallas_call(
        _ada_kernel,
        out_shape=jax.ShapeDtypeStruct((nl, bsz, n), F32),
        grid=(nl, n // tn),
        in_specs=[
            pl.BlockSpec((bsz, d), lambda l, j: (0, 0)),
            pl.BlockSpec((1, d, tn), lambda l, j: (l, 0, j)),
            pl.BlockSpec((1, 1, tn), lambda l, j: (l, 0, j)),
        ],
        out_specs=pl.BlockSpec((1, bsz, tn), lambda l, j: (l, 0, j)),
        compiler_params=_cparams(("parallel", "parallel")),
        name="ada_mod",
    )(c, w, b.reshape(nl, 1, n))


def _rope_table_kernel(pos_ref, freq_ref, cs_ref, sn_ref):
    ang = pos_ref[...] * freq_ref[...]
    lane = lax.broadcasted_iota(jnp.int32, ang.shape, 1)
    cs_ref[...] = jnp.cos(ang)
    sn_ref[...] = jnp.where(lane < RET_DK // 2, -jnp.sin(ang), jnp.sin(ang))


def _rope_tables(positions):
    n = positions.size
    inv_freq = jnp.power(ROPE_BASE, -jnp.arange(0, RET_DK, 2, dtype=F32) / RET_DK)
    freq = jnp.concatenate([inv_freq, inv_freq]).reshape(1, RET_DK)
    pos = jnp.broadcast_to(positions.astype(F32).reshape(n, 1), (n, RET_DK))
    tr = min(n, 2048)
    return pl.pallas_call(
        _rope_table_kernel,
        out_shape=(jax.ShapeDtypeStruct((n, RET_DK), F32),) * 2,
        grid=(n // tr,),
        in_specs=[pl.BlockSpec((tr, RET_DK), lambda i: (i, 0)),
                  pl.BlockSpec((1, RET_DK), lambda i: (0, 0))],
        out_specs=(pl.BlockSpec((tr, RET_DK), lambda i: (i, 0)),) * 2,
        compiler_params=_cparams(("parallel",)),
        name="rope_tables",
    )(pos, freq)


def _nmm_kernel(x_ref, g_ref, sh_ref, sc_ref, w_ref, *rest, rope_cols, tn):
    if rope_cols:
        cs_ref, sn_ref, o_ref, h_ref = rest
    else:
        o_ref, h_ref = rest
    j = pl.program_id(1)

    @pl.when(j == 0)
    def _():
        h_ref[...] = _rms_mod(x_ref[...], g_ref[...], sh_ref[0], sc_ref[0]).astype(BF16)

    r = jnp.dot(h_ref[...], w_ref[...], preferred_element_type=F32)
    if not rope_cols:
        o_ref[...] = r.astype(o_ref.dtype)
        return

    n_rope = rope_cols // tn

    @pl.when(j < n_rope)
    def _():
        cs = cs_ref[...]
        sn = sn_ref[...]
        k_scale = jnp.where(j >= n_rope // 2, RET_DK ** -0.5, 1.0).astype(F32)
        for c in range(tn // RET_DK):
            t = r[:, c * RET_DK:(c + 1) * RET_DK]
            rot = t * cs + pltpu.roll(t, RET_DK // 2, 1) * sn
            o_ref[:, c * RET_DK:(c + 1) * RET_DK] = (rot * k_scale).astype(o_ref.dtype)

    @pl.when(j >= n_rope)
    def _():
        o_ref[...] = r.astype(o_ref.dtype)


def _norm_mod_matmul(x, gain, shift, scale, w, seq, rope=None, tn=512):
    n, d = x.shape
    m = w.shape[1]
    tm = min(ROW_TILE, seq)
    bsz = shift.shape[0]
    in_specs = [
        pl.BlockSpec((tm, d), lambda i, j: (i, 0)),
        pl.BlockSpec((1, d), lambda i, j: (0, 0)),
        pl.BlockSpec((1, 1, d), lambda i, j: (i * tm // seq, 0, 0)),
        pl.BlockSpec((1, 1, d), lambda i, j: (i * tm // seq, 0, 0)),
        pl.BlockSpec((d, tn), lambda i, j: (0, j)),
    ]
    args = [x, gain.reshape(1, d), shift.reshape(bsz, 1, d), scale.reshape(bsz, 1, d), w]
    rope_cols = 0
    if rope is not None:
        rope_cols = 2 * RET_HEADS * RET_DK
        in_specs += [pl.BlockSpec((tm, RET_DK), lambda i, j: (i, 0))] * 2
        args += list(rope)
    return pl.pallas_call(
        functools.partial(_nmm_kernel, rope_cols=rope_cols, tn=tn),
        out_shape=jax.ShapeDtypeStruct((n, m), BF16),
        grid=(n // tm, m // tn),
        in_specs=in_specs,
        out_specs=pl.BlockSpec((tm, tn), lambda i, j: (i, j)),
        scratch_shapes=[pltpu.VMEM((tm, d), BF16)],
        compiler_params=_cparams(("parallel", "arbitrary")),
        name="norm_mod_matmul_rope" if rope_cols else "norm_mod_matmul",
    )(*args)


def _mm_res_kernel(a_ref, w_ref, x_ref, gate_ref, o_ref):
    y = jnp.dot(a_ref[...], w_ref[...], preferred_element_type=F32)
    o_ref[...] = x_ref[...] + gate_ref[0] * y


def _matmul_residual(a, w, x, gate, seq):
    n, k = a.shape
    d = w.shape[1]
    tm = min(512, seq)
    bsz = gate.shape[0]
    return pl.pallas_call(
        _mm_res_kernel,
        out_shape=jax.ShapeDtypeStruct((n, d), F32),
        grid=(n // tm,),
        in_specs=[
            pl.BlockSpec((tm, k), lambda i: (i, 0)),
            pl.BlockSpec((k, d), lambda i: (0, 0)),
            pl.BlockSpec((tm, d), lambda i: (i, 0)),
            pl.BlockSpec((1, 1, d), lambda i: (i * tm // seq, 0, 0)),
        ],
        out_specs=pl.BlockSpec((tm, d), lambda i: (i, 0)),
        compiler_params=_cparams(("parallel",)),
        name="matmul_residual",
    )(a, w, x, gate.reshape(bsz, 1, d))


def _ffn_kernel(x_ref, g_ref, sh_ref, sc_ref, gate_ref, wa_ref, wb_ref, wo_ref, o_ref, h_ref, acc_ref):
    f = pl.program_id(1)

    @pl.when(f == 0)
    def _():
        h_ref[...] = _rms_mod(x_ref[...], g_ref[...], sh_ref[0], sc_ref[0]).astype(BF16)
        acc_ref[...] = jnp.zeros_like(acc_ref)

    h = h_ref[...]
    a = jnp.dot(h, wa_ref[...], preferred_element_type=F32)
    b = jnp.dot(h, wb_ref[...], preferred_element_type=F32)
    act = (a * _sigmoid(a) * b).astype(BF16)
    acc_ref[...] += jnp.dot(act, wo_ref[...], preferred_element_type=F32)

    @pl.when(f == pl.num_programs(1) - 1)
    def _():
        o_ref[...] = x_ref[...] + gate_ref[0] * acc_ref[...]


def _dense_ffn(x, gain, shift, scale, gate, w_in, w_out, seq):
    n, d = x.shape
    ff = w_out.shape[0]
    tf = FF_DENSE_TILE
    nf = ff // tf
    tm = min(ROW_TILE, seq)
    bsz = shift.shape[0]
    vec = lambda i, f: (i * tm // seq, 0, 0)
    return pl.pallas_call(
        _ffn_kernel,
        out_shape=jax.ShapeDtypeStruct((n, d), F32),
        grid=(n // tm, nf),
        in_specs=[
            pl.BlockSpec((tm, d), lambda i, f: (i, 0)),
            pl.BlockSpec((1, d), lambda i, f: (0, 0)),
            pl.BlockSpec((1, 1, d), vec),
            pl.BlockSpec((1, 1, d), vec),
            pl.BlockSpec((1, 1, d), vec),
            pl.BlockSpec((d, tf), lambda i, f: (0, f)),
            pl.BlockSpec((d, tf), lambda i, f: (0, nf + f)),
            pl.BlockSpec((tf, d), lambda i, f: (f, 0)),
        ],
        out_specs=pl.BlockSpec((tm, d), lambda i, f: (i, 0)),
        scratch_shapes=[pltpu.VMEM((tm, d), BF16), pltpu.VMEM((tm, d), F32)],
        compiler_params=_cparams(("parallel", "arbitrary")),
        name="dense_ffn",
    )(x, gain.reshape(1, d), shift.reshape(bsz, 1, d), scale.reshape(bsz, 1, d),
      gate.reshape(bsz, 1, d), w_in, w_in, w_out)


def _ret_kernel(q_ref, k_ref, v_ref, g_ref, gn_ref, dm_ref, qd_ref, kd_ref, cd_ref, o_ref, st_ref, *, nblk):
    st_ref[...] = jnp.zeros_like(st_ref)
    dmat = dm_ref[0]
    qdec = qd_ref[0]
    kdec = kd_ref[0]
    cdec = cd_ref[0, 0:1, :]
    gn = gn_ref[...]

    def body(t, carry):
        sl = pl.ds(pl.multiple_of(t * RET_BLOCK, RET_BLOCK), RET_BLOCK)
        q = q_ref[0, sl, :]
        k = k_ref[0, sl, :]
        v = v_ref[0, sl, :]
        s = lax.dot_general(q, k, (((1,), (1,)), ((), ())), preferred_element_type=F32) * dmat
        intra = jnp.dot(s.astype(BF16), v, preferred_element_type=F32)
        state = st_ref[...]
        qd = (q.astype(F32) * qdec).astype(BF16)
        cross = jnp.dot(qd, state.astype(BF16), preferred_element_type=F32)
        kdt = jnp.transpose(k.astype(F32) * kdec).astype(BF16)
        st_ref[...] = state * cdec + jnp.dot(kdt, v, preferred_element_type=F32)
        o = intra + cross
        mu = jnp.mean(o, axis=-1, keepdims=True)
        oc = o - mu
        var = jnp.mean(oc * oc, axis=-1, keepdims=True)
        on = (oc * lax.rsqrt(var + EPS)) * gn
        g = g_ref[0, sl, :].astype(F32)
        o_ref[0, sl, :] = ((g * _sigmoid(g)) * on).astype(o_ref.dtype)
        return carry

    lax.fori_loop(0, nblk, body, 0)


def _retention_tables():
    t = RET_BLOCK
    log_gamma = jnp.log(1.0 - jnp.exp2(-5.0 - jnp.arange(RET_HEADS, dtype=F32)))
    pos = jnp.arange(t, dtype=F32)
    chunk = jnp.arange(t) // CHUNK
    dist = jnp.abs(pos[:, None] - pos[None, :])
    visible = (chunk[None, :] <= chunk[:, None]).astype(F32)
    dmat = jnp.exp(dist[None] * log_gamma[:, None, None]) * visible[None]
    qdec = jnp.exp((pos[None, :] + 1.0) * log_gamma[:, None])
    kdec = jnp.exp((t - 1.0 - pos)[None, :] * log_gamma[:, None])
    cdec = jnp.exp(t * log_gamma)
    qdec = jnp.broadcast_to(qdec[:, :, None], (RET_HEADS, t, RET_DK))
    kdec = jnp.broadcast_to(kdec[:, :, None], (RET_HEADS, t, RET_DK))
    cdec = jnp.broadcast_to(cdec[:, None, None], (RET_HEADS, 8, RET_DV))
    return dmat, qdec, kdec, cdec


def _retention_core(proj, gn_w, tables, bsz, seq):
    dmat, qdec, kdec, cdec = tables
    h = RET_HEADS
    t = RET_BLOCK
    kq = h * RET_DK // RET_DK
    kv = 2 * h * RET_DK // RET_DV
    return pl.pallas_call(
        functools.partial(_ret_kernel, nblk=seq // t),
        out_shape=jax.ShapeDtypeStruct((bsz, seq, h * RET_DV), BF16),
        grid=(bsz, h),
        in_specs=[
            pl.BlockSpec((1, seq, RET_DK), lambda b, i: (b, 0, i)),
            pl.BlockSpec((1, seq, RET_DK), lambda b, i: (b, 0, kq + i)),
            pl.BlockSpec((1, seq, RET_DV), lambda b, i: (b, 0, kv + i)),
            pl.BlockSpec((1, seq, RET_DV), lambda b, i: (b, 0, kv + h + i)),
            pl.BlockSpec((1, RET_DV), lambda b, i: (0, i)),
            pl.BlockSpec((1, t, t), lambda b, i: (i, 0, 0)),
            pl.BlockSpec((1, t, RET_DK), lambda b, i: (i, 0, 0)),
            pl.BlockSpec((1, t, RET_DK), lambda b, i: (i, 0, 0)),
            pl.BlockSpec((1, 8, RET_DV), lambda b, i: (i, 0, 0)),
        ],
        out_specs=pl.BlockSpec((1, seq, RET_DV), lambda b, i: (b, 0, i)),
        scratch_shapes=[pltpu.VMEM((RET_DK, RET_DV), F32)],
        compiler_params=_cparams(("parallel", "parallel")),
        name="retention_core",
    )(proj, proj, proj, proj, gn_w.reshape(1, h * RET_DV), dmat, qdec, kdec, cdec)


def _sb_kernel(q_ref, k_ref, v_ref, o_ref):
    t = SB_TILE
    qi = pl.program_id(2)
    q = q_ref[0]
    lane = lax.broadcasted_iota(jnp.int32, (t, 2 * SB_DH), 1)
    row = lax.broadcasted_iota(jnp.int32, (t, t), 0)
    col = lax.broadcasted_iota(jnp.int32, (t, t), 1)
    uj = lax.broadcasted_iota(jnp.int32, (t, 2 * t), 0)
    us = lax.broadcasted_iota(jnp.int32, (t, 2 * t), 1)
    umat = jnp.where((uj >= us) | (us >= t), 1.0, 0.0).astype(BF16)
    scale = SB_DH ** -0.5
    heads = []
    for hh in range(2):
        qh = jnp.where((lane >= hh * SB_DH) & (lane < (hh + 1) * SB_DH), q, jnp.zeros_like(q))

        def cond(carry):
            kb, r, _ = carry
            return jnp.logical_and(kb >= 0, jnp.max(r) > SB_LOG_FLOOR)

        def body(carry):
            kb, r, acc = carry
            ks = pl.ds(pl.multiple_of(kb * t, t), t)
            kk = k_ref[0, ks, :]
            vv = v_ref[0, ks, :]
            z = lax.dot_general(qh, kk, (((1,), (1,)), ((), ())), preferred_element_type=F32) * scale
            causal = (kb * t + col) < (qi * t + row)
            sp = jnp.maximum(z, 0.0) + jnp.log(1.0 + jnp.exp(-jnp.abs(z)))
            lom = jnp.where(causal, -sp, 0.0)
            hi = lom.astype(BF16)
            lo = (lom - hi.astype(F32)).astype(BF16)
            c = (jnp.dot(hi, umat, preferred_element_type=F32)
                 + jnp.dot(lo, umat, preferred_element_type=F32))
            incl = c[:, :t]
            tot = c[:, t:]
            a = jnp.where(causal, jnp.exp((z - sp) + (incl - lom) + r), 0.0)
            acc = acc + jnp.dot(a.astype(BF16), vv, preferred_element_type=F32)
            return kb - 1, r + tot, acc

        init = (qi, jnp.zeros((t, t), F32), jnp.zeros((t, 2 * SB_DH), F32))
        heads.append(lax.while_loop(cond, body, init)[2])
    o_ref[0] = jnp.where(lane < SB_DH, heads[0], heads[1]).astype(o_ref.dtype)


def _stick_breaking_core(q, kv, bsz, seq):
    t = SB_TILE
    pairs = SB_HEADS // 2
    w = 2 * SB_DH
    return pl.pallas_call(
        _sb_kernel,
        out_shape=jax.ShapeDtypeStruct((bsz, seq, D_MODEL), BF16),
        grid=(bsz, pairs, seq // t),
        in_specs=[
            pl.BlockSpec((1, t, w), lambda b, p, i: (b, i, p)),
            pl.BlockSpec((1, seq, w), lambda b, p, i: (b, 0, p)),
            pl.BlockSpec((1, seq, w), lambda b, p, i: (b, 0, pairs + p)),
        ],
        out_specs=pl.BlockSpec((1, t, w), lambda b, p, i: (b, i, p)),
        compiler_params=_cparams(("parallel", "parallel", "parallel")),
        name="stick_breaking_core",
    )(q, kv, kv)


def _route_kernel(x_ref, g_ref, sh_ref, sc_ref, r_ref, h_ref, meta_ref, meta_t_ref, pc_ref):
    tb = MOE_BLOCK
    h32 = _rms_mod(x_ref[...], g_ref[...], sh_ref[0], sc_ref[0])
    h_ref[...] = h32.astype(BF16)
    logits = jnp.dot(h32, r_ref[...], preferred_element_type=F32, precision=lax.Precision.HIGHEST)
    lane = lax.broadcasted_iota(jnp.int32, (tb, LANES), 1)
    lane_f = lane.astype(F32)
    neg = jnp.float32(-jnp.inf)
    lg = jnp.where(lane < N_EXPERTS, logits, neg)
    m1 = jnp.max(lg, axis=-1, keepdims=True)
    i1 = jnp.min(jnp.where(lg == m1, lane_f, float(LANES)), axis=-1, keepdims=True)
    oh1 = lane_f == i1
    lg2 = jnp.where(oh1, neg, lg)
    m2 = jnp.max(lg2, axis=-1, keepdims=True)
    i2 = jnp.min(jnp.where(lg2 == m2, lane_f, float(LANES)), axis=-1, keepdims=True)
    oh2 = lane_f == i2
    e = jnp.exp(m2 - m1)
    w1 = 1.0 / (1.0 + e)
    w2 = e / (1.0 + e)

    tr = lax.broadcasted_iota(jnp.int32, (tb, tb), 0)
    tc = lax.broadcasted_iota(jnp.int32, (tb, tb), 1)
    lower = jnp.where(tc < tr, 1.0, 0.0).astype(BF16)
    oh1f = jnp.where(oh1, 1.0, 0.0)
    oh2f = jnp.where(oh2, 1.0, 0.0)
    cum1 = jnp.dot(lower, oh1f.astype(BF16), preferred_element_type=F32)
    cum2 = jnp.dot(lower, oh2f.astype(BF16), preferred_element_type=F32)
    cnt1 = jnp.sum(oh1f, axis=0, keepdims=True)
    cnt2 = jnp.sum(oh2f, axis=0, keepdims=True)
    pc = jnp.floor((cnt1 + cnt2 + (MOE_GROUP - 1.0)) / MOE_GROUP) * MOE_GROUP
    pj = lax.broadcasted_iota(jnp.int32, (LANES, LANES), 0)
    ps = lax.broadcasted_iota(jnp.int32, (LANES, LANES), 1)
    before = jnp.where(pj < ps, 1.0, 0.0)
    off = jnp.dot(jnp.broadcast_to(pc, (8, LANES)), before, preferred_element_type=F32,
                  precision=lax.Precision.HIGHEST)[0:1]
    pos1 = jnp.sum(jnp.where(oh1, off + cum1, 0.0), axis=-1, keepdims=True)
    pos2 = jnp.sum(jnp.where(oh2, off + cnt1 + cum2, 0.0), axis=-1, keepdims=True)
    meta = jnp.where(lane == 0, pos1,
                     jnp.where(lane == 1, pos2,
                               jnp.where(lane == 2, w1, jnp.where(lane == 3, w2, 0.0))))
    meta_ref[...] = meta
    meta_t_ref[0] = jnp.transpose(meta)[0:8]
    pc_ref[0] = jnp.broadcast_to(pc, (8, LANES))


def _moe_route(x, gain, shift, scale, router, seq):
    n, d = x.shape
    tb = MOE_BLOCK
    nb = n // tb
    bsz = shift.shape[0]
    r_pad = jnp.zeros((d, LANES), F32).at[:, :N_EXPERTS].set(router)
    vec = lambda i: (i * tb // seq, 0, 0)
    return pl.pallas_call(
        _route_kernel,
        out_shape=(
            jax.ShapeDtypeStruct((n, d), BF16),
            jax.ShapeDtypeStruct((n, LANES), F32),
            jax.ShapeDtypeStruct((nb, 8, tb), F32),
            jax.ShapeDtypeStruct((nb, 8, LANES), F32),
        ),
        grid=(nb,),
        in_specs=[
            pl.BlockSpec((tb, d), lambda i: (i, 0)),
            pl.BlockSpec((1, d), lambda i: (0, 0)),
            pl.BlockSpec((1, 1, d), vec),
            pl.BlockSpec((1, 1, d), vec),
            pl.BlockSpec((d, LANES), lambda i: (0, 0)),
        ],
        out_specs=(
            pl.BlockSpec((tb, d), lambda i: (i, 0)),
            pl.BlockSpec((tb, LANES), lambda i: (i, 0)),
            pl.BlockSpec((1, 8, tb), lambda i: (i, 0, 0)),
            pl.BlockSpec((1, 8, LANES), lambda i: (i, 0, 0)),
        ),
        compiler_params=_cparams(("parallel",)),
        name="moe_route",
    )(x, gain.reshape(1, d), shift.reshape(bsz, 1, d), scale.reshape(bsz, 1, d), r_pad)


def _segment_copies(loc_ref, glb_ref, len_ref, blk, make_copy, wait):
    for e in range(N_EXPERTS):
        idx = blk * N_EXPERTS + e
        loc = loc_ref[idx]
        glb = glb_ref[idx]
        length = len_ref[idx]
        done = jnp.int32(0)
        for sz in SEG_SIZES:
            take = (length & sz) != 0

            @pl.when(take)
            def _(loc=loc, glb=glb, done=done, sz=sz):
                cp = make_copy(pl.multiple_of(loc + done, MOE_GROUP),
                               pl.multiple_of(glb + done, MOE_GROUP), sz)
                if wait:
                    cp.wait()
                else:
                    cp.start()

            done = done + jnp.where(take, sz, 0)


def _dispatch_kernel(loc_ref, glb_ref, len_ref, h_ref, meta_t_ref, hs_in_ref, hs_ref, buf_ref, sem):
    del hs_in_ref
    b = pl.program_id(0)
    tb = MOE_BLOCK
    pos1 = meta_t_ref[0, 0:1, :]
    pos2 = meta_t_ref[0, 1:2, :]
    r = lax.broadcasted_iota(jnp.int32, (MOE_CAP, tb), 0).astype(F32)
    perm = jnp.where((r == pos1) | (r == pos2), 1.0, 0.0).astype(BF16)
    buf_ref[...] = jnp.dot(perm, h_ref[...], preferred_element_type=F32).astype(BF16)

    def make_copy(loc, glb, sz):
        return pltpu.make_async_copy(buf_ref.at[pl.ds(loc, sz)], hs_ref.at[pl.ds(glb, sz)], sem)

    _segment_copies(loc_ref, glb_ref, len_ref, b, make_copy, wait=False)
    _segment_copies(loc_ref, glb_ref, len_ref, b, make_copy, wait=True)


def _moe_dispatch(h, meta_t, loc, glb, seg_len, rows_total):
    n, d = h.shape
    tb = MOE_BLOCK
    nb = n // tb
    hs0 = jnp.zeros((rows_total, d), BF16)
    return pl.pallas_call(
        _dispatch_kernel,
        out_shape=jax.ShapeDtypeStruct((rows_total, d), BF16),
        grid_spec=pltpu.PrefetchScalarGridSpec(
            num_scalar_prefetch=3,
            grid=(nb,),
            in_specs=[
                pl.BlockSpec((tb, d), lambda i, *_: (i, 0)),
                pl.BlockSpec((1, 8, tb), lambda i, *_: (i, 0, 0)),
                pl.BlockSpec(memory_space=pl.ANY),
            ],
            out_specs=pl.BlockSpec(memory_space=pl.ANY),
            scratch_shapes=[pltpu.VMEM((MOE_CAP, d), BF16), pltpu.SemaphoreType.DMA],
        ),
        input_output_aliases={5: 0},
        compiler_params=_cparams(("arbitrary",)),
        name="moe_dispatch",
    )(loc, glb, seg_len, h, meta_t, hs0)


def _expert_kernel(te_ref, tx_ref, tv_ref, x_ref, wa_ref, wb_ref, wo_ref, o_ref, acc_ref):
    i = pl.program_id(0)
    f = pl.program_id(1)

    @pl.when(tv_ref[i] == 1)
    def _():
        @pl.when(f == 0)
        def _():
            acc_ref[...] = jnp.zeros_like(acc_ref)

        x = x_ref[...]
        a = jnp.dot(x, wa_ref[0], preferred_element_type=F32)
        b = jnp.dot(x, wb_ref[0], preferred_element_type=F32)
        act = (a * _sigmoid(a) * b).astype(BF16)
        acc_ref[...] += jnp.dot(act, wo_ref[0], preferred_element_type=F32)

        @pl.when(f == pl.num_programs(1) - 1)
        def _():
            o_ref[...] = acc_ref[...].astype(o_ref.dtype)

    @pl.when(jnp.logical_and(tv_ref[i] == 0, f == pl.num_programs(1) - 1))
    def _():
        o_ref[...] = jnp.zeros_like(o_ref)


def _moe_experts(hs, w_in, w_out, tile_expert, tile_row, tile_valid):
    rows, d = hs.shape
    tm = ROW_TILE
    tf = FF_EXPERT_TILE
    nf = FF_EXPERT // tf
    nt = rows // tm

    def f_eff(i, f, tv):
        return jnp.where(tv[i] == 1, f, nf - 1)

    return pl.pallas_call(
        _expert_kernel,
        out_shape=jax.ShapeDtypeStruct((rows, d), BF16),
        grid_spec=pltpu.PrefetchScalarGridSpec(
            num_scalar_prefetch=3,
            grid=(nt, nf),
            in_specs=[
                pl.BlockSpec((tm, d), lambda i, f, te, tx, tv: (tx[i], 0)),
                pl.BlockSpec((1, d, tf), lambda i, f, te, tx, tv: (te[i], 0, f_eff(i, f, tv))),
                pl.BlockSpec((1, d, tf), lambda i, f, te, tx, tv: (te[i], 0, nf + f_eff(i, f, tv))),
                pl.BlockSpec((1, tf, d), lambda i, f, te, tx, tv: (te[i], f_eff(i, f, tv), 0)),
            ],
            out_specs=pl.BlockSpec((tm, d), lambda i, f, te, tx, tv: (i, 0)),
            scratch_shapes=[pltpu.VMEM((tm, d), F32)],
        ),
        compiler_params=_cparams(("arbitrary", "arbitrary")),
        name="moe_experts",
    )(tile_expert, tile_row, tile_valid, hs, w_in, w_in, w_out)


def _combine_kernel(loc_ref, glb_ref, len_ref, x_ref, gate_ref, meta_ref, ys_ref, o_ref, buf_ref, sem):
    b = pl.program_id(0)
    tb = MOE_BLOCK
    buf_ref[...] = jnp.zeros_like(buf_ref)

    def make_copy(loc, glb, sz):
        return pltpu.make_async_copy(ys_ref.at[pl.ds(glb, sz)], buf_ref.at[pl.ds(loc, sz)], sem)

    _segment_copies(loc_ref, glb_ref, len_ref, b, make_copy, wait=False)
    meta = meta_ref[...]
    pos1 = meta[:, 0:1]
    pos2 = meta[:, 1:2]
    w1 = meta[:, 2:3]
    w2 = meta[:, 3:4]
    r = lax.broadcasted_iota(jnp.int32, (tb, MOE_CAP), 1).astype(F32)
    sel1 = jnp.where(r == pos1, 1.0, 0.0).astype(BF16)
    sel2 = jnp.where(r == pos2, 1.0, 0.0).astype(BF16)
    _segment_copies(loc_ref, glb_ref, len_ref, b, make_copy, wait=True)
    ys = buf_ref[...]
    y = (w1 * jnp.dot(sel1, ys, preferred_element_type=F32)
         + w2 * jnp.dot(sel2, ys, preferred_element_type=F32))
    o_ref[...] = x_ref[...] + gate_ref[0] * y


def _moe_combine(x, gate, meta, ys, loc, glb, seg_len, seq):
    n, d = x.shape
    tb = MOE_BLOCK
    bsz = gate.shape[0]
    return pl.pallas_call(
        _combine_kernel,
        out_shape=jax.ShapeDtypeStruct((n, d), F32),
        grid_spec=pltpu.PrefetchScalarGridSpec(
            num_scalar_prefetch=3,
            grid=(n // tb,),
            in_specs=[
                pl.BlockSpec((tb, d), lambda i, *_: (i, 0)),
                pl.BlockSpec((1, 1, d), lambda i, *_: (i * tb // seq, 0, 0)),
                pl.BlockSpec((tb, LANES), lambda i, *_: (i, 0)),
                pl.BlockSpec(memory_space=pl.ANY),
            ],
            out_specs=pl.BlockSpec((tb, d), lambda i, *_: (i, 0)),
            scratch_shapes=[pltpu.VMEM((MOE_CAP, d), BF16), pltpu.SemaphoreType.DMA],
        ),
        compiler_params=_cparams(("arbitrary",)),
        name="moe_combine",
    )(loc, glb, seg_len, x, gate.reshape(bsz, 1, d), meta, ys)


def _moe_layout(pc):
    nb = pc.shape[0]
    tm = ROW_TILE
    n_tiles = (nb * (2 * MOE_BLOCK + N_EXPERTS * (MOE_GROUP - 1)) + tm - 1) // tm + N_EXPERTS
    loc = jnp.cumsum(pc, axis=1) - pc
    tiles_e = (jnp.sum(pc, axis=0) + tm - 1) // tm
    tile_end = jnp.cumsum(tiles_e)
    base_e = (tile_end - tiles_e) * tm
    glb = base_e[None, :] + jnp.cumsum(pc, axis=0) - pc
    n_valid = tile_end[-1]
    ids = jnp.arange(n_tiles, dtype=jnp.int32)
    tile_row = jnp.minimum(ids, n_valid - 1)
    tile_expert = jnp.sum((tile_row[:, None] >= tile_end[None, :]).astype(jnp.int32), axis=1)
    tile_valid = (ids < n_valid).astype(jnp.int32)
    flat = lambda a: a.reshape(-1).astype(jnp.int32)
    return flat(loc), flat(glb), flat(pc), tile_expert.astype(jnp.int32), tile_row.astype(jnp.int32), tile_valid, n_tiles * tm


def _moe_block(x, gain, shift, scale, gate, router, w_in, w_out, seq):
    h, meta, meta_t, pc = _moe_route(x, gain, shift, scale, router, seq)
    pc_i = pc[:, 0, :N_EXPERTS].astype(jnp.int32)
    loc, glb, seg_len, tile_expert, tile_row, tile_valid, rows_total = _moe_layout(pc_i)
    hs = _moe_dispatch(h, meta_t, loc, glb, seg_len, rows_total)
    ys = _moe_experts(hs, w_in, w_out, tile_expert, tile_row, tile_valid)
    return _moe_combine(x, gate, meta, ys, loc, glb, seg_len, seq)


def _final_norm_kernel(x_ref, g_ref, o_ref):
    x = x_ref[...]
    o_ref[...] = (x * lax.rsqrt(jnp.mean(x * x, axis=-1, keepdims=True) + EPS)) * g_ref[...]


def _final_norm(x, gain):
    n, d = x.shape
    tm = min(ROW_TILE, n)
    return pl.pallas_call(
        _final_norm_kernel,
        out_shape=jax.ShapeDtypeStruct((n, d), F32),
        grid=(n // tm,),
        in_specs=[pl.BlockSpec((tm, d), lambda i: (i, 0)), pl.BlockSpec((1, d), lambda i: (0, 0))],
        out_specs=pl.BlockSpec((tm, d), lambda i: (i, 0)),
        compiler_params=_cparams(("parallel",)),
        name="final_norm",
    )(x, gain.reshape(1, d))


def kernel(x, c, positions, ada_w, ada_b, norm_mix, norm_ff, ret_w_in, ret_gn, ret_w_out, kv_ada_w, kv_ada_b, kv_norm, kv_w, sb_w_q, sb_w_out, ff_w_in, ff_w_out, moe_router, moe_w_in, moe_w_out, final_norm):
    bsz, seq, d = x.shape
    n = bsz * seq
    mod = _ada_mod(c, ada_w, ada_b)
    kv_mod = _ada_mod(c, kv_ada_w[None], kv_ada_b[None])[0]
    rope = _rope_tables(positions)
    ret_tables = _retention_tables()
    xf = x.reshape(n, d)
    kv = None
    for i in range(DEPTH):
        sh_m, sc_m, g_m, sh_f, sc_f, g_f = [mod[i, :, p * d:(p + 1) * d] for p in range(6)]
        if i < N_A:
            proj = _norm_mod_matmul(xf, norm_mix[i], sh_m, sc_m, ret_w_in[i].astype(BF16), seq, rope=rope)
            o = _retention_core(proj.reshape(bsz, seq, -1), ret_gn[i], ret_tables, bsz, seq)
            xf = _matmul_residual(o.reshape(n, -1), ret_w_out[i].astype(BF16), xf, g_m, seq)
        else:
            j = i - N_A
            q = _norm_mod_matmul(xf, norm_mix[i], sh_m, sc_m, sb_w_q[j].astype(BF16), seq)
            o = _stick_breaking_core(q.reshape(bsz, seq, d), kv, bsz, seq)
            xf = _matmul_residual(o.reshape(n, d), sb_w_out[j].astype(BF16), xf, g_m, seq)
        if i % 2 == 0:
            xf = _dense_ffn(xf, norm_ff[i], sh_f, sc_f, g_f, ff_w_in[i // 2].astype(BF16),
                            ff_w_out[i // 2].astype(BF16), seq)
        else:
            xf = _moe_block(xf, norm_ff[i], sh_f, sc_f, g_f, moe_router[i // 2],
                            moe_w_in[i // 2].astype(BF16), moe_w_out[i // 2].astype(BF16), seq)
        if i == N_A - 1:
            kv = _norm_mod_matmul(xf, kv_norm, kv_mod[:, :d], kv_mod[:, d:], kv_w.astype(BF16), seq)
            kv = kv.reshape(bsz, seq, 2 * d)
    return _final_norm(xf, final_norm).reshape(bsz, seq, d)
```

```python
import functools

import jax
import jax.numpy as jnp
from jax import lax
from jax.experimental import pallas as pl
from jax.experimental.pallas import tpu as pltpu

F32 = jnp.float32
BF16 = jnp.bfloat16

D_MODEL = 1024
DEPTH = 4
N_A = DEPTH // 2
CHUNK = 64
RET_HEADS = 8
RET_DK = D_MODEL // RET_HEADS
RET_DV = 2 * D_MODEL // RET_HEADS
ROPE_BASE = 10000.0
SB_HEADS = 16
SB_DH = D_MODEL // SB_HEADS
FF_DENSE = 2816
N_EXPERTS = 8
FF_EXPERT = 3584
EPS = 1e-6

LANES = 128
BF16_ROWS = 16
MXU_COLS = 256
VMEM_LIMIT = 56 * 1024 * 1024

ROW_TILE = 1024
FF_EXPERT_TILE = 1792
RET_BLOCK = 256
SB_TILE = 128
MOE_BLOCK = 512
MOE_GROUP = BF16_ROWS
MOE_CAP = 2 * MOE_BLOCK + N_EXPERTS * MOE_GROUP
SEG_SIZES = (512, 256, 128, 64, 32, 16)
SB_LOG_FLOOR = -110.0


def _cparams(sem):
    return pltpu.CompilerParams(dimension_semantics=sem, vmem_limit_bytes=VMEM_LIMIT)


def _sigmoid(v):
    return 1.0 / (1.0 + jnp.exp(-v))


def _rms_mod(x, gain, shift, scale):
    y = x * lax.rsqrt(jnp.mean(x * x, axis=-1, keepdims=True) + EPS)
    return (y * gain) * (1.0 + scale) + shift


def _ada_kernel(c_ref, w_ref, b_ref, o_ref):
    c = c_ref[...]
    cond = (c * _sigmoid(c)).astype(BF16)
    o_ref[0] = jnp.dot(cond, w_ref[0].astype(BF16), preferred_element_type=F32) + b_ref[0]


def _ada_mod(c, w, b):
    nl, d, n = w.shape
    bsz = c.shape[0]
    tn = 1024
    return pl.pallas_call(
        _ada_kernel,
        out_shape=jax.ShapeDtypeStruct((nl, bsz, n), F32),
        grid=(nl, n // tn),
        in_specs=[
            pl.BlockSpec((bsz, d), lambda l, j: (0, 0)),
            pl.BlockSpec((1, d, tn), lambda l, j: (l, 0, j)),
            pl.BlockSpec((1, 1, tn), lambda l, j: (l, 0, j)),
        ],
        out_specs=pl.BlockSpec((1, bsz, tn), lambda l, j: (l, 0, j)),
        compiler_params=_cparams(("parallel", "parallel")),
        name="ada_mod",
    )(c, w, b.reshape(nl, 1, n))


def _rope_table_kernel(pos_ref, freq_ref, cs_ref, sn_ref):
    ang = pos_ref[...] * freq_ref[...]
    lane = lax.broadcasted_iota(jnp.int32, ang.shape, 1)
    cs_ref[...] = jnp.cos(ang)
    sn_ref[...] = jnp.where(lane < RET_DK // 2, -jnp.sin(ang), jnp.sin(ang))


def _rope_tables(positions):
    n = positions.size
    inv_freq = jnp.power(ROPE_BASE, -jnp.arange(0, RET_DK, 2, dtype=F32) / RET_DK)
    freq = jnp.concatenate([inv_freq, inv_freq]).reshape(1, RET_DK)
    pos = jnp.broadcast_to(positions.astype(F32).reshape(n, 1), (n, RET_DK))
    tr = min(n, 2048)
    return pl.pallas_call(
        _rope_table_kernel,
        out_shape=(jax.ShapeDtypeStruct((n, RET_DK), F32),) * 2,
        grid=(n // tr,),
        in_specs=[pl.BlockSpec((tr, RET_DK), lambda i: (i, 0)),
                  pl.BlockSpec((1, RET_DK), lambda i: (0, 0))],
        out_specs=(pl.BlockSpec((tr, RET_DK), lambda i: (i, 0)),) * 2,
        compiler_params=_cparams(("parallel",)),
        name="rope_tables",
    )(pos, freq)


def _nmm_kernel(x_ref, g_ref, sh_ref, sc_ref, w_ref, *rest, rope_cols, tn):
    if rope_cols:
        cs_ref, sn_ref, o_ref, h_ref = rest
    else:
        o_ref, h_ref = rest
    j = pl.program_id(1)

    @pl.when(j == 0)
    def _():
        h_ref[...] = _rms_mod(x_ref[...], g_ref[...], sh_ref[0], sc_ref[0]).astype(BF16)

    r = jnp.dot(h_ref[...], w_ref[...], preferred_element_type=F32)
    if not rope_cols:
        o_ref[...] = r.astype(o_ref.dtype)
        return

    n_rope = rope_cols // tn

    @pl.when(j < n_rope)
    def _():
        cs = cs_ref[...]
        sn = sn_ref[...]
        k_scale = jnp.where(j >= n_rope // 2, RET_DK ** -0.5, 1.0).astype(F32)
        for c in range(tn // RET_DK):
            t = r[:, c * RET_DK:(c + 1) * RET_DK]
            rot = t * cs + pltpu.roll(t, RET_DK // 2, 1) * sn
            o_ref[:, c * RET_DK:(c + 1) * RET_DK] = (rot * k_scale).astype(o_ref.dtype)

    @pl.when(j >= n_rope)
    def _():
        o_ref[...] = r.astype(o_ref.dtype)


def _norm_mod_matmul(x, gain, shift, scale, w, seq, rope=None, tn=1024):
    n, d = x.shape
    m = w.shape[1]
    tm = min(ROW_TILE, seq)
    bsz = shift.shape[0]
    in_specs = [
        pl.BlockSpec((tm, d), lambda i, j: (i, 0)),
        pl.BlockSpec((1, d), lambda i, j: (0, 0)),
        pl.BlockSpec((1, 1, d), lambda i, j: (i * tm // seq, 0, 0)),
        pl.BlockSpec((1, 1, d), lambda i, j: (i * tm // seq, 0, 0)),
        pl.BlockSpec((d, tn), lambda i, j: (0, j)),
    ]
    args = [x, gain.reshape(1, d), shift.reshape(bsz, 1, d), scale.reshape(bsz, 1, d), w]
    rope_cols = 0
    if rope is not None:
        rope_cols = 2 * RET_HEADS * RET_DK
        in_specs += [pl.BlockSpec((tm, RET_DK), lambda i, j: (i, 0))] * 2
        args += list(rope)
    return pl.pallas_call(
        functools.partial(_nmm_kernel, rope_cols=rope_cols, tn=tn),
        out_shape=jax.ShapeDtypeStruct((n, m), BF16),
        grid=(n // tm, m // tn),
        in_specs=in_specs,
        out_specs=pl.BlockSpec((tm, tn), lambda i, j: (i, j)),
        scratch_shapes=[pltpu.VMEM((tm, d), BF16)],
        compiler_params=_cparams(("parallel", "arbitrary")),
        name="norm_mod_matmul_rope" if rope_cols else "norm_mod_matmul",
    )(*args)


def _mm_res_kernel(a_ref, w_ref, x_ref, gate_ref, o_ref):
    y = jnp.dot(a_ref[...], w_ref[...], preferred_element_type=F32)
    o_ref[...] = x_ref[...] + gate_ref[0] * y


def _matmul_residual(a, w, x, gate, seq):
    n, k = a.shape
    d = w.shape[1]
    tm = min(512, seq)
    bsz = gate.shape[0]
    return pl.pallas_call(
        _mm_res_kernel,
        out_shape=jax.ShapeDtypeStruct((n, d), F32),
        grid=(n // tm,),
        in_specs=[
            pl.BlockSpec((tm, k), lambda i: (i, 0)),
            pl.BlockSpec((k, d), lambda i: (0, 0)),
            pl.BlockSpec((tm, d), lambda i: (i, 0)),
            pl.BlockSpec((1, 1, d), lambda i: (i * tm // seq, 0, 0)),
        ],
        out_specs=pl.BlockSpec((tm, d), lambda i: (i, 0)),
        compiler_params=_cparams(("parallel",)),
        name="matmul_residual",
    )(a, w, x, gate.reshape(bsz, 1, d))


def _swiglu(h, load_wa, load_wb, w_out, width):
    def up(c):
        sl = slice(c * MXU_COLS, (c + 1) * MXU_COLS)
        return (jnp.dot(h, load_wa(sl), preferred_element_type=F32),
                jnp.dot(h, load_wb(sl), preferred_element_type=F32))

    n_chunks = width // MXU_COLS
    acts = []
    nxt = up(0)
    for c in range(n_chunks):
        a, b = nxt
        if c + 1 < n_chunks:
            nxt = up(c + 1)
        acts.append((a * _sigmoid(a) * b).astype(BF16))
    return jnp.dot(jnp.concatenate(acts, axis=1), w_out, preferred_element_type=F32)


def _ffn_kernel(x_ref, g_ref, sh_ref, sc_ref, gate_ref, wa_ref, wb_ref, wo_ref, o_ref):
    x = x_ref[...]
    h = _rms_mod(x, g_ref[...], sh_ref[0], sc_ref[0]).astype(BF16)
    y = _swiglu(h, lambda sl: wa_ref[:, sl], lambda sl: wb_ref[:, sl], wo_ref[...], wo_ref.shape[0])
    o_ref[...] = x + gate_ref[0] * y


def _dense_ffn(x, gain, shift, scale, gate, w_in, w_out, seq):
    n, d = x.shape
    ff = w_out.shape[0]
    tm = min(ROW_TILE, seq)
    bsz = shift.shape[0]
    vec = lambda i: (i * tm // seq, 0, 0)
    resident = pl.Buffered(1)
    return pl.pallas_call(
        _ffn_kernel,
        out_shape=jax.ShapeDtypeStruct((n, d), F32),
        grid=(n // tm,),
        in_specs=[
            pl.BlockSpec((tm, d), lambda i: (i, 0)),
            pl.BlockSpec((1, d), lambda i: (0, 0)),
            pl.BlockSpec((1, 1, d), vec),
            pl.BlockSpec((1, 1, d), vec),
            pl.BlockSpec((1, 1, d), vec),
            pl.BlockSpec((d, ff), lambda i: (0, 0), pipeline_mode=resident),
            pl.BlockSpec((d, ff), lambda i: (0, 1), pipeline_mode=resident),
            pl.BlockSpec((ff, d), lambda i: (0, 0), pipeline_mode=resident),
        ],
        out_specs=pl.BlockSpec((tm, d), lambda i: (i, 0)),
        compiler_params=_cparams(("parallel",)),
        name="dense_ffn",
    )(x, gain.reshape(1, d), shift.reshape(bsz, 1, d), scale.reshape(bsz, 1, d),
      gate.reshape(bsz, 1, d), w_in, w_in, w_out)


def _ret_kernel(q_ref, k_ref, v_ref, g_ref, gn_ref, dm_ref, qd_ref, kd_ref, cd_ref, o_ref, *, nblk):
    dmat = dm_ref[0]
    qdec = qd_ref[0]
    kdec = kd_ref[0]
    cdec = cd_ref[0, 0:1, :]
    gn = gn_ref[...]

    def load(t):
        sl = slice(t * RET_BLOCK, (t + 1) * RET_BLOCK)
        return q_ref[0, sl, :], k_ref[0, sl, :], v_ref[0, sl, :]

    def scores(q, k):
        return lax.dot_general(q, k, (((1,), (1,)), ((), ())), preferred_element_type=F32)

    state = jnp.zeros((RET_DK, RET_DV), F32)
    q, k, v = load(0)
    s = scores(q, k)
    for t in range(nblk):
        qd = (q.astype(F32) * qdec).astype(BF16)
        cross = jnp.dot(qd, state.astype(BF16), preferred_element_type=F32)
        kdt = jnp.transpose(k.astype(F32) * kdec).astype(BF16)
        update = jnp.dot(kdt, v, preferred_element_type=F32)
        if t + 1 < nblk:
            q, k, v_next = load(t + 1)
            s_next = scores(q, k)
        intra = jnp.dot((s * dmat).astype(BF16), v, preferred_element_type=F32)
        state = state * cdec + update
        o = intra + cross
        mu = jnp.mean(o, axis=-1, keepdims=True)
        oc = o - mu
        var = jnp.mean(oc * oc, axis=-1, keepdims=True)
        on = (oc * lax.rsqrt(var + EPS)) * gn
        sl = slice(t * RET_BLOCK, (t + 1) * RET_BLOCK)
        g = g_ref[0, sl, :].astype(F32)
        o_ref[0, sl, :] = ((g * _sigmoid(g)) * on).astype(o_ref.dtype)
        if t + 1 < nblk:
            v, s = v_next, s_next


def _retention_tables():
    t = RET_BLOCK
    log_gamma = jnp.log(1.0 - jnp.exp2(-5.0 - jnp.arange(RET_HEADS, dtype=F32)))
    pos = jnp.arange(t, dtype=F32)
    chunk = jnp.arange(t) // CHUNK
    dist = jnp.abs(pos[:, None] - pos[None, :])
    visible = (chunk[None, :] <= chunk[:, None]).astype(F32)
    dmat = jnp.exp(dist[None] * log_gamma[:, None, None]) * visible[None]
    qdec = jnp.exp((pos[None, :] + 1.0) * log_gamma[:, None])
    kdec = jnp.exp((t - 1.0 - pos)[None, :] * log_gamma[:, None])
    cdec = jnp.exp(t * log_gamma)
    qdec = jnp.broadcast_to(qdec[:, :, None], (RET_HEADS, t, RET_DK))
    kdec = jnp.broadcast_to(kdec[:, :, None], (RET_HEADS, t, RET_DK))
    cdec = jnp.broadcast_to(cdec[:, None, None], (RET_HEADS, 8, RET_DV))
    return dmat, qdec, kdec, cdec


def _retention_core(proj, gn_w, tables, bsz, seq):
    dmat, qdec, kdec, cdec = tables
    h = RET_HEADS
    t = RET_BLOCK
    kq = h * RET_DK // RET_DK
    kv = 2 * h * RET_DK // RET_DV
    return pl.pallas_call(
        functools.partial(_ret_kernel, nblk=seq // t),
        out_shape=jax.ShapeDtypeStruct((bsz, seq, h * RET_DV), BF16),
        grid=(bsz, h),
        in_specs=[
            pl.BlockSpec((1, seq, RET_DK), lambda b, i: (b, 0, i)),
            pl.BlockSpec((1, seq, RET_DK), lambda b, i: (b, 0, kq + i)),
            pl.BlockSpec((1, seq, RET_DV), lambda b, i: (b, 0, kv + i)),
            pl.BlockSpec((1, seq, RET_DV), lambda b, i: (b, 0, kv + h + i)),
            pl.BlockSpec((1, RET_DV), lambda b, i: (0, i)),
            pl.BlockSpec((1, t, t), lambda b, i: (i, 0, 0)),
            pl.BlockSpec((1, t, RET_DK), lambda b, i: (i, 0, 0)),
            pl.BlockSpec((1, t, RET_DK), lambda b, i: (i, 0, 0)),
            pl.BlockSpec((1, 8, RET_DV), lambda b, i: (i, 0, 0)),
        ],
        out_specs=pl.BlockSpec((1, seq, RET_DV), lambda b, i: (b, 0, i)),
        compiler_params=_cparams(("parallel", "parallel")),
        name="retention_core",
    )(proj, proj, proj, proj, gn_w.reshape(1, h * RET_DV), dmat, qdec, kdec, cdec)


def _sb_kernel(q_ref, k_ref, v_ref, o_ref):
    t = SB_TILE
    w = 2 * SB_DH
    pairs = SB_HEADS // 2
    qi = pl.program_id(1)
    lane = lax.broadcasted_iota(jnp.int32, (t, w), 1)
    row = lax.broadcasted_iota(jnp.int32, (t, 2 * t), 0)
    col = lax.broadcasted_iota(jnp.int32, (t, 2 * t), 1)
    diag_causal = jnp.where(col >= t, col - t, col) < row
    uj = lax.broadcasted_iota(jnp.int32, (t, 2 * t), 0)
    us = lax.broadcasted_iota(jnp.int32, (t, 2 * t), 1)
    umat = jnp.where((uj >= us) | (us >= t), 1.0, 0.0).astype(BF16)
    scale = jnp.asarray(SB_DH ** -0.5, BF16)

    def split_heads(x):
        zero = jnp.zeros_like(x)
        return jnp.concatenate([jnp.where(lane < SB_DH, x, zero), jnp.where(lane >= SB_DH, x, zero)], axis=0)

    def key_tile(kb, rs, accs, causal):
        ks = pl.ds(pl.multiple_of(kb * t, t), t)
        zs = []
        for p in range(pairs):
            qp = q_ref[0, :, p * w:(p + 1) * w] * scale
            kcat = split_heads(k_ref[0, ks, p * w:(p + 1) * w])
            zs.append(lax.dot_general(qp, kcat, (((1,), (1,)), ((), ())), preferred_element_type=F32))
        his, los = [], []
        for p in range(pairs):
            z = zs[p]
            sp = jnp.maximum(z, 0.0) + jnp.log(1.0 + jnp.exp(-jnp.abs(z)))
            lom = -sp if causal is None else jnp.where(causal, -sp, 0.0)
            hi = lom.astype(BF16)
            his.append(hi)
            los.append((lom - hi.astype(F32)).astype(BF16))
        incls, new_rs = [], []
        for p in range(pairs):
            cs = [jnp.dot(his[p][:, h * t:(h + 1) * t], umat, preferred_element_type=F32)
                  + jnp.dot(los[p][:, h * t:(h + 1) * t], umat, preferred_element_type=F32) for h in range(2)]
            incls.append(jnp.concatenate([cs[0][:, :t], cs[1][:, :t]], axis=1))
            new_rs.append(rs[p] + jnp.concatenate([cs[0][:, t:], cs[1][:, t:]], axis=1))
        avs = []
        for p in range(pairs):
            a = jnp.exp(zs[p] + incls[p] + rs[p])
            if causal is not None:
                a = jnp.where(causal, a, 0.0)
            avs.append(a.astype(BF16))
        new_accs = []
        for p in range(pairs):
            vcat = split_heads(v_ref[0, ks, p * w:(p + 1) * w])
            new_accs.append(accs[p] + jnp.dot(avs[p], vcat, preferred_element_type=F32))
        return tuple(new_rs), tuple(new_accs)

    zeros_r = tuple(jnp.zeros((t, 2 * t), F32) for _ in range(pairs))
    zeros_a = tuple(jnp.zeros((t, w), F32) for _ in range(pairs))
    rs, accs = key_tile(qi, zeros_r, zeros_a, diag_causal)

    def cond(carry):
        kb, rs, _ = carry
        return jnp.logical_and(kb >= 0, jnp.max(functools.reduce(jnp.maximum, rs)) > SB_LOG_FLOOR)

    def body(carry):
        kb, rs, accs = carry
        rs, accs = key_tile(kb, rs, accs, None)
        return kb - 1, rs, accs

    _, _, accs = lax.while_loop(cond, body, (qi - 1, rs, accs))
    for p in range(pairs):
        o_ref[0, :, p * w:(p + 1) * w] = accs[p].astype(o_ref.dtype)


def _stick_breaking_core(q, kv, bsz, seq):
    t = SB_TILE
    return pl.pallas_call(
        _sb_kernel,
        out_shape=jax.ShapeDtypeStruct((bsz, seq, D_MODEL), BF16),
        grid=(bsz, seq // t),
        in_specs=[
            pl.BlockSpec((1, t, D_MODEL), lambda b, i: (b, i, 0)),
            pl.BlockSpec((1, seq, D_MODEL), lambda b, i: (b, 0, 0)),
            pl.BlockSpec((1, seq, D_MODEL), lambda b, i: (b, 0, 1)),
        ],
        out_specs=pl.BlockSpec((1, t, D_MODEL), lambda b, i: (b, i, 0)),
        compiler_params=_cparams(("parallel", "parallel")),
        name="stick_breaking_core",
    )(q, kv, kv)


def _route_kernel(x_ref, g_ref, sh_ref, sc_ref, r_ref, h_ref, meta_ref, meta_t_ref, pc_ref):
    tb = MOE_BLOCK
    h32 = _rms_mod(x_ref[...], g_ref[...], sh_ref[0], sc_ref[0])
    h_ref[...] = h32.astype(BF16)
    logits = jnp.dot(h32, r_ref[...], preferred_element_type=F32, precision=lax.Precision.HIGHEST)
    lane = lax.broadcasted_iota(jnp.int32, (tb, LANES), 1)
    lane_f = lane.astype(F32)
    neg = jnp.float32(-jnp.inf)
    lg = jnp.where(lane < N_EXPERTS, logits, neg)
    m1 = jnp.max(lg, axis=-1, keepdims=True)
    i1 = jnp.min(jnp.where(lg == m1, lane_f, float(LANES)), axis=-1, keepdims=True)
    oh1 = lane_f == i1
    lg2 = jnp.where(oh1, neg, lg)
    m2 = jnp.max(lg2, axis=-1, keepdims=True)
    i2 = jnp.min(jnp.where(lg2 == m2, lane_f, float(LANES)), axis=-1, keepdims=True)
    oh2 = lane_f == i2
    e = jnp.exp(m2 - m1)
    w1 = 1.0 / (1.0 + e)
    w2 = e / (1.0 + e)

    tr = lax.broadcasted_iota(jnp.int32, (tb, tb), 0)
    tc = lax.broadcasted_iota(jnp.int32, (tb, tb), 1)
    lower = jnp.where(tc < tr, 1.0, 0.0).astype(BF16)
    oh1f = jnp.where(oh1, 1.0, 0.0)
    oh2f = jnp.where(oh2, 1.0, 0.0)
    cum1 = jnp.dot(lower, oh1f.astype(BF16), preferred_element_type=F32)
    cum2 = jnp.dot(lower, oh2f.astype(BF16), preferred_element_type=F32)
    cnt1 = jnp.sum(oh1f, axis=0, keepdims=True)
    cnt2 = jnp.sum(oh2f, axis=0, keepdims=True)
    pc = jnp.floor((cnt1 + cnt2 + (MOE_GROUP - 1.0)) / MOE_GROUP) * MOE_GROUP
    pj = lax.broadcasted_iota(jnp.int32, (LANES, LANES), 0)
    ps = lax.broadcasted_iota(jnp.int32, (LANES, LANES), 1)
    before = jnp.where(pj < ps, 1.0, 0.0)
    off = jnp.dot(jnp.broadcast_to(pc, (8, LANES)), before, preferred_element_type=F32,
                  precision=lax.Precision.HIGHEST)[0:1]
    pos1 = jnp.sum(jnp.where(oh1, off + cum1, 0.0), axis=-1, keepdims=True)
    pos2 = jnp.sum(jnp.where(oh2, off + cnt1 + cum2, 0.0), axis=-1, keepdims=True)
    meta = jnp.where(lane == 0, pos1,
                     jnp.where(lane == 1, pos2,
                               jnp.where(lane == 2, w1, jnp.where(lane == 3, w2, 0.0))))
    meta_ref[...] = meta
    meta_t_ref[0] = jnp.transpose(meta)[0:8]
    pc_ref[0] = jnp.broadcast_to(pc, (8, LANES))


def _moe_route(x, gain, shift, scale, router, seq):
    n, d = x.shape
    tb = MOE_BLOCK
    nb = n // tb
    bsz = shift.shape[0]
    r_pad = jnp.zeros((d, LANES), F32).at[:, :N_EXPERTS].set(router)
    vec = lambda i: (i * tb // seq, 0, 0)
    return pl.pallas_call(
        _route_kernel,
        out_shape=(
            jax.ShapeDtypeStruct((n, d), BF16),
            jax.ShapeDtypeStruct((n, LANES), F32),
            jax.ShapeDtypeStruct((nb, 8, tb), F32),
            jax.ShapeDtypeStruct((nb, 8, LANES), F32),
        ),
        grid=(nb,),
        in_specs=[
            pl.BlockSpec((tb, d), lambda i: (i, 0)),
            pl.BlockSpec((1, d), lambda i: (0, 0)),
            pl.BlockSpec((1, 1, d), vec),
            pl.BlockSpec((1, 1, d), vec),
            pl.BlockSpec((d, LANES), lambda i: (0, 0)),
        ],
        out_specs=(
            pl.BlockSpec((tb, d), lambda i: (i, 0)),
            pl.BlockSpec((tb, LANES), lambda i: (i, 0)),
            pl.BlockSpec((1, 8, tb), lambda i: (i, 0, 0)),
            pl.BlockSpec((1, 8, LANES), lambda i: (i, 0, 0)),
        ),
        compiler_params=_cparams(("parallel",)),
        name="moe_route",
    )(x, gain.reshape(1, d), shift.reshape(bsz, 1, d), scale.reshape(bsz, 1, d), r_pad)


def _segment_copies(loc_ref, glb_ref, len_ref, blk, make_copy, wait):
    for e in range(N_EXPERTS):
        idx = blk * N_EXPERTS + e
        loc = loc_ref[idx]
        glb = glb_ref[idx]
        length = len_ref[idx]
        done = jnp.int32(0)
        for sz in SEG_SIZES:
            take = (length & sz) != 0

            @pl.when(take)
            def _(loc=loc, glb=glb, done=done, sz=sz):
                cp = make_copy(pl.multiple_of(loc + done, MOE_GROUP),
                               pl.multiple_of(glb + done, MOE_GROUP), sz)
                if wait:
                    cp.wait()
                else:
                    cp.start()

            done = done + jnp.where(take, sz, 0)


def _dispatch_kernel(loc_ref, glb_ref, len_ref, h_ref, meta_t_ref, hs_in_ref, hs_ref, buf_ref, sem):
    del hs_in_ref
    b = pl.program_id(0)
    tb = MOE_BLOCK
    pos1 = meta_t_ref[0, 0:1, :]
    pos2 = meta_t_ref[0, 1:2, :]
    r = lax.broadcasted_iota(jnp.int32, (MOE_CAP, tb), 0).astype(F32)
    perm = jnp.where((r == pos1) | (r == pos2), 1.0, 0.0).astype(BF16)
    buf_ref[...] = jnp.dot(perm, h_ref[...], preferred_element_type=F32).astype(BF16)

    def make_copy(loc, glb, sz):
        return pltpu.make_async_copy(buf_ref.at[pl.ds(loc, sz)], hs_ref.at[pl.ds(glb, sz)], sem)

    _segment_copies(loc_ref, glb_ref, len_ref, b, make_copy, wait=False)
    _segment_copies(loc_ref, glb_ref, len_ref, b, make_copy, wait=True)


def _moe_dispatch(h, meta_t, loc, glb, seg_len, rows_total):
    n, d = h.shape
    tb = MOE_BLOCK
    nb = n // tb
    hs0 = jnp.zeros((rows_total, d), BF16)
    return pl.pallas_call(
        _dispatch_kernel,
        out_shape=jax.ShapeDtypeStruct((rows_total, d), BF16),
        grid_spec=pltpu.PrefetchScalarGridSpec(
            num_scalar_prefetch=3,
            grid=(nb,),
            in_specs=[
                pl.BlockSpec((tb, d), lambda i, *_: (i, 0)),
                pl.BlockSpec((1, 8, tb), lambda i, *_: (i, 0, 0)),
                pl.BlockSpec(memory_space=pl.ANY),
            ],
            out_specs=pl.BlockSpec(memory_space=pl.ANY),
            scratch_shapes=[pltpu.VMEM((MOE_CAP, d), BF16), pltpu.SemaphoreType.DMA],
        ),
        input_output_aliases={5: 0},
        compiler_params=_cparams(("arbitrary",)),
        name="moe_dispatch",
    )(loc, glb, seg_len, h, meta_t, hs0)


def _expert_kernel(te_ref, tx_ref, tv_ref, x_ref, wa_ref, wb_ref, wo_ref, o_ref, acc_ref):
    i = pl.program_id(0)
    f = pl.program_id(1)

    last = pl.num_programs(1) - 1

    @pl.when(tv_ref[i] == 1)
    def _():
        y = _swiglu(x_ref[...], lambda sl: wa_ref[0, :, sl], lambda sl: wb_ref[0, :, sl], wo_ref[0],
                    wo_ref.shape[1])

        @pl.when(f == 0)
        def _():
            acc_ref[...] = y

        @pl.when(jnp.logical_and(f > 0, f < last))
        def _():
            acc_ref[...] += y

        @pl.when(f == last)
        def _():
            o_ref[...] = (acc_ref[...] + y).astype(o_ref.dtype)

    @pl.when(jnp.logical_and(tv_ref[i] == 0, f == pl.num_programs(1) - 1))
    def _():
        o_ref[...] = jnp.zeros_like(o_ref)


def _moe_experts(hs, w_in, w_out, tile_expert, tile_row, tile_valid):
    rows, d = hs.shape
    tm = ROW_TILE
    tf = FF_EXPERT_TILE
    nf = FF_EXPERT // tf
    nt = rows // tm

    def f_eff(i, f, tv):
        return jnp.where(tv[i] == 1, f, nf - 1)

    return pl.pallas_call(
        _expert_kernel,
        out_shape=jax.ShapeDtypeStruct((rows, d), BF16),
        grid_spec=pltpu.PrefetchScalarGridSpec(
            num_scalar_prefetch=3,
            grid=(nt, nf),
            in_specs=[
                pl.BlockSpec((tm, d), lambda i, f, te, tx, tv: (tx[i], 0)),
                pl.BlockSpec((1, d, tf), lambda i, f, te, tx, tv: (te[i], 0, f_eff(i, f, tv))),
                pl.BlockSpec((1, d, tf), lambda i, f, te, tx, tv: (te[i], 0, nf + f_eff(i, f, tv))),
                pl.BlockSpec((1, tf, d), lambda i, f, te, tx, tv: (te[i], f_eff(i, f, tv), 0)),
            ],
            out_specs=pl.BlockSpec((tm, d), lambda i, f, te, tx, tv: (i, 0)),
            scratch_shapes=[pltpu.VMEM((tm, d), F32)],
        ),
        compiler_params=_cparams(("arbitrary", "arbitrary")),
        name="moe_experts",
    )(tile_expert, tile_row, tile_valid, hs, w_in, w_in, w_out)


def _combine_kernel(loc_ref, glb_ref, len_ref, x_ref, gate_ref, meta_ref, ys_ref, o_ref, buf_ref, sems):
    b = pl.program_id(0)
    tb = MOE_BLOCK

    def fetch(blk, slot, wait):
        def make_copy(loc, glb, sz):
            return pltpu.make_async_copy(ys_ref.at[pl.ds(glb, sz)], buf_ref.at[slot, pl.ds(loc, sz)],
                                         sems.at[slot])

        _segment_copies(loc_ref, glb_ref, len_ref, blk, make_copy, wait)

    def start(blk, slot):
        buf_ref[slot] = jnp.zeros(buf_ref.shape[1:], buf_ref.dtype)
        fetch(blk, slot, wait=False)

    @pl.when(b == 0)
    def _():
        start(0, 0)

    @pl.when(b + 1 < pl.num_programs(0))
    def _():
        start(b + 1, (b + 1) % 2)

    slot = b % 2
    meta = meta_ref[...]
    pos1 = meta[:, 0:1]
    pos2 = meta[:, 1:2]
    w1 = meta[:, 2:3]
    w2 = meta[:, 3:4]
    r = lax.broadcasted_iota(jnp.int32, (tb, MOE_CAP), 1).astype(F32)
    sel1 = jnp.where(r == pos1, 1.0, 0.0).astype(BF16)
    sel2 = jnp.where(r == pos2, 1.0, 0.0).astype(BF16)
    fetch(b, slot, wait=True)
    ys = buf_ref[slot]
    y = (w1 * jnp.dot(sel1, ys, preferred_element_type=F32)
         + w2 * jnp.dot(sel2, ys, preferred_element_type=F32))
    o_ref[...] = x_ref[...] + gate_ref[0] * y


def _moe_combine(x, gate, meta, ys, loc, glb, seg_len, seq):
    n, d = x.shape
    tb = MOE_BLOCK
    bsz = gate.shape[0]
    return pl.pallas_call(
        _combine_kernel,
        out_shape=jax.ShapeDtypeStruct((n, d), F32),
        grid_spec=pltpu.PrefetchScalarGridSpec(
            num_scalar_prefetch=3,
            grid=(n // tb,),
            in_specs=[
                pl.BlockSpec((tb, d), lambda i, *_: (i, 0)),
                pl.BlockSpec((1, 1, d), lambda i, *_: (i * tb // seq, 0, 0)),
                pl.BlockSpec((tb, LANES), lambda i, *_: (i, 0)),
                pl.BlockSpec(memory_space=pl.ANY),
            ],
            out_specs=pl.BlockSpec((tb, d), lambda i, *_: (i, 0)),
            scratch_shapes=[pltpu.VMEM((2, MOE_CAP, d), BF16), pltpu.SemaphoreType.DMA((2,))],
        ),
        compiler_params=_cparams(("arbitrary",)),
        name="moe_combine",
    )(loc, glb, seg_len, x, gate.reshape(bsz, 1, d), meta, ys)


def _moe_layout(pc):
    nb = pc.shape[0]
    tm = ROW_TILE
    n_tiles = (nb * (2 * MOE_BLOCK + N_EXPERTS * (MOE_GROUP - 1)) + tm - 1) // tm + N_EXPERTS
    loc = jnp.cumsum(pc, axis=1) - pc
    tiles_e = (jnp.sum(pc, axis=0) + tm - 1) // tm
    tile_end = jnp.cumsum(tiles_e)
    base_e = (tile_end - tiles_e) * tm
    glb = base_e[None, :] + jnp.cumsum(pc, axis=0) - pc
    n_valid = tile_end[-1]
    ids = jnp.arange(n_tiles, dtype=jnp.int32)
    tile_row = jnp.minimum(ids, n_valid - 1)
    tile_expert = jnp.sum((tile_row[:, None] >= tile_end[None, :]).astype(jnp.int32), axis=1)
    tile_valid = (ids < n_valid).astype(jnp.int32)
    flat = lambda a: a.reshape(-1).astype(jnp.int32)
    return flat(loc), flat(glb), flat(pc), tile_expert.astype(jnp.int32), tile_row.astype(jnp.int32), tile_valid, n_tiles * tm


def _moe_block(x, gain, shift, scale, gate, router, w_in, w_out, layer, seq):
    h, meta, meta_t, pc = _moe_route(x, gain, shift, scale, router, seq)
    pc_i = pc[:, 0, :N_EXPERTS].astype(jnp.int32)
    loc, glb, seg_len, tile_expert, tile_row, tile_valid, rows_total = _moe_layout(pc_i)
    hs = _moe_dispatch(h, meta_t, loc, glb, seg_len, rows_total)
    ys = _moe_experts(hs, w_in, w_out, tile_expert + layer * N_EXPERTS, tile_row, tile_valid)
    return _moe_combine(x, gate, meta, ys, loc, glb, seg_len, seq)


def _final_norm_kernel(x_ref, g_ref, o_ref):
    x = x_ref[...]
    o_ref[...] = (x * lax.rsqrt(jnp.mean(x * x, axis=-1, keepdims=True) + EPS)) * g_ref[...]


def _final_norm(x, gain):
    n, d = x.shape
    tm = min(ROW_TILE, n)
    return pl.pallas_call(
        _final_norm_kernel,
        out_shape=jax.ShapeDtypeStruct((n, d), F32),
        grid=(n // tm,),
        in_specs=[pl.BlockSpec((tm, d), lambda i: (i, 0)), pl.BlockSpec((1, d), lambda i: (0, 0))],
        out_specs=pl.BlockSpec((tm, d), lambda i: (i, 0)),
        compiler_params=_cparams(("parallel",)),
        name="final_norm",
    )(x, gain.reshape(1, d))


def kernel(x, c, positions, ada_w, ada_b, norm_mix, norm_ff, ret_w_in, ret_gn, ret_w_out, kv_ada_w, kv_ada_b, kv_norm, kv_w, sb_w_q, sb_w_out, ff_w_in, ff_w_out, moe_router, moe_w_in, moe_w_out, final_norm):
    bsz, seq, d = x.shape
    n = bsz * seq
    mod = _ada_mod(c, ada_w, ada_b)
    kv_mod = _ada_mod(c, kv_ada_w[None], kv_ada_b[None])[0]
    rope = _rope_tables(positions)
    ret_tables = _retention_tables()
    moe_in = moe_w_in.astype(BF16).reshape(-1, d, 2 * FF_EXPERT)
    moe_out = moe_w_out.astype(BF16).reshape(-1, FF_EXPERT, d)
    xf = x.reshape(n, d)
    kv = None
    for i in range(DEPTH):
        sh_m, sc_m, g_m, sh_f, sc_f, g_f = [mod[i, :, p * d:(p + 1) * d] for p in range(6)]
        if i < N_A:
            proj = _norm_mod_matmul(xf, norm_mix[i], sh_m, sc_m, ret_w_in[i].astype(BF16), seq, rope=rope)
            o = _retention_core(proj.reshape(bsz, seq, -1), ret_gn[i], ret_tables, bsz, seq)
            xf = _matmul_residual(o.reshape(n, -1), ret_w_out[i].astype(BF16), xf, g_m, seq)
        else:
            j = i - N_A
            q = _norm_mod_matmul(xf, norm_mix[i], sh_m, sc_m, sb_w_q[j].astype(BF16), seq)
            o = _stick_breaking_core(q.reshape(bsz, seq, d), kv, bsz, seq)
            xf = _matmul_residual(o.reshape(n, d), sb_w_out[j].astype(BF16), xf, g_m, seq)
        if i % 2 == 0:
            xf = _dense_ffn(xf, norm_ff[i], sh_f, sc_f, g_f, ff_w_in[i // 2].astype(BF16),
                            ff_w_out[i // 2].astype(BF16), seq)
        else:
            xf = _moe_block(xf, norm_ff[i], sh_f, sc_f, g_f, moe_router[i // 2],
                            moe_in, moe_out, i // 2, seq)
        if i == N_A - 1:
            kv = _norm_mod_matmul(xf, kv_norm, kv_mod[:, :d], kv_mod[:, d:], kv_w.astype(BF16), seq)
            kv = kv.reshape(bsz, seq, 2 * d)
    return _final_norm(xf, final_norm).reshape(bsz, seq, d)
```

```python
import functools

import jax
import jax.numpy as jnp
from jax import lax
from jax.experimental import pallas as pl
from jax.experimental.pallas import tpu as pltpu

F32 = jnp.float32
BF16 = jnp.bfloat16

D_MODEL = 1024
DEPTH = 4
N_A = DEPTH // 2
CHUNK = 64
RET_HEADS = 8
RET_DK = D_MODEL // RET_HEADS
RET_DV = 2 * D_MODEL // RET_HEADS
ROPE_BASE = 10000.0
SB_HEADS = 16
SB_DH = D_MODEL // SB_HEADS
FF_DENSE = 2816
N_EXPERTS = 8
FF_EXPERT = 3584
EPS = 1e-6

LANES = 128
BF16_ROWS = 16
MXU_COLS = 256
VMEM_LIMIT = 56 * 1024 * 1024

ROW_TILE = 1024
NMM_ROWS = 512
NMM_CHUNK = 512
FF_EXPERT_TILE = 1792
RET_BLOCK = 256
SB_TILE = 128
MOE_BLOCK = 512
MOE_GROUP = BF16_ROWS
MOE_CAP = 2 * MOE_BLOCK + N_EXPERTS * MOE_GROUP
SEG_SIZES = (512, 256, 128, 64, 32, 16)
SB_LOG_FLOOR = -110.0


def _cparams(sem):
    return pltpu.CompilerParams(dimension_semantics=sem, vmem_limit_bytes=VMEM_LIMIT)


def _sigmoid(v):
    return 1.0 / (1.0 + jnp.exp(-v))


def _rms_mod(x, gain, shift, scale):
    y = x * lax.rsqrt(jnp.mean(x * x, axis=-1, keepdims=True) + EPS)
    return (y * gain) * (1.0 + scale) + shift


def _ada_kernel(c_ref, w_ref, b_ref, o_ref):
    c = c_ref[...]
    cond = (c * _sigmoid(c)).astype(BF16)
    o_ref[0] = jnp.dot(cond, w_ref[0].astype(BF16), preferred_element_type=F32) + b_ref[0]


def _ada_mod(c, w, b):
    nl, d, n = w.shape
    bsz = c.shape[0]
    tn = 1024
    return pl.pallas_call(
        _ada_kernel,
        out_shape=jax.ShapeDtypeStruct((nl, bsz, n), F32),
        grid=(nl, n // tn),
        in_specs=[
            pl.BlockSpec((bsz, d), lambda l, j: (0, 0)),
            pl.BlockSpec((1, d, tn), lambda l, j: (l, 0, j)),
            pl.BlockSpec((1, 1, tn), lambda l, j: (l, 0, j)),
        ],
        out_specs=pl.BlockSpec((1, bsz, tn), lambda l, j: (l, 0, j)),
        compiler_params=_cparams(("parallel", "parallel")),
        name="ada_mod",
    )(c, w, b.reshape(nl, 1, n))


def _rope_table_kernel(pos_ref, freq_ref, cs_ref, sn_ref):
    ang = pos_ref[...] * freq_ref[...]
    lane = lax.broadcasted_iota(jnp.int32, ang.shape, 1)
    cs_ref[...] = jnp.cos(ang)
    sn_ref[...] = jnp.where(lane < RET_DK // 2, -jnp.sin(ang), jnp.sin(ang))


def _rope_tables(positions):
    n = positions.size
    inv_freq = jnp.power(ROPE_BASE, -jnp.arange(0, RET_DK, 2, dtype=F32) / RET_DK)
    freq = jnp.concatenate([inv_freq, inv_freq]).reshape(1, RET_DK)
    pos = jnp.broadcast_to(positions.astype(F32).reshape(n, 1), (n, RET_DK))
    tr = min(n, 2048)
    return pl.pallas_call(
        _rope_table_kernel,
        out_shape=(jax.ShapeDtypeStruct((n, RET_DK), F32),) * 2,
        grid=(n // tr,),
        in_specs=[pl.BlockSpec((tr, RET_DK), lambda i: (i, 0)),
                  pl.BlockSpec((1, RET_DK), lambda i: (0, 0))],
        out_specs=(pl.BlockSpec((tr, RET_DK), lambda i: (i, 0)),) * 2,
        compiler_params=_cparams(("parallel",)),
        name="rope_tables",
    )(pos, freq)


def _nmm_kernel(x_ref, g_ref, sh_ref, sc_ref, w_ref, *rest, rope_cols):
    if rope_cols:
        cs_ref, sn_ref, o_ref = rest
    else:
        (o_ref,) = rest
    cw = NMM_CHUNK
    n_chunks = w_ref.shape[1] // cw
    h = _rms_mod(x_ref[...], g_ref[...], sh_ref[0], sc_ref[0]).astype(BF16)

    def mm(c):
        return jnp.dot(h, w_ref[:, c * cw:(c + 1) * cw], preferred_element_type=F32)

    nxt = mm(0)
    for c in range(n_chunks):
        r = nxt
        if c + 1 < n_chunks:
            nxt = mm(c + 1)
        col0 = c * cw
        if col0 >= rope_cols:
            o_ref[:, col0:col0 + cw] = r.astype(o_ref.dtype)
            continue
        k_scale = RET_DK ** -0.5 if col0 >= rope_cols // 2 else None
        for s in range(cw // RET_DK):
            t = r[:, s * RET_DK:(s + 1) * RET_DK]
            rot = t * cs_ref[...] + pltpu.roll(t, RET_DK // 2, 1) * sn_ref[...]
            if k_scale is not None:
                rot = rot * k_scale
            o_ref[:, col0 + s * RET_DK:col0 + (s + 1) * RET_DK] = rot.astype(o_ref.dtype)


def _norm_mod_matmul(x, gain, shift, scale, w, seq, rope=None):
    n, d = x.shape
    m = w.shape[1]
    tm = min(NMM_ROWS, seq)
    bsz = shift.shape[0]
    vec = lambda i: (i * tm // seq, 0, 0)
    in_specs = [
        pl.BlockSpec((tm, d), lambda i: (i, 0)),
        pl.BlockSpec((1, d), lambda i: (0, 0)),
        pl.BlockSpec((1, 1, d), vec),
        pl.BlockSpec((1, 1, d), vec),
        pl.BlockSpec((d, m), lambda i: (0, 0), pipeline_mode=pl.Buffered(1)),
    ]
    args = [x, gain.reshape(1, d), shift.reshape(bsz, 1, d), scale.reshape(bsz, 1, d), w]
    rope_cols = 0
    if rope is not None:
        rope_cols = 2 * RET_HEADS * RET_DK
        in_specs += [pl.BlockSpec((tm, RET_DK), lambda i: (i, 0))] * 2
        args += list(rope)
    return pl.pallas_call(
        functools.partial(_nmm_kernel, rope_cols=rope_cols),
        out_shape=jax.ShapeDtypeStruct((n, m), BF16),
        grid=(n // tm,),
        in_specs=in_specs,
        out_specs=pl.BlockSpec((tm, m), lambda i: (i, 0)),
        compiler_params=_cparams(("parallel",)),
        name="norm_mod_matmul_rope" if rope_cols else "norm_mod_matmul",
    )(*args)


def _mm_res_kernel(a_ref, w_ref, x_ref, gate_ref, o_ref):
    y = jnp.dot(a_ref[...], w_ref[...], preferred_element_type=F32)
    o_ref[...] = x_ref[...] + gate_ref[0] * y


def _matmul_residual(a, w, x, gate, seq):
    n, k = a.shape
    d = w.shape[1]
    tm = min(512, seq)
    bsz = gate.shape[0]
    return pl.pallas_call(
        _mm_res_kernel,
        out_shape=jax.ShapeDtypeStruct((n, d), F32),
        grid=(n // tm,),
        in_specs=[
            pl.BlockSpec((tm, k), lambda i: (i, 0)),
            pl.BlockSpec((k, d), lambda i: (0, 0)),
            pl.BlockSpec((tm, d), lambda i: (i, 0)),
            pl.BlockSpec((1, 1, d), lambda i: (i * tm // seq, 0, 0)),
        ],
        out_specs=pl.BlockSpec((tm, d), lambda i: (i, 0)),
        compiler_params=_cparams(("parallel",)),
        name="matmul_residual",
    )(a, w, x, gate.reshape(bsz, 1, d))


def _swiglu(h, load_wa, load_wb, w_out, width):
    def up(c):
        sl = slice(c * MXU_COLS, (c + 1) * MXU_COLS)
        return (jnp.dot(h, load_wa(sl), preferred_element_type=F32),
                jnp.dot(h, load_wb(sl), preferred_element_type=F32))

    n_chunks = width // MXU_COLS
    acts = []
    nxt = up(0)
    for c in range(n_chunks):
        a, b = nxt
        if c + 1 < n_chunks:
            nxt = up(c + 1)
        acts.append((a * _sigmoid(a) * b).astype(BF16))
    return jnp.dot(jnp.concatenate(acts, axis=1), w_out, preferred_element_type=F32)


def _ffn_kernel(x_ref, g_ref, sh_ref, sc_ref, gate_ref, wa_ref, wb_ref, wo_ref, o_ref):
    x = x_ref[...]
    h = _rms_mod(x, g_ref[...], sh_ref[0], sc_ref[0]).astype(BF16)
    y = _swiglu(h, lambda sl: wa_ref[:, sl], lambda sl: wb_ref[:, sl], wo_ref[...], wo_ref.shape[0])
    o_ref[...] = x + gate_ref[0] * y


def _dense_ffn(x, gain, shift, scale, gate, w_in, w_out, seq):
    n, d = x.shape
    ff = w_out.shape[0]
    tm = min(ROW_TILE, seq)
    bsz = shift.shape[0]
    vec = lambda i: (i * tm // seq, 0, 0)
    resident = pl.Buffered(1)
    return pl.pallas_call(
        _ffn_kernel,
        out_shape=jax.ShapeDtypeStruct((n, d), F32),
        grid=(n // tm,),
        in_specs=[
            pl.BlockSpec((tm, d), lambda i: (i, 0)),
            pl.BlockSpec((1, d), lambda i: (0, 0)),
            pl.BlockSpec((1, 1, d), vec),
            pl.BlockSpec((1, 1, d), vec),
            pl.BlockSpec((1, 1, d), vec),
            pl.BlockSpec((d, ff), lambda i: (0, 0), pipeline_mode=resident),
            pl.BlockSpec((d, ff), lambda i: (0, 1), pipeline_mode=resident),
            pl.BlockSpec((ff, d), lambda i: (0, 0), pipeline_mode=resident),
        ],
        out_specs=pl.BlockSpec((tm, d), lambda i: (i, 0)),
        compiler_params=_cparams(("parallel",)),
        name="dense_ffn",
    )(x, gain.reshape(1, d), shift.reshape(bsz, 1, d), scale.reshape(bsz, 1, d),
      gate.reshape(bsz, 1, d), w_in, w_in, w_out)


def _ret_kernel(q_ref, k_ref, v_ref, g_ref, gn_ref, dm_ref, qd_ref, kd_ref, cd_ref, o_ref, *, nblk):
    dmat = dm_ref[0]
    qdec = qd_ref[0]
    kdec = kd_ref[0]
    cdec = cd_ref[0, 0:1, :]
    gn = gn_ref[...]

    def load(t):
        sl = slice(t * RET_BLOCK, (t + 1) * RET_BLOCK)
        return q_ref[0, sl, :], k_ref[0, sl, :], v_ref[0, sl, :]

    def scores(q, k):
        return lax.dot_general(q, k, (((1,), (1,)), ((), ())), preferred_element_type=F32)

    state = jnp.zeros((RET_DK, RET_DV), F32)
    q, k, v = load(0)
    s = scores(q, k)
    for t in range(nblk):
        qd = (q.astype(F32) * qdec).astype(BF16)
        cross = jnp.dot(qd, state.astype(BF16), preferred_element_type=F32)
        kdt = jnp.transpose(k.astype(F32) * kdec).astype(BF16)
        update = jnp.dot(kdt, v, preferred_element_type=F32)
        if t + 1 < nblk:
            q, k, v_next = load(t + 1)
            s_next = scores(q, k)
        intra = jnp.dot((s * dmat).astype(BF16), v, preferred_element_type=F32)
        state = state * cdec + update
        o = intra + cross
        mu = jnp.mean(o, axis=-1, keepdims=True)
        oc = o - mu
        var = jnp.mean(oc * oc, axis=-1, keepdims=True)
        on = (oc * lax.rsqrt(var + EPS)) * gn
        sl = slice(t * RET_BLOCK, (t + 1) * RET_BLOCK)
        g = g_ref[0, sl, :].astype(F32)
        o_ref[0, sl, :] = ((g * _sigmoid(g)) * on).astype(o_ref.dtype)
        if t + 1 < nblk:
            v, s = v_next, s_next


def _retention_tables():
    t = RET_BLOCK
    log_gamma = jnp.log(1.0 - jnp.exp2(-5.0 - jnp.arange(RET_HEADS, dtype=F32)))
    pos = jnp.arange(t, dtype=F32)
    chunk = jnp.arange(t) // CHUNK
    dist = jnp.abs(pos[:, None] - pos[None, :])
    visible = (chunk[None, :] <= chunk[:, None]).astype(F32)
    dmat = jnp.exp(dist[None] * log_gamma[:, None, None]) * visible[None]
    qdec = jnp.exp((pos[None, :] + 1.0) * log_gamma[:, None])
    kdec = jnp.exp((t - 1.0 - pos)[None, :] * log_gamma[:, None])
    cdec = jnp.exp(t * log_gamma)
    qdec = jnp.broadcast_to(qdec[:, :, None], (RET_HEADS, t, RET_DK))
    kdec = jnp.broadcast_to(kdec[:, :, None], (RET_HEADS, t, RET_DK))
    cdec = jnp.broadcast_to(cdec[:, None, None], (RET_HEADS, 8, RET_DV))
    return dmat, qdec, kdec, cdec


def _retention_core(proj, gn_w, tables, bsz, seq):
    dmat, qdec, kdec, cdec = tables
    h = RET_HEADS
    t = RET_BLOCK
    kq = h * RET_DK // RET_DK
    kv = 2 * h * RET_DK // RET_DV
    return pl.pallas_call(
        functools.partial(_ret_kernel, nblk=seq // t),
        out_shape=jax.ShapeDtypeStruct((bsz, seq, h * RET_DV), BF16),
        grid=(bsz, h),
        in_specs=[
            pl.BlockSpec((1, seq, RET_DK), lambda b, i: (b, 0, i)),
            pl.BlockSpec((1, seq, RET_DK), lambda b, i: (b, 0, kq + i)),
            pl.BlockSpec((1, seq, RET_DV), lambda b, i: (b, 0, kv + i)),
            pl.BlockSpec((1, seq, RET_DV), lambda b, i: (b, 0, kv + h + i)),
            pl.BlockSpec((1, RET_DV), lambda b, i: (0, i)),
            pl.BlockSpec((1, t, t), lambda b, i: (i, 0, 0)),
            pl.BlockSpec((1, t, RET_DK), lambda b, i: (i, 0, 0)),
            pl.BlockSpec((1, t, RET_DK), lambda b, i: (i, 0, 0)),
            pl.BlockSpec((1, 8, RET_DV), lambda b, i: (i, 0, 0)),
        ],
        out_specs=pl.BlockSpec((1, seq, RET_DV), lambda b, i: (b, 0, i)),
        compiler_params=_cparams(("parallel", "parallel")),
        name="retention_core",
    )(proj, proj, proj, proj, gn_w.reshape(1, h * RET_DV), dmat, qdec, kdec, cdec)


def _sb_kernel(q_ref, k_ref, v_ref, o_ref, r_ref, acc_ref):
    t = SB_TILE
    w = 2 * SB_DH
    pairs = SB_HEADS // 2
    qi = pl.program_id(1)
    lane = lax.broadcasted_iota(jnp.int32, (t, w), 1)
    row = lax.broadcasted_iota(jnp.int32, (t, 2 * t), 0)
    col = lax.broadcasted_iota(jnp.int32, (t, 2 * t), 1)
    diag_causal = jnp.where(col >= t, col - t, col) < row
    uj = lax.broadcasted_iota(jnp.int32, (2 * t, 2 * t), 0)
    us = lax.broadcasted_iota(jnp.int32, (2 * t, 2 * t), 1)
    uj = jnp.where(uj >= t, uj - t, uj)
    umat = jnp.where((uj >= us) | (us >= t), -1.0, 0.0).astype(BF16)
    scale = jnp.asarray(SB_DH ** -0.5, BF16)
    qs = [q_ref[0, :, p * w:(p + 1) * w] * scale for p in range(pairs)]

    def split_heads(x):
        zero = jnp.zeros_like(x)
        return jnp.concatenate([jnp.where(lane < SB_DH, x, zero), jnp.where(lane >= SB_DH, x, zero)], axis=0)

    def key_tiles(kbs, causal, first):
        spans = [pl.ds(pl.multiple_of(kb * t, t), t) for kb in kbs]
        tiles = [(j, p) for j in range(len(kbs)) for p in range(pairs)]
        zs, his, los, incls, tots = {}, {}, {}, {}, {}
        for j, p in tiles:
            kcat = split_heads(k_ref[0, spans[j], p * w:(p + 1) * w])
            zs[j, p] = lax.dot_general(qs[p], kcat, (((1,), (1,)), ((), ())), preferred_element_type=F32)
        for j, p in tiles:
            z = zs[j, p]
            sp = jnp.maximum(z, 0.0) + jnp.log(1.0 + jnp.exp(-jnp.abs(z)))
            if causal is not None and j == 0:
                sp = jnp.where(causal, sp, 0.0)
            hi = sp.astype(BF16)
            his[j, p] = hi
            los[j, p] = (sp - hi.astype(F32)).astype(BF16)
        for j, p in tiles:
            cs = [jnp.dot(jnp.concatenate([his[j, p][:, h * t:(h + 1) * t], los[j, p][:, h * t:(h + 1) * t]],
                                          axis=1), umat, preferred_element_type=F32) for h in range(2)]
            incls[j, p] = jnp.concatenate([cs[0][:, :t], cs[1][:, :t]], axis=1)
            tots[j, p] = jnp.concatenate([cs[0][:, t:], cs[1][:, t:]], axis=1)
        avs = {}
        rs = [None if first else r_ref[p] for p in range(pairs)]
        for j, p in tiles:
            e = zs[j, p] + incls[j, p]
            a = jnp.exp(e if rs[p] is None else e + rs[p])
            if causal is not None and j == 0:
                a = jnp.where(causal, a, 0.0)
            avs[j, p] = a.astype(BF16)
            rs[p] = tots[j, p] if rs[p] is None else rs[p] + tots[j, p]
        for p in range(pairs):
            r_ref[p] = rs[p]
        accs = [None if first else acc_ref[p] for p in range(pairs)]
        for j, p in tiles:
            vcat = split_heads(v_ref[0, spans[j], p * w:(p + 1) * w])
            av = jnp.dot(avs[j, p], vcat, preferred_element_type=F32)
            accs[p] = av if accs[p] is None else accs[p] + av
        for p in range(pairs):
            acc_ref[p] = accs[p]
        return jnp.max(functools.reduce(jnp.maximum, rs)) > SB_LOG_FLOOR

    left = key_tiles([qi], diag_causal, first=True)

    def walk(carry, width):
        def cond(carry):
            kb, left = carry
            return jnp.logical_and(kb >= width - 1, left)

        def body(carry):
            kb, _ = carry
            return kb - width, key_tiles([kb - j for j in range(width)], None, first=False)

        return lax.while_loop(cond, body, carry)

    walk(walk((qi - 1, left), 2), 1)
    for p in range(pairs):
        o_ref[0, :, p * w:(p + 1) * w] = acc_ref[p].astype(o_ref.dtype)


def _stick_breaking_core(q, kv, bsz, seq):
    t = SB_TILE
    return pl.pallas_call(
        _sb_kernel,
        out_shape=jax.ShapeDtypeStruct((bsz, seq, D_MODEL), BF16),
        grid=(bsz, seq // t),
        in_specs=[
            pl.BlockSpec((1, t, D_MODEL), lambda b, i: (b, i, 0)),
            pl.BlockSpec((1, seq, D_MODEL), lambda b, i: (b, 0, 0)),
            pl.BlockSpec((1, seq, D_MODEL), lambda b, i: (b, 0, 1)),
        ],
        out_specs=pl.BlockSpec((1, t, D_MODEL), lambda b, i: (b, i, 0)),
        scratch_shapes=[pltpu.VMEM((SB_HEADS // 2, t, 2 * t), F32),
                        pltpu.VMEM((SB_HEADS // 2, t, 2 * SB_DH), F32)],
        compiler_params=_cparams(("parallel", "parallel")),
        name="stick_breaking_core",
    )(q, kv, kv)


def _route_kernel(x_ref, g_ref, sh_ref, sc_ref, r_ref, h_ref, meta_ref, meta_t_ref, pc_ref):
    tb = MOE_BLOCK
    h32 = _rms_mod(x_ref[...], g_ref[...], sh_ref[0], sc_ref[0])
    h1 = h32.astype(BF16)
    h_ref[...] = h1
    e1 = h32 - h1.astype(F32)
    h2 = e1.astype(BF16)
    h3 = (e1 - h2.astype(F32)).astype(BF16)
    r3 = r_ref[...]
    s3 = (jnp.dot(h1, r3, preferred_element_type=F32) + jnp.dot(h2, r3, preferred_element_type=F32)
          + jnp.dot(h3, r3, preferred_element_type=F32))
    logits = s3 + pltpu.roll(s3, LANES - N_EXPERTS, 1) + pltpu.roll(s3, LANES - 2 * N_EXPERTS, 1)
    lane = lax.broadcasted_iota(jnp.int32, (tb, LANES), 1)
    lane_f = lane.astype(F32)
    neg = jnp.float32(-jnp.inf)
    lg = jnp.where(lane < N_EXPERTS, logits, neg)
    m1 = jnp.max(lg, axis=-1, keepdims=True)
    i1 = jnp.min(jnp.where(lg == m1, lane_f, float(LANES)), axis=-1, keepdims=True)
    oh1 = lane_f == i1
    lg2 = jnp.where(oh1, neg, lg)
    m2 = jnp.max(lg2, axis=-1, keepdims=True)
    i2 = jnp.min(jnp.where(lg2 == m2, lane_f, float(LANES)), axis=-1, keepdims=True)
    oh2 = lane_f == i2
    e = jnp.exp(m2 - m1)
    w1 = 1.0 / (1.0 + e)
    w2 = e / (1.0 + e)

    tr = lax.broadcasted_iota(jnp.int32, (tb, tb), 0)
    tc = lax.broadcasted_iota(jnp.int32, (tb, tb), 1)
    lower = jnp.where(tc < tr, 1.0, 0.0).astype(BF16)
    oh1f = jnp.where(oh1, 1.0, 0.0)
    oh2f = jnp.where(oh2, 1.0, 0.0)
    cum1 = jnp.dot(lower, oh1f.astype(BF16), preferred_element_type=F32)
    cum2 = jnp.dot(lower, oh2f.astype(BF16), preferred_element_type=F32)
    cnt1 = jnp.sum(oh1f, axis=0, keepdims=True)
    cnt2 = jnp.sum(oh2f, axis=0, keepdims=True)
    pc = jnp.floor((cnt1 + cnt2 + (MOE_GROUP - 1.0)) / MOE_GROUP) * MOE_GROUP
    pj = lax.broadcasted_iota(jnp.int32, (LANES, LANES), 0)
    ps = lax.broadcasted_iota(jnp.int32, (LANES, LANES), 1)
    before = jnp.where(pj < ps, 1.0, 0.0)
    off = jnp.dot(jnp.broadcast_to(pc, (8, LANES)), before, preferred_element_type=F32,
                  precision=lax.Precision.HIGHEST)[0:1]
    pos1 = jnp.sum(jnp.where(oh1, off + cum1, 0.0), axis=-1, keepdims=True)
    pos2 = jnp.sum(jnp.where(oh2, off + cnt1 + cum2, 0.0), axis=-1, keepdims=True)
    meta = jnp.where(lane == 0, pos1,
                     jnp.where(lane == 1, pos2,
                               jnp.where(lane == 2, w1, jnp.where(lane == 3, w2, 0.0))))
    meta_ref[...] = meta
    meta_t_ref[0] = jnp.transpose(meta)[0:8]
    pc_ref[0] = jnp.broadcast_to(pc, (8, LANES))


def _moe_route(x, gain, shift, scale, router, seq):
    n, d = x.shape
    tb = MOE_BLOCK
    nb = n // tb
    bsz = shift.shape[0]
    r1 = router.astype(BF16)
    r2 = (router - r1.astype(F32)).astype(BF16)
    r3 = (router - r1.astype(F32) - r2.astype(F32)).astype(BF16)
    r_pad = jnp.zeros((d, LANES), BF16).at[:, :3 * N_EXPERTS].set(jnp.concatenate([r1, r2, r3], axis=1))
    vec = lambda i: (i * tb // seq, 0, 0)
    return pl.pallas_call(
        _route_kernel,
        out_shape=(
            jax.ShapeDtypeStruct((n, d), BF16),
            jax.ShapeDtypeStruct((n, LANES), F32),
            jax.ShapeDtypeStruct((nb, 8, tb), F32),
            jax.ShapeDtypeStruct((nb, 8, LANES), F32),
        ),
        grid=(nb,),
        in_specs=[
            pl.BlockSpec((tb, d), lambda i: (i, 0)),
            pl.BlockSpec((1, d), lambda i: (0, 0)),
            pl.BlockSpec((1, 1, d), vec),
            pl.BlockSpec((1, 1, d), vec),
            pl.BlockSpec((d, LANES), lambda i: (0, 0)),
        ],
        out_specs=(
            pl.BlockSpec((tb, d), lambda i: (i, 0)),
            pl.BlockSpec((tb, LANES), lambda i: (i, 0)),
            pl.BlockSpec((1, 8, tb), lambda i: (i, 0, 0)),
            pl.BlockSpec((1, 8, LANES), lambda i: (i, 0, 0)),
        ),
        compiler_params=_cparams(("parallel",)),
        name="moe_route",
    )(x, gain.reshape(1, d), shift.reshape(bsz, 1, d), scale.reshape(bsz, 1, d), r_pad)


def _segment_copies(loc_ref, glb_ref, len_ref, blk, make_copy, wait):
    for e in range(N_EXPERTS):
        idx = blk * N_EXPERTS + e
        loc = loc_ref[idx]
        glb = glb_ref[idx]
        length = len_ref[idx]
        done = jnp.int32(0)
        for sz in SEG_SIZES:
            take = (length & sz) != 0

            @pl.when(take)
            def _(loc=loc, glb=glb, done=done, sz=sz):
                cp = make_copy(pl.multiple_of(loc + done, MOE_GROUP),
                               pl.multiple_of(glb + done, MOE_GROUP), sz)
                if wait:
                    cp.wait()
                else:
                    cp.start()

            done = done + jnp.where(take, sz, 0)


def _dispatch_kernel(loc_ref, glb_ref, len_ref, h_ref, meta_t_ref, hs_in_ref, hs_ref, buf_ref, sem):
    del hs_in_ref
    b = pl.program_id(0)
    tb = MOE_BLOCK
    pos1 = meta_t_ref[0, 0:1, :]
    pos2 = meta_t_ref[0, 1:2, :]
    r = lax.broadcasted_iota(jnp.int32, (MOE_CAP, tb), 0).astype(F32)
    perm = jnp.where((r == pos1) | (r == pos2), 1.0, 0.0).astype(BF16)
    sorted_rows = jnp.dot(perm, h_ref[...], preferred_element_type=F32).astype(BF16)

    def make_copy(loc, glb, sz):
        return pltpu.make_async_copy(buf_ref.at[pl.ds(loc, sz)], hs_ref.at[pl.ds(glb, sz)], sem)

    @pl.when(b > 0)
    def _():
        _segment_copies(loc_ref, glb_ref, len_ref, b - 1, make_copy, wait=True)

    buf_ref[...] = sorted_rows
    _segment_copies(loc_ref, glb_ref, len_ref, b, make_copy, wait=False)

    @pl.when(b == pl.num_programs(0) - 1)
    def _():
        _segment_copies(loc_ref, glb_ref, len_ref, b, make_copy, wait=True)


def _moe_dispatch(h, meta_t, loc, glb, seg_len, rows_total):
    n, d = h.shape
    tb = MOE_BLOCK
    nb = n // tb
    hs0 = jnp.zeros((rows_total, d), BF16)
    return pl.pallas_call(
        _dispatch_kernel,
        out_shape=jax.ShapeDtypeStruct((rows_total, d), BF16),
        grid_spec=pltpu.PrefetchScalarGridSpec(
            num_scalar_prefetch=3,
            grid=(nb,),
            in_specs=[
                pl.BlockSpec((tb, d), lambda i, *_: (i, 0)),
                pl.BlockSpec((1, 8, tb), lambda i, *_: (i, 0, 0)),
                pl.BlockSpec(memory_space=pl.ANY),
            ],
            out_specs=pl.BlockSpec(memory_space=pl.ANY),
            scratch_shapes=[pltpu.VMEM((MOE_CAP, d), BF16), pltpu.SemaphoreType.DMA],
        ),
        input_output_aliases={5: 0},
        compiler_params=_cparams(("arbitrary",)),
        name="moe_dispatch",
    )(loc, glb, seg_len, h, meta_t, hs0)


def _expert_kernel(te_ref, tx_ref, tv_ref, x_ref, wa_ref, wb_ref, wo_ref, o_ref, acc_ref):
    i = pl.program_id(0)
    f = pl.program_id(1)

    last = pl.num_programs(1) - 1

    @pl.when(tv_ref[i] == 1)
    def _():
        y = _swiglu(x_ref[...], lambda sl: wa_ref[0, :, sl], lambda sl: wb_ref[0, :, sl], wo_ref[0],
                    wo_ref.shape[1])

        @pl.when(f == 0)
        def _():
            acc_ref[...] = y

        @pl.when(jnp.logical_and(f > 0, f < last))
        def _():
            acc_ref[...] += y

        @pl.when(f == last)
        def _():
            o_ref[...] = (acc_ref[...] + y).astype(o_ref.dtype)

    @pl.when(jnp.logical_and(tv_ref[i] == 0, f == pl.num_programs(1) - 1))
    def _():
        o_ref[...] = jnp.zeros_like(o_ref)


def _moe_experts(hs, w_in, w_out, tile_expert, tile_row, tile_valid):
    rows, d = hs.shape
    tm = ROW_TILE
    tf = FF_EXPERT_TILE
    nf = FF_EXPERT // tf
    nt = rows // tm

    def f_eff(i, f, tv):
        return jnp.where(tv[i] == 1, f, nf - 1)

    return pl.pallas_call(
        _expert_kernel,
        out_shape=jax.ShapeDtypeStruct((rows, d), BF16),
        grid_spec=pltpu.PrefetchScalarGridSpec(
            num_scalar_prefetch=3,
            grid=(nt, nf),
            in_specs=[
                pl.BlockSpec((tm, d), lambda i, f, te, tx, tv: (tx[i], 0)),
                pl.BlockSpec((1, d, tf), lambda i, f, te, tx, tv: (te[i], 0, f_eff(i, f, tv))),
                pl.BlockSpec((1, d, tf), lambda i, f, te, tx, tv: (te[i], 0, nf + f_eff(i, f, tv))),
                pl.BlockSpec((1, tf, d), lambda i, f, te, tx, tv: (te[i], f_eff(i, f, tv), 0)),
            ],
            out_specs=pl.BlockSpec((tm, d), lambda i, f, te, tx, tv: (i, 0)),
            scratch_shapes=[pltpu.VMEM((tm, d), F32)],
        ),
        compiler_params=_cparams(("arbitrary", "arbitrary")),
        name="moe_experts",
    )(tile_expert, tile_row, tile_valid, hs, w_in, w_in, w_out)


def _combine_kernel(loc_ref, glb_ref, len_ref, x_ref, gate_ref, meta_ref, fg_ref, ys_ref, o_ref, buf_ref, sems,
                    *, final_norm):
    b = pl.program_id(0)
    tb = MOE_BLOCK

    def fetch(blk, slot, wait):
        def make_copy(loc, glb, sz):
            return pltpu.make_async_copy(ys_ref.at[pl.ds(glb, sz)], buf_ref.at[slot, pl.ds(loc, sz)],
                                         sems.at[slot])

        _segment_copies(loc_ref, glb_ref, len_ref, blk, make_copy, wait)

    def start(blk, slot):
        buf_ref[slot] = jnp.zeros(buf_ref.shape[1:], buf_ref.dtype)
        fetch(blk, slot, wait=False)

    @pl.when(b == 0)
    def _():
        start(0, 0)

    @pl.when(b + 1 < pl.num_programs(0))
    def _():
        start(b + 1, (b + 1) % 2)

    slot = b % 2
    meta = meta_ref[...]
    pos1 = meta[:, 0:1]
    pos2 = meta[:, 1:2]
    w1 = meta[:, 2:3]
    w2 = meta[:, 3:4]
    r = lax.broadcasted_iota(jnp.int32, (tb, MOE_CAP), 1).astype(F32)
    sel1 = jnp.where(r == pos1, 1.0, 0.0).astype(BF16)
    sel2 = jnp.where(r == pos2, 1.0, 0.0).astype(BF16)
    fetch(b, slot, wait=True)
    ys = buf_ref[slot]
    y = (w1 * jnp.dot(sel1, ys, preferred_element_type=F32)
         + w2 * jnp.dot(sel2, ys, preferred_element_type=F32))
    x = x_ref[...] + gate_ref[0] * y
    if final_norm:
        x = (x * lax.rsqrt(jnp.mean(x * x, axis=-1, keepdims=True) + EPS)) * fg_ref[...]
    o_ref[...] = x


def _moe_combine(x, gate, meta, ys, loc, glb, seg_len, seq, final_gain=None):
    n, d = x.shape
    tb = MOE_BLOCK
    bsz = gate.shape[0]
    fg = jnp.ones((1, d), F32) if final_gain is None else final_gain.reshape(1, d)
    return pl.pallas_call(
        functools.partial(_combine_kernel, final_norm=final_gain is not None),
        out_shape=jax.ShapeDtypeStruct((n, d), F32),
        grid_spec=pltpu.PrefetchScalarGridSpec(
            num_scalar_prefetch=3,
            grid=(n // tb,),
            in_specs=[
                pl.BlockSpec((tb, d), lambda i, *_: (i, 0)),
                pl.BlockSpec((1, 1, d), lambda i, *_: (i * tb // seq, 0, 0)),
                pl.BlockSpec((tb, LANES), lambda i, *_: (i, 0)),
                pl.BlockSpec((1, d), lambda i, *_: (0, 0)),
                pl.BlockSpec(memory_space=pl.ANY),
            ],
            out_specs=pl.BlockSpec((tb, d), lambda i, *_: (i, 0)),
            scratch_shapes=[pltpu.VMEM((2, MOE_CAP, d), BF16), pltpu.SemaphoreType.DMA((2,))],
        ),
        compiler_params=_cparams(("arbitrary",)),
        name="moe_combine",
    )(loc, glb, seg_len, x, gate.reshape(bsz, 1, d), meta, fg, ys)


def _moe_layout(pc):
    nb = pc.shape[0]
    tm = ROW_TILE
    n_tiles = (nb * (2 * MOE_BLOCK + N_EXPERTS * (MOE_GROUP - 1)) + tm - 1) // tm + N_EXPERTS
    loc = jnp.cumsum(pc, axis=1) - pc
    tiles_e = (jnp.sum(pc, axis=0) + tm - 1) // tm
    tile_end = jnp.cumsum(tiles_e)
    base_e = (tile_end - tiles_e) * tm
    glb = base_e[None, :] + jnp.cumsum(pc, axis=0) - pc
    n_valid = tile_end[-1]
    ids = jnp.arange(n_tiles, dtype=jnp.int32)
    tile_row = jnp.maximum(jnp.minimum(ids, n_valid - 1), 0)
    tile_expert = jnp.sum((tile_row[:, None] >= tile_end[None, :]).astype(jnp.int32), axis=1)
    tile_valid = (ids < n_valid).astype(jnp.int32)
    flat = lambda a: a.reshape(-1).astype(jnp.int32)
    return flat(loc), flat(glb), flat(pc), tile_expert.astype(jnp.int32), tile_row.astype(jnp.int32), tile_valid, n_tiles * tm


def _moe_block(x, gain, shift, scale, gate, router, w_in, w_out, layer, seq, final_gain=None):
    h, meta, meta_t, pc = _moe_route(x, gain, shift, scale, router, seq)
    pc_i = pc[:, 0, :N_EXPERTS].astype(jnp.int32)
    loc, glb, seg_len, tile_expert, tile_row, tile_valid, rows_total = _moe_layout(pc_i)
    hs = _moe_dispatch(h, meta_t, loc, glb, seg_len, rows_total)
    ys = _moe_experts(hs, w_in, w_out, tile_expert + layer * N_EXPERTS, tile_row, tile_valid)
    return _moe_combine(x, gate, meta, ys, loc, glb, seg_len, seq, final_gain)


def _final_norm_kernel(x_ref, g_ref, o_ref):
    x = x_ref[...]
    o_ref[...] = (x * lax.rsqrt(jnp.mean(x * x, axis=-1, keepdims=True) + EPS)) * g_ref[...]


def _final_norm(x, gain):
    n, d = x.shape
    tm = min(ROW_TILE, n)
    return pl.pallas_call(
        _final_norm_kernel,
        out_shape=jax.ShapeDtypeStruct((n, d), F32),
        grid=(n // tm,),
        in_specs=[pl.BlockSpec((tm, d), lambda i: (i, 0)), pl.BlockSpec((1, d), lambda i: (0, 0))],
        out_specs=pl.BlockSpec((tm, d), lambda i: (i, 0)),
        compiler_params=_cparams(("parallel",)),
        name="final_norm",
    )(x, gain.reshape(1, d))


def kernel(x, c, positions, ada_w, ada_b, norm_mix, norm_ff, ret_w_in, ret_gn, ret_w_out, kv_ada_w, kv_ada_b, kv_norm, kv_w, sb_w_q, sb_w_out, ff_w_in, ff_w_out, moe_router, moe_w_in, moe_w_out, final_norm):
    bsz, seq, d = x.shape
    n = bsz * seq
    mod = _ada_mod(c, ada_w, ada_b)
    kv_mod = _ada_mod(c, kv_ada_w[None], kv_ada_b[None])[0]
    rope = _rope_tables(positions)
    ret_tables = _retention_tables()
    moe_in = moe_w_in.astype(BF16).reshape(-1, d, 2 * FF_EXPERT)
    moe_out = moe_w_out.astype(BF16).reshape(-1, FF_EXPERT, d)
    xf = x.reshape(n, d)
    kv = None
    for i in range(DEPTH):
        sh_m, sc_m, g_m, sh_f, sc_f, g_f = [mod[i, :, p * d:(p + 1) * d] for p in range(6)]
        if i < N_A:
            proj = _norm_mod_matmul(xf, norm_mix[i], sh_m, sc_m, ret_w_in[i].astype(BF16), seq, rope=rope)
            o = _retention_core(proj.reshape(bsz, seq, -1), ret_gn[i], ret_tables, bsz, seq)
            xf = _matmul_residual(o.reshape(n, -1), ret_w_out[i].astype(BF16), xf, g_m, seq)
        else:
            j = i - N_A
            q = _norm_mod_matmul(xf, norm_mix[i], sh_m, sc_m, sb_w_q[j].astype(BF16), seq)
            o = _stick_breaking_core(q.reshape(bsz, seq, d), kv, bsz, seq)
            xf = _matmul_residual(o.reshape(n, d), sb_w_out[j].astype(BF16), xf, g_m, seq)
        if i % 2 == 0:
            xf = _dense_ffn(xf, norm_ff[i], sh_f, sc_f, g_f, ff_w_in[i // 2].astype(BF16),
                            ff_w_out[i // 2].astype(BF16), seq)
        else:
            xf = _moe_block(xf, norm_ff[i], sh_f, sc_f, g_f, moe_router[i // 2],
                            moe_in, moe_out, i // 2, seq,
                            final_gain=final_norm if i == DEPTH - 1 else None)
        if i == N_A - 1:
            kv = _norm_mod_matmul(xf, kv_norm, kv_mod[:, :d], kv_mod[:, d:], kv_w.astype(BF16), seq)
            kv = kv.reshape(bsz, seq, 2 * d)
    if (DEPTH - 1) % 2 == 0:
        xf = _final_norm(xf, final_norm)
    return xf.reshape(bsz, seq, d)
```

```python
import functools

import jax
import jax.numpy as jnp
from jax import lax
from jax.experimental import pallas as pl
from jax.experimental.pallas import tpu as pltpu

F32 = jnp.float32
BF16 = jnp.bfloat16

D_MODEL = 1024
DEPTH = 4
N_A = DEPTH // 2
CHUNK = 64
RET_HEADS = 8
RET_DK = D_MODEL // RET_HEADS
RET_DV = 2 * D_MODEL // RET_HEADS
ROPE_BASE = 10000.0
SB_HEADS = 16
SB_DH = D_MODEL // SB_HEADS
FF_DENSE = 2816
N_EXPERTS = 8
FF_EXPERT = 3584
EPS = 1e-6

LANES = 128
BF16_ROWS = 16
MXU_COLS = 256
VMEM_LIMIT = 56 * 1024 * 1024

ROW_TILE = 1024
NMM_ROWS = 512
NMM_WIDE = 2048
NMM_CHUNK = 512
DENSE_ROWS = 512
FF_EXPERT_TILE = 1792
RET_BLOCK = 256
SB_TILE = 128
MOE_BLOCK = 512
MOE_GROUP = BF16_ROWS
MOE_CAP = 2 * MOE_BLOCK + N_EXPERTS * MOE_GROUP
SEG_SIZES = (512, 256, 128, 64, 32, 16)
SB_LOG_FLOOR = -110.0


def _cparams(sem):
    return pltpu.CompilerParams(dimension_semantics=sem, vmem_limit_bytes=VMEM_LIMIT)


def _sigmoid(v):
    return 1.0 / (1.0 + jnp.exp(-v))


def _rms_mod(x, gain, shift, scale):
    y = x * lax.rsqrt(jnp.mean(x * x, axis=-1, keepdims=True) + EPS)
    return (y * gain) * (1.0 + scale) + shift


def _ada_kernel(c_ref, w_ref, b_ref, o_ref):
    c = c_ref[...]
    cond = (c * _sigmoid(c)).astype(BF16)
    o_ref[0] = jnp.dot(cond, w_ref[0].astype(BF16), preferred_element_type=F32) + b_ref[0]


def _ada_mod(c, w, b):
    nl, d, n = w.shape
    bsz = c.shape[0]
    tn = 1024
    return pl.pallas_call(
        _ada_kernel,
        out_shape=jax.ShapeDtypeStruct((nl, bsz, n), F32),
        grid=(nl, n // tn),
        in_specs=[
            pl.BlockSpec((bsz, d), lambda l, j: (0, 0)),
            pl.BlockSpec((1, d, tn), lambda l, j: (l, 0, j)),
            pl.BlockSpec((1, 1, tn), lambda l, j: (l, 0, j)),
        ],
        out_specs=pl.BlockSpec((1, bsz, tn), lambda l, j: (l, 0, j)),
        compiler_params=_cparams(("parallel", "parallel")),
        name="ada_mod",
    )(c, w, b.reshape(nl, 1, n))


def _rope_table_kernel(pos_ref, freq_ref, cs_ref, sn_ref):
    ang = pos_ref[...] * freq_ref[...]
    lane = lax.broadcasted_iota(jnp.int32, ang.shape, 1)
    cs_ref[...] = jnp.cos(ang)
    sn_ref[...] = jnp.where(lane < RET_DK // 2, -jnp.sin(ang), jnp.sin(ang))


def _rope_tables(positions):
    n = positions.size
    inv_freq = jnp.power(ROPE_BASE, -jnp.arange(0, RET_DK, 2, dtype=F32) / RET_DK)
    freq = jnp.concatenate([inv_freq, inv_freq]).reshape(1, RET_DK)
    pos = jnp.broadcast_to(positions.astype(F32).reshape(n, 1), (n, RET_DK))
    tr = min(n, 2048)
    return pl.pallas_call(
        _rope_table_kernel,
        out_shape=(jax.ShapeDtypeStruct((n, RET_DK), F32),) * 2,
        grid=(n // tr,),
        in_specs=[pl.BlockSpec((tr, RET_DK), lambda i: (i, 0)),
                  pl.BlockSpec((1, RET_DK), lambda i: (0, 0))],
        out_specs=(pl.BlockSpec((tr, RET_DK), lambda i: (i, 0)),) * 2,
        compiler_params=_cparams(("parallel",)),
        name="rope_tables",
    )(pos, freq)


def _nmm_kernel(x_ref, g_ref, sh_ref, sc_ref, w_ref, *rest, rope_cols):
    if rope_cols:
        cs_ref, sn_ref, o_ref = rest
    else:
        (o_ref,) = rest
    cw = NMM_CHUNK
    n_chunks = w_ref.shape[1] // cw
    h = _rms_mod(x_ref[...], g_ref[...], sh_ref[0], sc_ref[0]).astype(BF16)

    def mm(c):
        return jnp.dot(h, w_ref[:, c * cw:(c + 1) * cw], preferred_element_type=F32)

    nxt = mm(0)
    for c in range(n_chunks):
        r = nxt
        if c + 1 < n_chunks:
            nxt = mm(c + 1)
        col0 = c * cw
        if col0 >= rope_cols:
            o_ref[:, col0:col0 + cw] = r.astype(o_ref.dtype)
            continue
        k_scale = RET_DK ** -0.5 if col0 >= rope_cols // 2 else None
        for s in range(cw // RET_DK):
            t = r[:, s * RET_DK:(s + 1) * RET_DK]
            rot = t * cs_ref[...] + pltpu.roll(t, RET_DK // 2, 1) * sn_ref[...]
            if k_scale is not None:
                rot = rot * k_scale
            o_ref[:, col0 + s * RET_DK:col0 + (s + 1) * RET_DK] = rot.astype(o_ref.dtype)


def _norm_mod_matmul(x, gain, shift, scale, w, seq, rope=None):
    n, d = x.shape
    m = w.shape[1]
    tm = min(NMM_ROWS if m > NMM_WIDE else ROW_TILE, seq)
    bsz = shift.shape[0]
    vec = lambda i: (i * tm // seq, 0, 0)
    in_specs = [
        pl.BlockSpec((tm, d), lambda i: (i, 0)),
        pl.BlockSpec((1, d), lambda i: (0, 0)),
        pl.BlockSpec((1, 1, d), vec),
        pl.BlockSpec((1, 1, d), vec),
        pl.BlockSpec((d, m), lambda i: (0, 0), pipeline_mode=pl.Buffered(1)),
    ]
    args = [x, gain.reshape(1, d), shift.reshape(bsz, 1, d), scale.reshape(bsz, 1, d), w]
    rope_cols = 0
    if rope is not None:
        rope_cols = 2 * RET_HEADS * RET_DK
        in_specs += [pl.BlockSpec((tm, RET_DK), lambda i: (i, 0))] * 2
        args += list(rope)
    return pl.pallas_call(
        functools.partial(_nmm_kernel, rope_cols=rope_cols),
        out_shape=jax.ShapeDtypeStruct((n, m), BF16),
        grid=(n // tm,),
        in_specs=in_specs,
        out_specs=pl.BlockSpec((tm, m), lambda i: (i, 0)),
        compiler_params=_cparams(("parallel",)),
        name="norm_mod_matmul_rope" if rope_cols else "norm_mod_matmul",
    )(*args)


def _swiglu(h, load_wa, load_wb, load_wo, width, emit):
    def up(c):
        sl = slice(c * MXU_COLS, (c + 1) * MXU_COLS)
        return (jnp.dot(h, load_wa(sl), preferred_element_type=F32),
                jnp.dot(h, load_wb(sl), preferred_element_type=F32))

    n_chunks = width // MXU_COLS
    acts = []
    nxt = up(0)
    for c in range(n_chunks):
        a, b = nxt
        if c + 1 < n_chunks:
            nxt = up(c + 1)
        acts.append((a * _sigmoid(a) * b).astype(BF16))
    act = jnp.concatenate(acts, axis=1)

    def down(c):
        return jnp.dot(act, load_wo(slice(c * MXU_COLS, (c + 1) * MXU_COLS)), preferred_element_type=F32)

    n_out = D_MODEL // MXU_COLS
    nxt = down(0)
    for c in range(n_out):
        y = nxt
        if c + 1 < n_out:
            nxt = down(c + 1)
        emit(slice(c * MXU_COLS, (c + 1) * MXU_COLS), y)


def _ffn_kernel(a_ref, wp_ref, gp_ref, x_ref, g_ref, sh_ref, sc_ref, gate_ref, wa_ref, wb_ref, wo_ref, o_ref,
                x1_ref):
    x1_ref[...] = x_ref[...] + gp_ref[0] * jnp.dot(a_ref[...], wp_ref[...], preferred_element_type=F32)
    h = _rms_mod(x1_ref[...], g_ref[...], sh_ref[0], sc_ref[0]).astype(BF16)

    def emit(cols, y):
        o_ref[:, cols] = x1_ref[:, cols] + gate_ref[0, :, cols] * y

    _swiglu(h, lambda sl: wa_ref[:, sl], lambda sl: wb_ref[:, sl], lambda sl: wo_ref[:, sl],
            wo_ref.shape[0], emit)


def _dense_ffn(a, w_proj, gate_proj, x, gain, shift, scale, gate, w_in, w_out, seq):
    n, d = x.shape
    k = a.shape[1]
    ff = w_out.shape[0]
    tm = min(DENSE_ROWS, seq)
    bsz = shift.shape[0]
    vec = lambda i: (i * tm // seq, 0, 0)
    resident = pl.Buffered(1)
    return pl.pallas_call(
        _ffn_kernel,
        out_shape=jax.ShapeDtypeStruct((n, d), F32),
        grid=(n // tm,),
        in_specs=[
            pl.BlockSpec((tm, k), lambda i: (i, 0)),
            pl.BlockSpec((k, d), lambda i: (0, 0), pipeline_mode=resident),
            pl.BlockSpec((1, 1, d), vec),
            pl.BlockSpec((tm, d), lambda i: (i, 0)),
            pl.BlockSpec((1, d), lambda i: (0, 0)),
            pl.BlockSpec((1, 1, d), vec),
            pl.BlockSpec((1, 1, d), vec),
            pl.BlockSpec((1, 1, d), vec),
            pl.BlockSpec((d, ff), lambda i: (0, 0), pipeline_mode=resident),
            pl.BlockSpec((d, ff), lambda i: (0, 1), pipeline_mode=resident),
            pl.BlockSpec((ff, d), lambda i: (0, 0), pipeline_mode=resident),
        ],
        out_specs=pl.BlockSpec((tm, d), lambda i: (i, 0)),
        scratch_shapes=[pltpu.VMEM((tm, d), F32)],
        compiler_params=_cparams(("parallel",)),
        name="dense_ffn",
    )(a, w_proj, gate_proj.reshape(bsz, 1, d), x, gain.reshape(1, d), shift.reshape(bsz, 1, d),
      scale.reshape(bsz, 1, d), gate.reshape(bsz, 1, d), w_in, w_in, w_out)


def _ret_kernel(q_ref, k_ref, v_ref, g_ref, gn_ref, dm_ref, qd_ref, kd_ref, cd_ref, o_ref, *, nblk):
    dmat = dm_ref[0]
    qdec = qd_ref[0]
    kdec = kd_ref[0]
    cdec = cd_ref[0, 0:1, :]
    gn = gn_ref[...]

    def load(t):
        sl = slice(t * RET_BLOCK, (t + 1) * RET_BLOCK)
        return q_ref[0, sl, :], k_ref[0, sl, :], v_ref[0, sl, :]

    def scores(q, k):
        return lax.dot_general(q, k, (((1,), (1,)), ((), ())), preferred_element_type=F32)

    state = jnp.zeros((RET_DK, RET_DV), F32)
    q, k, v = load(0)
    s = scores(q, k)
    for t in range(nblk):
        qd = (q.astype(F32) * qdec).astype(BF16)
        cross = jnp.dot(qd, state.astype(BF16), preferred_element_type=F32)
        kdt = jnp.transpose(k.astype(F32) * kdec).astype(BF16)
        update = jnp.dot(kdt, v, preferred_element_type=F32)
        if t + 1 < nblk:
            q, k, v_next = load(t + 1)
            s_next = scores(q, k)
        intra = jnp.dot((s * dmat).astype(BF16), v, preferred_element_type=F32)
        state = state * cdec + update
        o = intra + cross
        mu = jnp.mean(o, axis=-1, keepdims=True)
        oc = o - mu
        var = jnp.mean(oc * oc, axis=-1, keepdims=True)
        on = (oc * lax.rsqrt(var + EPS)) * gn
        sl = slice(t * RET_BLOCK, (t + 1) * RET_BLOCK)
        g = g_ref[0, sl, :].astype(F32)
        o_ref[0, sl, :] = ((g * _sigmoid(g)) * on).astype(o_ref.dtype)
        if t + 1 < nblk:
            v, s = v_next, s_next


def _retention_tables():
    t = RET_BLOCK
    log_gamma = jnp.log(1.0 - jnp.exp2(-5.0 - jnp.arange(RET_HEADS, dtype=F32)))
    pos = jnp.arange(t, dtype=F32)
    chunk = jnp.arange(t) // CHUNK
    dist = jnp.abs(pos[:, None] - pos[None, :])
    visible = (chunk[None, :] <= chunk[:, None]).astype(F32)
    dmat = jnp.exp(dist[None] * log_gamma[:, None, None]) * visible[None]
    qdec = jnp.exp((pos[None, :] + 1.0) * log_gamma[:, None])
    kdec = jnp.exp((t - 1.0 - pos)[None, :] * log_gamma[:, None])
    cdec = jnp.exp(t * log_gamma)
    qdec = jnp.broadcast_to(qdec[:, :, None], (RET_HEADS, t, RET_DK))
    kdec = jnp.broadcast_to(kdec[:, :, None], (RET_HEADS, t, RET_DK))
    cdec = jnp.broadcast_to(cdec[:, None, None], (RET_HEADS, 8, RET_DV))
    return dmat, qdec, kdec, cdec


def _retention_core(proj, gn_w, tables, bsz, seq):
    dmat, qdec, kdec, cdec = tables
    h = RET_HEADS
    t = RET_BLOCK
    kq = h * RET_DK // RET_DK
    kv = 2 * h * RET_DK // RET_DV
    return pl.pallas_call(
        functools.partial(_ret_kernel, nblk=seq // t),
        out_shape=jax.ShapeDtypeStruct((bsz, seq, h * RET_DV), BF16),
        grid=(bsz, h),
        in_specs=[
            pl.BlockSpec((1, seq, RET_DK), lambda b, i: (b, 0, i)),
            pl.BlockSpec((1, seq, RET_DK), lambda b, i: (b, 0, kq + i)),
            pl.BlockSpec((1, seq, RET_DV), lambda b, i: (b, 0, kv + i)),
            pl.BlockSpec((1, seq, RET_DV), lambda b, i: (b, 0, kv + h + i)),
            pl.BlockSpec((1, RET_DV), lambda b, i: (0, i)),
            pl.BlockSpec((1, t, t), lambda b, i: (i, 0, 0)),
            pl.BlockSpec((1, t, RET_DK), lambda b, i: (i, 0, 0)),
            pl.BlockSpec((1, t, RET_DK), lambda b, i: (i, 0, 0)),
            pl.BlockSpec((1, 8, RET_DV), lambda b, i: (i, 0, 0)),
        ],
        out_specs=pl.BlockSpec((1, seq, RET_DV), lambda b, i: (b, 0, i)),
        compiler_params=_cparams(("parallel", "parallel")),
        name="retention_core",
    )(proj, proj, proj, proj, gn_w.reshape(1, h * RET_DV), dmat, qdec, kdec, cdec)


def _sb_kernel(q_ref, k_ref, v_ref, o_ref, r_ref, acc_ref):
    t = SB_TILE
    w = 2 * SB_DH
    pairs = SB_HEADS // 2
    qi = pl.program_id(1)
    lane = lax.broadcasted_iota(jnp.int32, (t, w), 1)
    row = lax.broadcasted_iota(jnp.int32, (t, 2 * t), 0)
    col = lax.broadcasted_iota(jnp.int32, (t, 2 * t), 1)
    diag_causal = jnp.where(col >= t, col - t, col) < row
    uj = lax.broadcasted_iota(jnp.int32, (2 * t, 2 * t), 0)
    us = lax.broadcasted_iota(jnp.int32, (2 * t, 2 * t), 1)
    uj = jnp.where(uj >= t, uj - t, uj)
    umat = jnp.where((uj >= us) | (us >= t), -1.0, 0.0).astype(BF16)
    scale = jnp.asarray(SB_DH ** -0.5, BF16)
    qs = [q_ref[0, :, p * w:(p + 1) * w] * scale for p in range(pairs)]

    def split_heads(x):
        zero = jnp.zeros_like(x)
        return jnp.concatenate([jnp.where(lane < SB_DH, x, zero), jnp.where(lane >= SB_DH, x, zero)], axis=0)

    def key_tiles(kbs, causal, first):
        spans = [pl.ds(pl.multiple_of(kb * t, t), t) for kb in kbs]
        tiles = [(j, p) for j in range(len(kbs)) for p in range(pairs)]
        zs, his, los, incls, tots = {}, {}, {}, {}, {}
        for j, p in tiles:
            kcat = split_heads(k_ref[0, spans[j], p * w:(p + 1) * w])
            zs[j, p] = lax.dot_general(qs[p], kcat, (((1,), (1,)), ((), ())), preferred_element_type=F32)
        for j, p in tiles:
            z = zs[j, p]
            sp = jnp.maximum(z, 0.0) + jnp.log(1.0 + jnp.exp(-jnp.abs(z)))
            if causal is not None and j == 0:
                sp = jnp.where(causal, sp, 0.0)
            hi = sp.astype(BF16)
            his[j, p] = hi
            los[j, p] = (sp - hi.astype(F32)).astype(BF16)
        for j, p in tiles:
            cs = [jnp.dot(jnp.concatenate([his[j, p][:, h * t:(h + 1) * t], los[j, p][:, h * t:(h + 1) * t]],
                                          axis=1), umat, preferred_element_type=F32) for h in range(2)]
            incls[j, p] = jnp.concatenate([cs[0][:, :t], cs[1][:, :t]], axis=1)
            tots[j, p] = jnp.concatenate([cs[0][:, t:], cs[1][:, t:]], axis=1)
        avs = {}
        rs = [None if first else r_ref[p] for p in range(pairs)]
        for j, p in tiles:
            e = zs[j, p] + incls[j, p]
            a = jnp.exp(e if rs[p] is None else e + rs[p])
            if causal is not None and j == 0:
                a = jnp.where(causal, a, 0.0)
            avs[j, p] = a.astype(BF16)
            rs[p] = tots[j, p] if rs[p] is None else rs[p] + tots[j, p]
        for p in range(pairs):
            r_ref[p] = rs[p]
        accs = [None if first else acc_ref[p] for p in range(pairs)]
        for j, p in tiles:
            vcat = split_heads(v_ref[0, spans[j], p * w:(p + 1) * w])
            av = jnp.dot(avs[j, p], vcat, preferred_element_type=F32)
            accs[p] = av if accs[p] is None else accs[p] + av
        for p in range(pairs):
            acc_ref[p] = accs[p]
        return jnp.max(functools.reduce(jnp.maximum, rs)) > SB_LOG_FLOOR

    left = key_tiles([qi], diag_causal, first=True)

    def walk(carry, width):
        def cond(carry):
            kb, left = carry
            return jnp.logical_and(kb >= width - 1, left)

        def body(carry):
            kb, _ = carry
            return kb - width, key_tiles([kb - j for j in range(width)], None, first=False)

        return lax.while_loop(cond, body, carry)

    walk(walk((qi - 1, left), 2), 1)
    for p in range(pairs):
        o_ref[0, :, p * w:(p + 1) * w] = acc_ref[p].astype(o_ref.dtype)


def _stick_breaking_core(q, kv, bsz, seq):
    t = SB_TILE
    return pl.pallas_call(
        _sb_kernel,
        out_shape=jax.ShapeDtypeStruct((bsz, seq, D_MODEL), BF16),
        grid=(bsz, seq // t),
        in_specs=[
            pl.BlockSpec((1, t, D_MODEL), lambda b, i: (b, i, 0)),
            pl.BlockSpec((1, seq, D_MODEL), lambda b, i: (b, 0, 0)),
            pl.BlockSpec((1, seq, D_MODEL), lambda b, i: (b, 0, 1)),
        ],
        out_specs=pl.BlockSpec((1, t, D_MODEL), lambda b, i: (b, i, 0)),
        scratch_shapes=[pltpu.VMEM((SB_HEADS // 2, t, 2 * t), F32),
                        pltpu.VMEM((SB_HEADS // 2, t, 2 * SB_DH), F32)],
        compiler_params=_cparams(("parallel", "parallel")),
        name="stick_breaking_core",
    )(q, kv, kv)


def _route_kernel(a_ref, wp_ref, gp_ref, x_ref, g_ref, sh_ref, sc_ref, r_ref,
                  x1_ref, h_ref, meta_ref, meta_t_ref, pc_ref):
    tb = MOE_BLOCK
    x1 = x_ref[...] + gp_ref[0] * jnp.dot(a_ref[...], wp_ref[...], preferred_element_type=F32)
    x1_ref[...] = x1
    h32 = _rms_mod(x1, g_ref[...], sh_ref[0], sc_ref[0])
    h1 = h32.astype(BF16)
    h_ref[...] = h1
    e1 = h32 - h1.astype(F32)
    h2 = e1.astype(BF16)
    h3 = (e1 - h2.astype(F32)).astype(BF16)
    r3 = r_ref[...]
    s3 = (jnp.dot(h1, r3, preferred_element_type=F32) + jnp.dot(h2, r3, preferred_element_type=F32)
          + jnp.dot(h3, r3, preferred_element_type=F32))
    logits = s3 + pltpu.roll(s3, LANES - N_EXPERTS, 1) + pltpu.roll(s3, LANES - 2 * N_EXPERTS, 1)
    lane = lax.broadcasted_iota(jnp.int32, (tb, LANES), 1)
    lane_f = lane.astype(F32)
    neg = jnp.float32(-jnp.inf)
    lg = jnp.where(lane < N_EXPERTS, logits, neg)
    m1 = jnp.max(lg, axis=-1, keepdims=True)
    i1 = jnp.min(jnp.where(lg == m1, lane_f, float(LANES)), axis=-1, keepdims=True)
    oh1 = lane_f == i1
    lg2 = jnp.where(oh1, neg, lg)
    m2 = jnp.max(lg2, axis=-1, keepdims=True)
    i2 = jnp.min(jnp.where(lg2 == m2, lane_f, float(LANES)), axis=-1, keepdims=True)
    oh2 = lane_f == i2
    e = jnp.exp(m2 - m1)
    w1 = 1.0 / (1.0 + e)
    w2 = e / (1.0 + e)

    tr = lax.broadcasted_iota(jnp.int32, (tb, tb), 0)
    tc = lax.broadcasted_iota(jnp.int32, (tb, tb), 1)
    lower = jnp.where(tc < tr, 1.0, 0.0).astype(BF16)
    oh1f = jnp.where(oh1, 1.0, 0.0)
    oh2f = jnp.where(oh2, 1.0, 0.0)
    cum1 = jnp.dot(lower, oh1f.astype(BF16), preferred_element_type=F32)
    cum2 = jnp.dot(lower, oh2f.astype(BF16), preferred_element_type=F32)
    cnt1 = jnp.sum(oh1f, axis=0, keepdims=True)
    cnt2 = jnp.sum(oh2f, axis=0, keepdims=True)
    pc = jnp.floor((cnt1 + cnt2 + (MOE_GROUP - 1.0)) / MOE_GROUP) * MOE_GROUP
    pj = lax.broadcasted_iota(jnp.int32, (LANES, LANES), 0)
    ps = lax.broadcasted_iota(jnp.int32, (LANES, LANES), 1)
    before = jnp.where(pj < ps, 1.0, 0.0)
    off = jnp.dot(jnp.broadcast_to(pc, (8, LANES)), before, preferred_element_type=F32,
                  precision=lax.Precision.HIGHEST)[0:1]
    pos1 = jnp.sum(jnp.where(oh1, off + cum1, 0.0), axis=-1, keepdims=True)
    pos2 = jnp.sum(jnp.where(oh2, off + cnt1 + cum2, 0.0), axis=-1, keepdims=True)
    meta = jnp.where(lane == 0, pos1,
                     jnp.where(lane == 1, pos2,
                               jnp.where(lane == 2, w1, jnp.where(lane == 3, w2, 0.0))))
    meta_ref[...] = meta
    meta_t_ref[0] = jnp.transpose(meta)[0:8]
    pc_ref[0] = jnp.broadcast_to(pc, (8, LANES))


def _moe_route(a, w_proj, gate_proj, x, gain, shift, scale, router, seq):
    n, d = x.shape
    k = a.shape[1]
    tb = MOE_BLOCK
    nb = n // tb
    bsz = shift.shape[0]
    r1 = router.astype(BF16)
    r2 = (router - r1.astype(F32)).astype(BF16)
    r3 = (router - r1.astype(F32) - r2.astype(F32)).astype(BF16)
    r_pad = jnp.zeros((d, LANES), BF16).at[:, :3 * N_EXPERTS].set(jnp.concatenate([r1, r2, r3], axis=1))
    vec = lambda i: (i * tb // seq, 0, 0)
    return pl.pallas_call(
        _route_kernel,
        out_shape=(
            jax.ShapeDtypeStruct((n, d), F32),
            jax.ShapeDtypeStruct((n, d), BF16),
            jax.ShapeDtypeStruct((n, LANES), F32),
            jax.ShapeDtypeStruct((nb, 8, tb), F32),
            jax.ShapeDtypeStruct((nb, 8, LANES), F32),
        ),
        grid=(nb,),
        in_specs=[
            pl.BlockSpec((tb, k), lambda i: (i, 0)),
            pl.BlockSpec((k, d), lambda i: (0, 0), pipeline_mode=pl.Buffered(1)),
            pl.BlockSpec((1, 1, d), vec),
            pl.BlockSpec((tb, d), lambda i: (i, 0)),
            pl.BlockSpec((1, d), lambda i: (0, 0)),
            pl.BlockSpec((1, 1, d), vec),
            pl.BlockSpec((1, 1, d), vec),
            pl.BlockSpec((d, LANES), lambda i: (0, 0)),
        ],
        out_specs=(
            pl.BlockSpec((tb, d), lambda i: (i, 0)),
            pl.BlockSpec((tb, d), lambda i: (i, 0)),
            pl.BlockSpec((tb, LANES), lambda i: (i, 0)),
            pl.BlockSpec((1, 8, tb), lambda i: (i, 0, 0)),
            pl.BlockSpec((1, 8, LANES), lambda i: (i, 0, 0)),
        ),
        compiler_params=_cparams(("parallel",)),
        name="moe_route",
    )(a, w_proj, gate_proj.reshape(bsz, 1, d), x, gain.reshape(1, d), shift.reshape(bsz, 1, d),
      scale.reshape(bsz, 1, d), r_pad)


def _segment_copies(loc_ref, glb_ref, len_ref, blk, make_copy, wait):
    for e in range(N_EXPERTS):
        idx = blk * N_EXPERTS + e
        loc = loc_ref[idx]
        glb = glb_ref[idx]
        length = len_ref[idx]
        done = jnp.int32(0)
        for sz in SEG_SIZES:
            take = (length & sz) != 0

            @pl.when(take)
            def _(loc=loc, glb=glb, done=done, sz=sz):
                cp = make_copy(pl.multiple_of(loc + done, MOE_GROUP),
                               pl.multiple_of(glb + done, MOE_GROUP), sz)
                if wait:
                    cp.wait()
                else:
                    cp.start()

            done = done + jnp.where(take, sz, 0)


def _dispatch_kernel(loc_ref, glb_ref, len_ref, h_ref, meta_t_ref, hs_in_ref, hs_ref, buf_ref, sem):
    del hs_in_ref
    b = pl.program_id(0)
    tb = MOE_BLOCK
    pos1 = meta_t_ref[0, 0:1, :]
    pos2 = meta_t_ref[0, 1:2, :]
    r = lax.broadcasted_iota(jnp.int32, (MOE_CAP, tb), 0).astype(F32)
    perm = jnp.where((r == pos1) | (r == pos2), 1.0, 0.0).astype(BF16)
    sorted_rows = jnp.dot(perm, h_ref[...], preferred_element_type=F32).astype(BF16)

    def make_copy(loc, glb, sz):
        return pltpu.make_async_copy(buf_ref.at[pl.ds(loc, sz)], hs_ref.at[pl.ds(glb, sz)], sem)

    @pl.when(b > 0)
    def _():
        _segment_copies(loc_ref, glb_ref, len_ref, b - 1, make_copy, wait=True)

    buf_ref[...] = sorted_rows
    _segment_copies(loc_ref, glb_ref, len_ref, b, make_copy, wait=False)

    @pl.when(b == pl.num_programs(0) - 1)
    def _():
        _segment_copies(loc_ref, glb_ref, len_ref, b, make_copy, wait=True)


def _moe_dispatch(h, meta_t, loc, glb, seg_len, rows_total):
    n, d = h.shape
    tb = MOE_BLOCK
    nb = n // tb
    hs0 = jnp.zeros((rows_total, d), BF16)
    return pl.pallas_call(
        _dispatch_kernel,
        out_shape=jax.ShapeDtypeStruct((rows_total, d), BF16),
        grid_spec=pltpu.PrefetchScalarGridSpec(
            num_scalar_prefetch=3,
            grid=(nb,),
            in_specs=[
                pl.BlockSpec((tb, d), lambda i, *_: (i, 0)),
                pl.BlockSpec((1, 8, tb), lambda i, *_: (i, 0, 0)),
                pl.BlockSpec(memory_space=pl.ANY),
            ],
            out_specs=pl.BlockSpec(memory_space=pl.ANY),
            scratch_shapes=[pltpu.VMEM((MOE_CAP, d), BF16), pltpu.SemaphoreType.DMA],
        ),
        input_output_aliases={5: 0},
        compiler_params=_cparams(("arbitrary",)),
        name="moe_dispatch",
    )(loc, glb, seg_len, h, meta_t, hs0)


def _expert_kernel(te_ref, tx_ref, tv_ref, x_ref, wa_ref, wb_ref, wo_ref, o_ref, acc_ref):
    i = pl.program_id(0)
    f = pl.program_id(1)

    last = pl.num_programs(1) - 1
    valid = tv_ref[i] == 1

    def run(emit):
        _swiglu(x_ref[...], lambda sl: wa_ref[0, :, sl], lambda sl: wb_ref[0, :, sl],
                lambda sl: wo_ref[0, :, sl], wo_ref.shape[1], emit)

    @pl.when(jnp.logical_and(valid, f == 0))
    def _():
        def emit(cols, y):
            acc_ref[:, cols] = y
        run(emit)

    if FF_EXPERT // FF_EXPERT_TILE > 2:
        @pl.when(jnp.logical_and(valid, jnp.logical_and(f > 0, f < last)))
        def _():
            def emit(cols, y):
                acc_ref[:, cols] += y
            run(emit)

    @pl.when(jnp.logical_and(valid, f == last))
    def _():
        def emit(cols, y):
            o_ref[:, cols] = (acc_ref[:, cols] + y).astype(o_ref.dtype)
        run(emit)

    @pl.when(jnp.logical_and(tv_ref[i] == 0, f == pl.num_programs(1) - 1))
    def _():
        o_ref[...] = jnp.zeros_like(o_ref)


def _moe_experts(hs, w_in, w_out, tile_expert, tile_row, tile_valid):
    rows, d = hs.shape
    tm = ROW_TILE
    tf = FF_EXPERT_TILE
    nf = FF_EXPERT // tf
    nt = rows // tm

    def f_eff(i, f, tv):
        return jnp.where(tv[i] == 1, f, nf - 1)

    return pl.pallas_call(
        _expert_kernel,
        out_shape=jax.ShapeDtypeStruct((rows, d), BF16),
        grid_spec=pltpu.PrefetchScalarGridSpec(
            num_scalar_prefetch=3,
            grid=(nt, nf),
            in_specs=[
                pl.BlockSpec((tm, d), lambda i, f, te, tx, tv: (tx[i], 0)),
                pl.BlockSpec((1, d, tf), lambda i, f, te, tx, tv: (te[i], 0, f_eff(i, f, tv))),
                pl.BlockSpec((1, d, tf), lambda i, f, te, tx, tv: (te[i], 0, nf + f_eff(i, f, tv))),
                pl.BlockSpec((1, tf, d), lambda i, f, te, tx, tv: (te[i], f_eff(i, f, tv), 0)),
            ],
            out_specs=pl.BlockSpec((tm, d), lambda i, f, te, tx, tv: (i, 0)),
            scratch_shapes=[pltpu.VMEM((tm, d), F32)],
        ),
        compiler_params=_cparams(("arbitrary", "arbitrary")),
        name="moe_experts",
    )(tile_expert, tile_row, tile_valid, hs, w_in, w_in, w_out)


def _combine_kernel(loc_ref, glb_ref, len_ref, x_ref, gate_ref, meta_ref, fg_ref, ys_ref, o_ref, buf_ref, sems,
                    *, final_norm):
    b = pl.program_id(0)
    tb = MOE_BLOCK

    def fetch(blk, slot, wait):
        def make_copy(loc, glb, sz):
            return pltpu.make_async_copy(ys_ref.at[pl.ds(glb, sz)], buf_ref.at[slot, pl.ds(loc, sz)],
                                         sems.at[slot])

        _segment_copies(loc_ref, glb_ref, len_ref, blk, make_copy, wait)

    def start(blk, slot):
        buf_ref[slot] = jnp.zeros(buf_ref.shape[1:], buf_ref.dtype)
        fetch(blk, slot, wait=False)

    @pl.when(b == 0)
    def _():
        start(0, 0)

    @pl.when(b + 1 < pl.num_programs(0))
    def _():
        start(b + 1, (b + 1) % 2)

    slot = b % 2
    meta = meta_ref[...]
    pos1 = meta[:, 0:1]
    pos2 = meta[:, 1:2]
    w1 = meta[:, 2:3]
    w2 = meta[:, 3:4]
    r = lax.broadcasted_iota(jnp.int32, (tb, MOE_CAP), 1).astype(F32)
    sel1 = jnp.where(r == pos1, 1.0, 0.0).astype(BF16)
    sel2 = jnp.where(r == pos2, 1.0, 0.0).astype(BF16)
    fetch(b, slot, wait=True)
    ys = buf_ref[slot]
    y = (w1 * jnp.dot(sel1, ys, preferred_element_type=F32)
         + w2 * jnp.dot(sel2, ys, preferred_element_type=F32))
    x = x_ref[...] + gate_ref[0] * y
    if final_norm:
        x = (x * lax.rsqrt(jnp.mean(x * x, axis=-1, keepdims=True) + EPS)) * fg_ref[...]
    o_ref[...] = x


def _moe_combine(x, gate, meta, ys, loc, glb, seg_len, seq, final_gain=None):
    n, d = x.shape
    tb = MOE_BLOCK
    bsz = gate.shape[0]
    fg = jnp.ones((1, d), F32) if final_gain is None else final_gain.reshape(1, d)
    return pl.pallas_call(
        functools.partial(_combine_kernel, final_norm=final_gain is not None),
        out_shape=jax.ShapeDtypeStruct((n, d), F32),
        grid_spec=pltpu.PrefetchScalarGridSpec(
            num_scalar_prefetch=3,
            grid=(n // tb,),
            in_specs=[
                pl.BlockSpec((tb, d), lambda i, *_: (i, 0)),
                pl.BlockSpec((1, 1, d), lambda i, *_: (i * tb // seq, 0, 0)),
                pl.BlockSpec((tb, LANES), lambda i, *_: (i, 0)),
                pl.BlockSpec((1, d), lambda i, *_: (0, 0)),
                pl.BlockSpec(memory_space=pl.ANY),
            ],
            out_specs=pl.BlockSpec((tb, d), lambda i, *_: (i, 0)),
            scratch_shapes=[pltpu.VMEM((2, MOE_CAP, d), BF16), pltpu.SemaphoreType.DMA((2,))],
        ),
        compiler_params=_cparams(("arbitrary",)),
        name="moe_combine",
    )(loc, glb, seg_len, x, gate.reshape(bsz, 1, d), meta, fg, ys)


def _moe_layout(pc):
    nb = pc.shape[0]
    tm = ROW_TILE
    n_tiles = (nb * (2 * MOE_BLOCK + N_EXPERTS * (MOE_GROUP - 1)) + tm - 1) // tm + N_EXPERTS
    loc = jnp.cumsum(pc, axis=1) - pc
    tiles_e = (jnp.sum(pc, axis=0) + tm - 1) // tm
    tile_end = jnp.cumsum(tiles_e)
    base_e = (tile_end - tiles_e) * tm
    glb = base_e[None, :] + jnp.cumsum(pc, axis=0) - pc
    n_valid = tile_end[-1]
    ids = jnp.arange(n_tiles, dtype=jnp.int32)
    tile_row = jnp.maximum(jnp.minimum(ids, n_valid - 1), 0)
    tile_expert = jnp.sum((tile_row[:, None] >= tile_end[None, :]).astype(jnp.int32), axis=1)
    tile_valid = (ids < n_valid).astype(jnp.int32)
    flat = lambda a: a.reshape(-1).astype(jnp.int32)
    return flat(loc), flat(glb), flat(pc), tile_expert.astype(jnp.int32), tile_row.astype(jnp.int32), tile_valid, n_tiles * tm


def _moe_block(a, w_proj, gate_proj, x, gain, shift, scale, gate, router, w_in, w_out, layer, seq,
               final_gain=None):
    x, h, meta, meta_t, pc = _moe_route(a, w_proj, gate_proj, x, gain, shift, scale, router, seq)
    pc_i = pc[:, 0, :N_EXPERTS].astype(jnp.int32)
    loc, glb, seg_len, tile_expert, tile_row, tile_valid, rows_total = _moe_layout(pc_i)
    hs = _moe_dispatch(h, meta_t, loc, glb, seg_len, rows_total)
    ys = _moe_experts(hs, w_in, w_out, tile_expert + layer * N_EXPERTS, tile_row, tile_valid)
    return _moe_combine(x, gate, meta, ys, loc, glb, seg_len, seq, final_gain)


def _final_norm_kernel(x_ref, g_ref, o_ref):
    x = x_ref[...]
    o_ref[...] = (x * lax.rsqrt(jnp.mean(x * x, axis=-1, keepdims=True) + EPS)) * g_ref[...]


def _final_norm(x, gain):
    n, d = x.shape
    tm = min(ROW_TILE, n)
    return pl.pallas_call(
        _final_norm_kernel,
        out_shape=jax.ShapeDtypeStruct((n, d), F32),
        grid=(n // tm,),
        in_specs=[pl.BlockSpec((tm, d), lambda i: (i, 0)), pl.BlockSpec((1, d), lambda i: (0, 0))],
        out_specs=pl.BlockSpec((tm, d), lambda i: (i, 0)),
        compiler_params=_cparams(("parallel",)),
        name="final_norm",
    )(x, gain.reshape(1, d))


def kernel(x, c, positions, ada_w, ada_b, norm_mix, norm_ff, ret_w_in, ret_gn, ret_w_out, kv_ada_w, kv_ada_b, kv_norm, kv_w, sb_w_q, sb_w_out, ff_w_in, ff_w_out, moe_router, moe_w_in, moe_w_out, final_norm):
    bsz, seq, d = x.shape
    n = bsz * seq
    mod = _ada_mod(c, ada_w, ada_b)
    kv_mod = _ada_mod(c, kv_ada_w[None], kv_ada_b[None])[0]
    rope = _rope_tables(positions)
    ret_tables = _retention_tables()
    moe_in = moe_w_in.astype(BF16).reshape(-1, d, 2 * FF_EXPERT)
    moe_out = moe_w_out.astype(BF16).reshape(-1, FF_EXPERT, d)
    xf = x.reshape(n, d)
    kv = None
    for i in range(DEPTH):
        sh_m, sc_m, g_m, sh_f, sc_f, g_f = [mod[i, :, p * d:(p + 1) * d] for p in range(6)]
        if i < N_A:
            proj = _norm_mod_matmul(xf, norm_mix[i], sh_m, sc_m, ret_w_in[i].astype(BF16), seq, rope=rope)
            o = _retention_core(proj.reshape(bsz, seq, -1), ret_gn[i], ret_tables, bsz, seq)
            o, w_o = o.reshape(n, -1), ret_w_out[i].astype(BF16)
        else:
            j = i - N_A
            q = _norm_mod_matmul(xf, norm_mix[i], sh_m, sc_m, sb_w_q[j].astype(BF16), seq)
            o = _stick_breaking_core(q.reshape(bsz, seq, d), kv, bsz, seq)
            o, w_o = o.reshape(n, d), sb_w_out[j].astype(BF16)
        if i % 2 == 0:
            xf = _dense_ffn(o, w_o, g_m, xf, norm_ff[i], sh_f, sc_f, g_f, ff_w_in[i // 2].astype(BF16),
                            ff_w_out[i // 2].astype(BF16), seq)
        else:
            xf = _moe_block(o, w_o, g_m, xf, norm_ff[i], sh_f, sc_f, g_f, moe_router[i // 2],
                            moe_in, moe_out, i // 2, seq,
                            final_gain=final_norm if i == DEPTH - 1 else None)
        if i == N_A - 1:
            kv = _norm_mod_matmul(xf, kv_norm, kv_mod[:, :d], kv_mod[:, d:], kv_w.astype(BF16), seq)
            kv = kv.reshape(bsz, seq, 2 * d)
    if (DEPTH - 1) % 2 == 0:
        xf = _final_norm(xf, final_norm)
    return xf.reshape(bsz, seq, d)
```

```python
import functools

import jax
import jax.numpy as jnp
from jax import lax
from jax.experimental import pallas as pl
from jax.experimental.pallas import tpu as pltpu

F32 = jnp.float32
BF16 = jnp.bfloat16

D_MODEL = 1024
DEPTH = 4
N_A = DEPTH // 2
CHUNK = 64
RET_HEADS = 8
RET_DK = D_MODEL // RET_HEADS
RET_DV = 2 * D_MODEL // RET_HEADS
ROPE_BASE = 10000.0
SB_HEADS = 16
SB_DH = D_MODEL // SB_HEADS
FF_DENSE = 2816
N_EXPERTS = 8
FF_EXPERT = 3584
EPS = 1e-6

LANES = 128
BF16_ROWS = 16
MXU_COLS = 256
VMEM_LIMIT = 56 * 1024 * 1024

ROW_TILE = 1024
NMM_ROWS = 512
NMM_WIDE = 2048
NMM_CHUNK = 512
DENSE_ROWS = 512
FF_EXPERT_TILE = 1792
RET_BLOCK = 256
SB_TILE = 128
MOE_BLOCK = 512
MOE_GROUP = BF16_ROWS
MOE_CAP = 2 * MOE_BLOCK + N_EXPERTS * MOE_GROUP
SEG_SIZES = (512, 256, 128, 64, 32, 16)
SB_LOG_FLOOR = -110.0


def _cparams(sem):
    return pltpu.CompilerParams(dimension_semantics=sem, vmem_limit_bytes=VMEM_LIMIT)


def _sigmoid(v):
    return 1.0 / (1.0 + jnp.exp(-v))


def _rms_mod(x, gain, shift, scale):
    y = x * lax.rsqrt(jnp.mean(x * x, axis=-1, keepdims=True) + EPS)
    return (y * gain) * (1.0 + scale) + shift


def _ada_kernel(c_ref, w_ref, b_ref, o_ref):
    c = c_ref[...]
    cond = (c * _sigmoid(c)).astype(BF16)
    o_ref[0] = jnp.dot(cond, w_ref[0].astype(BF16), preferred_element_type=F32) + b_ref[0]


def _ada_mod(c, w, b):
    nl, d, n = w.shape
    bsz = c.shape[0]
    tn = 1024
    return pl.pallas_call(
        _ada_kernel,
        out_shape=jax.ShapeDtypeStruct((nl, bsz, n), F32),
        grid=(nl, n // tn),
        in_specs=[
            pl.BlockSpec((bsz, d), lambda l, j: (0, 0)),
            pl.BlockSpec((1, d, tn), lambda l, j: (l, 0, j)),
            pl.BlockSpec((1, 1, tn), lambda l, j: (l, 0, j)),
        ],
        out_specs=pl.BlockSpec((1, bsz, tn), lambda l, j: (l, 0, j)),
        compiler_params=_cparams(("parallel", "parallel")),
        name="ada_mod",
    )(c, w, b.reshape(nl, 1, n))


def _rope_table_kernel(pos_ref, freq_ref, cs_ref, sn_ref):
    ang = pos_ref[...] * freq_ref[...]
    lane = lax.broadcasted_iota(jnp.int32, ang.shape, 1)
    cs_ref[...] = jnp.cos(ang)
    sn_ref[...] = jnp.where(lane < RET_DK // 2, -jnp.sin(ang), jnp.sin(ang))


def _rope_tables(positions):
    n = positions.size
    inv_freq = jnp.power(ROPE_BASE, -jnp.arange(0, RET_DK, 2, dtype=F32) / RET_DK)
    freq = jnp.concatenate([inv_freq, inv_freq]).reshape(1, RET_DK)
    pos = jnp.broadcast_to(positions.astype(F32).reshape(n, 1), (n, RET_DK))
    tr = min(n, 2048)
    return pl.pallas_call(
        _rope_table_kernel,
        out_shape=(jax.ShapeDtypeStruct((n, RET_DK), F32),) * 2,
        grid=(n // tr,),
        in_specs=[pl.BlockSpec((tr, RET_DK), lambda i: (i, 0)),
                  pl.BlockSpec((1, RET_DK), lambda i: (0, 0))],
        out_specs=(pl.BlockSpec((tr, RET_DK), lambda i: (i, 0)),) * 2,
        compiler_params=_cparams(("parallel",)),
        name="rope_tables",
    )(pos, freq)


def _nmm_kernel(x_ref, g_ref, sh_ref, sc_ref, w_ref, *rest, rope_cols):
    if rope_cols:
        cs_ref, sn_ref, o_ref = rest
    else:
        (o_ref,) = rest
    cw = NMM_CHUNK
    n_chunks = w_ref.shape[1] // cw
    h = _rms_mod(x_ref[...], g_ref[...], sh_ref[0], sc_ref[0]).astype(BF16)

    def mm(c):
        return jnp.dot(h, w_ref[:, c * cw:(c + 1) * cw], preferred_element_type=F32)

    nxt = mm(0)
    for c in range(n_chunks):
        r = nxt
        if c + 1 < n_chunks:
            nxt = mm(c + 1)
        col0 = c * cw
        if col0 >= rope_cols:
            o_ref[:, col0:col0 + cw] = r.astype(o_ref.dtype)
            continue
        k_scale = RET_DK ** -0.5 if col0 >= rope_cols // 2 else None
        for s in range(cw // RET_DK):
            t = r[:, s * RET_DK:(s + 1) * RET_DK]
            rot = t * cs_ref[...] + pltpu.roll(t, RET_DK // 2, 1) * sn_ref[...]
            if k_scale is not None:
                rot = rot * k_scale
            o_ref[:, col0 + s * RET_DK:col0 + (s + 1) * RET_DK] = rot.astype(o_ref.dtype)


def _norm_mod_matmul(x, gain, shift, scale, w, seq, rope=None):
    n, d = x.shape
    m = w.shape[1]
    tm = min(NMM_ROWS if m > NMM_WIDE else ROW_TILE, seq)
    bsz = shift.shape[0]
    vec = lambda i: (i * tm // seq, 0, 0)
    in_specs = [
        pl.BlockSpec((tm, d), lambda i: (i, 0)),
        pl.BlockSpec((1, d), lambda i: (0, 0)),
        pl.BlockSpec((1, 1, d), vec),
        pl.BlockSpec((1, 1, d), vec),
        pl.BlockSpec((d, m), lambda i: (0, 0), pipeline_mode=pl.Buffered(1)),
    ]
    args = [x, gain.reshape(1, d), shift.reshape(bsz, 1, d), scale.reshape(bsz, 1, d), w]
    rope_cols = 0
    if rope is not None:
        rope_cols = 2 * RET_HEADS * RET_DK
        in_specs += [pl.BlockSpec((tm, RET_DK), lambda i: (i, 0))] * 2
        args += list(rope)
    return pl.pallas_call(
        functools.partial(_nmm_kernel, rope_cols=rope_cols),
        out_shape=jax.ShapeDtypeStruct((n, m), BF16),
        grid=(n // tm,),
        in_specs=in_specs,
        out_specs=pl.BlockSpec((tm, m), lambda i: (i, 0)),
        compiler_params=_cparams(("parallel",)),
        name="norm_mod_matmul_rope" if rope_cols else "norm_mod_matmul",
    )(*args)


def _swiglu(h, load_wa, load_wb, load_wo, width, emit):
    def up(c):
        sl = slice(c * MXU_COLS, (c + 1) * MXU_COLS)
        return (jnp.dot(h, load_wa(sl), preferred_element_type=F32),
                jnp.dot(h, load_wb(sl), preferred_element_type=F32))

    n_chunks = width // MXU_COLS
    acts = []
    nxt = up(0)
    for c in range(n_chunks):
        a, b = nxt
        if c + 1 < n_chunks:
            nxt = up(c + 1)
        acts.append((a * _sigmoid(a) * b).astype(BF16))
    act = jnp.concatenate(acts, axis=1)

    def down(c):
        return jnp.dot(act, load_wo(slice(c * MXU_COLS, (c + 1) * MXU_COLS)), preferred_element_type=F32)

    n_out = D_MODEL // MXU_COLS
    nxt = down(0)
    for c in range(n_out):
        y = nxt
        if c + 1 < n_out:
            nxt = down(c + 1)
        emit(slice(c * MXU_COLS, (c + 1) * MXU_COLS), y)


def _ffn_kernel(a_ref, wp_ref, gp_ref, x_ref, g_ref, sh_ref, sc_ref, gate_ref, wa_ref, wb_ref, wo_ref, o_ref,
                x1_ref):
    x1_ref[...] = x_ref[...] + gp_ref[0] * jnp.dot(a_ref[...], wp_ref[...], preferred_element_type=F32)
    h = _rms_mod(x1_ref[...], g_ref[...], sh_ref[0], sc_ref[0]).astype(BF16)

    def emit(cols, y):
        o_ref[:, cols] = x1_ref[:, cols] + gate_ref[0, :, cols] * y

    _swiglu(h, lambda sl: wa_ref[:, sl], lambda sl: wb_ref[:, sl], lambda sl: wo_ref[:, sl],
            wo_ref.shape[0], emit)


def _dense_ffn(a, w_proj, gate_proj, x, gain, shift, scale, gate, w_in, w_out, seq):
    n, d = x.shape
    k = a.shape[1]
    ff = w_out.shape[0]
    tm = min(DENSE_ROWS, seq)
    bsz = shift.shape[0]
    vec = lambda i: (i * tm // seq, 0, 0)
    resident = pl.Buffered(1)
    return pl.pallas_call(
        _ffn_kernel,
        out_shape=jax.ShapeDtypeStruct((n, d), F32),
        grid=(n // tm,),
        in_specs=[
            pl.BlockSpec((tm, k), lambda i: (i, 0)),
            pl.BlockSpec((k, d), lambda i: (0, 0), pipeline_mode=resident),
            pl.BlockSpec((1, 1, d), vec),
            pl.BlockSpec((tm, d), lambda i: (i, 0)),
            pl.BlockSpec((1, d), lambda i: (0, 0)),
            pl.BlockSpec((1, 1, d), vec),
            pl.BlockSpec((1, 1, d), vec),
            pl.BlockSpec((1, 1, d), vec),
            pl.BlockSpec((d, ff), lambda i: (0, 0), pipeline_mode=resident),
            pl.BlockSpec((d, ff), lambda i: (0, 1), pipeline_mode=resident),
            pl.BlockSpec((ff, d), lambda i: (0, 0), pipeline_mode=resident),
        ],
        out_specs=pl.BlockSpec((tm, d), lambda i: (i, 0)),
        scratch_shapes=[pltpu.VMEM((tm, d), F32)],
        compiler_params=_cparams(("parallel",)),
        name="dense_ffn",
    )(a, w_proj, gate_proj.reshape(bsz, 1, d), x, gain.reshape(1, d), shift.reshape(bsz, 1, d),
      scale.reshape(bsz, 1, d), gate.reshape(bsz, 1, d), w_in, w_in, w_out)


def _ret_kernel(q_ref, k_ref, v_ref, g_ref, gn_ref, dm_ref, qd_ref, kd_ref, cd_ref, o_ref, *, nblk):
    dmat = dm_ref[0]
    qdec = qd_ref[0]
    kdec = kd_ref[0]
    cdec = cd_ref[0, 0:1, :]
    gn = gn_ref[...]

    def load(t):
        sl = slice(t * RET_BLOCK, (t + 1) * RET_BLOCK)
        return q_ref[0, sl, :], k_ref[0, sl, :], v_ref[0, sl, :]

    def scores(q, k):
        return lax.dot_general(q, k, (((1,), (1,)), ((), ())), preferred_element_type=F32)

    state = jnp.zeros((RET_DK, RET_DV), F32)
    q, k, v = load(0)
    s = scores(q, k)
    for t in range(nblk):
        qd = (q.astype(F32) * qdec).astype(BF16)
        cross = jnp.dot(qd, state.astype(BF16), preferred_element_type=F32)
        kdt = jnp.transpose(k.astype(F32) * kdec).astype(BF16)
        update = jnp.dot(kdt, v, preferred_element_type=F32)
        if t + 1 < nblk:
            q, k, v_next = load(t + 1)
            s_next = scores(q, k)
        intra = jnp.dot((s * dmat).astype(BF16), v, preferred_element_type=F32)
        state = state * cdec + update
        o = intra + cross
        mu = jnp.mean(o, axis=-1, keepdims=True)
        oc = o - mu
        var = jnp.mean(oc * oc, axis=-1, keepdims=True)
        on = (oc * lax.rsqrt(var + EPS)) * gn
        sl = slice(t * RET_BLOCK, (t + 1) * RET_BLOCK)
        g = g_ref[0, sl, :].astype(F32)
        o_ref[0, sl, :] = ((g * _sigmoid(g)) * on).astype(o_ref.dtype)
        if t + 1 < nblk:
            v, s = v_next, s_next


def _retention_tables():
    t = RET_BLOCK
    log_gamma = jnp.log(1.0 - jnp.exp2(-5.0 - jnp.arange(RET_HEADS, dtype=F32)))
    pos = jnp.arange(t, dtype=F32)
    chunk = jnp.arange(t) // CHUNK
    dist = jnp.abs(pos[:, None] - pos[None, :])
    visible = (chunk[None, :] <= chunk[:, None]).astype(F32)
    dmat = jnp.exp(dist[None] * log_gamma[:, None, None]) * visible[None]
    qdec = jnp.exp((pos[None, :] + 1.0) * log_gamma[:, None])
    kdec = jnp.exp((t - 1.0 - pos)[None, :] * log_gamma[:, None])
    cdec = jnp.exp(t * log_gamma)
    qdec = jnp.broadcast_to(qdec[:, :, None], (RET_HEADS, t, RET_DK))
    kdec = jnp.broadcast_to(kdec[:, :, None], (RET_HEADS, t, RET_DK))
    cdec = jnp.broadcast_to(cdec[:, None, None], (RET_HEADS, 8, RET_DV))
    return dmat, qdec, kdec, cdec


def _retention_core(proj, gn_w, tables, bsz, seq):
    dmat, qdec, kdec, cdec = tables
    h = RET_HEADS
    t = RET_BLOCK
    kq = h * RET_DK // RET_DK
    kv = 2 * h * RET_DK // RET_DV
    return pl.pallas_call(
        functools.partial(_ret_kernel, nblk=seq // t),
        out_shape=jax.ShapeDtypeStruct((bsz, seq, h * RET_DV), BF16),
        grid=(bsz, h),
        in_specs=[
            pl.BlockSpec((1, seq, RET_DK), lambda b, i: (b, 0, i)),
            pl.BlockSpec((1, seq, RET_DK), lambda b, i: (b, 0, kq + i)),
            pl.BlockSpec((1, seq, RET_DV), lambda b, i: (b, 0, kv + i)),
            pl.BlockSpec((1, seq, RET_DV), lambda b, i: (b, 0, kv + h + i)),
            pl.BlockSpec((1, RET_DV), lambda b, i: (0, i)),
            pl.BlockSpec((1, t, t), lambda b, i: (i, 0, 0)),
            pl.BlockSpec((1, t, RET_DK), lambda b, i: (i, 0, 0)),
            pl.BlockSpec((1, t, RET_DK), lambda b, i: (i, 0, 0)),
            pl.BlockSpec((1, 8, RET_DV), lambda b, i: (i, 0, 0)),
        ],
        out_specs=pl.BlockSpec((1, seq, RET_DV), lambda b, i: (b, 0, i)),
        compiler_params=_cparams(("parallel", "parallel")),
        name="retention_core",
    )(proj, proj, proj, proj, gn_w.reshape(1, h * RET_DV), dmat, qdec, kdec, cdec)


def _sb_kernel(q_ref, k_ref, v_ref, o_ref, r_ref, acc_ref):
    t = SB_TILE
    w = 2 * SB_DH
    pairs = SB_HEADS // 2
    qi = pl.program_id(1)
    lane = lax.broadcasted_iota(jnp.int32, (t, w), 1)
    row = lax.broadcasted_iota(jnp.int32, (t, 2 * t), 0)
    col = lax.broadcasted_iota(jnp.int32, (t, 2 * t), 1)
    diag_causal = jnp.where(col >= t, col - t, col) < row
    uj = lax.broadcasted_iota(jnp.int32, (2 * t, 2 * t), 0)
    us = lax.broadcasted_iota(jnp.int32, (2 * t, 2 * t), 1)
    uj = jnp.where(uj >= t, uj - t, uj)
    umat = jnp.where((uj >= us) | (us >= t), -1.0, 0.0).astype(BF16)
    scale = jnp.asarray(SB_DH ** -0.5, BF16)
    qs = [q_ref[0, :, p * w:(p + 1) * w] * scale for p in range(pairs)]

    def split_heads(x):
        zero = jnp.zeros_like(x)
        return jnp.concatenate([jnp.where(lane < SB_DH, x, zero), jnp.where(lane >= SB_DH, x, zero)], axis=0)

    def key_tiles(kbs, causal, first, ps):
        spans = [pl.ds(pl.multiple_of(kb * t, t), t) for kb in kbs]
        tiles = [(j, p) for j in range(len(kbs)) for p in ps]
        zs, his, los, incls, tots = {}, {}, {}, {}, {}
        for j, p in tiles:
            kcat = split_heads(k_ref[0, spans[j], p * w:(p + 1) * w])
            zs[j, p] = lax.dot_general(qs[p], kcat, (((1,), (1,)), ((), ())), preferred_element_type=F32)
        for j, p in tiles:
            z = zs[j, p]
            sp = jnp.maximum(z, 0.0) + jnp.log(1.0 + jnp.exp(-jnp.abs(z)))
            if causal is not None and j == 0:
                sp = jnp.where(causal, sp, 0.0)
            hi = sp.astype(BF16)
            his[j, p] = hi
            los[j, p] = (sp - hi.astype(F32)).astype(BF16)
        for j, p in tiles:
            cs = [jnp.dot(jnp.concatenate([his[j, p][:, h * t:(h + 1) * t], los[j, p][:, h * t:(h + 1) * t]],
                                          axis=1), umat, preferred_element_type=F32) for h in range(2)]
            incls[j, p] = jnp.concatenate([cs[0][:, :t], cs[1][:, :t]], axis=1)
            tots[j, p] = jnp.concatenate([cs[0][:, t:], cs[1][:, t:]], axis=1)
        avs = {}
        rs = {p: None if first else r_ref[p] for p in ps}
        for j, p in tiles:
            e = zs[j, p] + incls[j, p]
            a = jnp.exp(e if rs[p] is None else e + rs[p])
            if causal is not None and j == 0:
                a = jnp.where(causal, a, 0.0)
            avs[j, p] = a.astype(BF16)
            rs[p] = tots[j, p] if rs[p] is None else rs[p] + tots[j, p]
        for p in ps:
            r_ref[p] = rs[p]
        accs = {p: None if first else acc_ref[p] for p in ps}
        for j, p in tiles:
            vcat = split_heads(v_ref[0, spans[j], p * w:(p + 1) * w])
            av = jnp.dot(avs[j, p], vcat, preferred_element_type=F32)
            accs[p] = av if accs[p] is None else accs[p] + av
        for p in ps:
            acc_ref[p] = accs[p]
        return tuple((jnp.max(rs[p]) > SB_LOG_FLOOR).astype(jnp.int32) for p in ps)

    everyone = list(range(pairs))
    left = key_tiles([qi], diag_causal, True, everyone)
    kb = qi - 1

    two = jnp.logical_and(kb >= 1, sum(left) > 0)
    left = lax.cond(two, lambda: key_tiles([kb, kb - 1], None, False, everyone), lambda: left)
    kb = jnp.where(two, kb - 2, kb)

    def cond(carry):
        kb, *left = carry
        return jnp.logical_and(kb >= 0, sum(left) > 0)

    def body(carry):
        kb, *left = carry

        def stragglers():
            return tuple(lax.cond(left[p] > 0, lambda p=p: key_tiles([kb], None, False, [p])[0],
                                  lambda: jnp.int32(0)) for p in everyone)

        left = lax.cond(sum(left) > pairs // 2, lambda: key_tiles([kb], None, False, everyone), stragglers)
        return (kb - 1, *left)

    lax.while_loop(cond, body, (kb, *left))
    for p in range(pairs):
        o_ref[0, :, p * w:(p + 1) * w] = acc_ref[p].astype(o_ref.dtype)


def _stick_breaking_core(q, kv, bsz, seq):
    t = SB_TILE
    return pl.pallas_call(
        _sb_kernel,
        out_shape=jax.ShapeDtypeStruct((bsz, seq, D_MODEL), BF16),
        grid=(bsz, seq // t),
        in_specs=[
            pl.BlockSpec((1, t, D_MODEL), lambda b, i: (b, i, 0)),
            pl.BlockSpec((1, seq, D_MODEL), lambda b, i: (b, 0, 0)),
            pl.BlockSpec((1, seq, D_MODEL), lambda b, i: (b, 0, 1)),
        ],
        out_specs=pl.BlockSpec((1, t, D_MODEL), lambda b, i: (b, i, 0)),
        scratch_shapes=[pltpu.VMEM((SB_HEADS // 2, t, 2 * t), F32),
                        pltpu.VMEM((SB_HEADS // 2, t, 2 * SB_DH), F32)],
        compiler_params=_cparams(("parallel", "parallel")),
        name="stick_breaking_core",
    )(q, kv, kv)


def _route_kernel(a_ref, wp_ref, gp_ref, x_ref, g_ref, sh_ref, sc_ref, r_ref,
                  x1_ref, h_ref, meta_ref, meta_t_ref, pc_ref):
    tb = MOE_BLOCK
    x1 = x_ref[...] + gp_ref[0] * jnp.dot(a_ref[...], wp_ref[...], preferred_element_type=F32)
    x1_ref[...] = x1
    h32 = _rms_mod(x1, g_ref[...], sh_ref[0], sc_ref[0])
    h1 = h32.astype(BF16)
    h_ref[...] = h1
    e1 = h32 - h1.astype(F32)
    h2 = e1.astype(BF16)
    h3 = (e1 - h2.astype(F32)).astype(BF16)
    r3 = r_ref[...]
    s3 = (jnp.dot(h1, r3, preferred_element_type=F32) + jnp.dot(h2, r3, preferred_element_type=F32)
          + jnp.dot(h3, r3, preferred_element_type=F32))
    logits = s3 + pltpu.roll(s3, LANES - N_EXPERTS, 1) + pltpu.roll(s3, LANES - 2 * N_EXPERTS, 1)
    lane = lax.broadcasted_iota(jnp.int32, (tb, LANES), 1)
    lane_f = lane.astype(F32)
    neg = jnp.float32(-jnp.inf)
    lg = jnp.where(lane < N_EXPERTS, logits, neg)
    m1 = jnp.max(lg, axis=-1, keepdims=True)
    i1 = jnp.min(jnp.where(lg == m1, lane_f, float(LANES)), axis=-1, keepdims=True)
    oh1 = lane_f == i1
    lg2 = jnp.where(oh1, neg, lg)
    m2 = jnp.max(lg2, axis=-1, keepdims=True)
    i2 = jnp.min(jnp.where(lg2 == m2, lane_f, float(LANES)), axis=-1, keepdims=True)
    oh2 = lane_f == i2
    e = jnp.exp(m2 - m1)
    w1 = 1.0 / (1.0 + e)
    w2 = e / (1.0 + e)

    tr = lax.broadcasted_iota(jnp.int32, (tb, tb), 0)
    tc = lax.broadcasted_iota(jnp.int32, (tb, tb), 1)
    lower = jnp.where(tc < tr, 1.0, 0.0).astype(BF16)
    oh1f = jnp.where(oh1, 1.0, 0.0)
    oh2f = jnp.where(oh2, 1.0, 0.0)
    cum1 = jnp.dot(lower, oh1f.astype(BF16), preferred_element_type=F32)
    cum2 = jnp.dot(lower, oh2f.astype(BF16), preferred_element_type=F32)
    cnt1 = jnp.sum(oh1f, axis=0, keepdims=True)
    cnt2 = jnp.sum(oh2f, axis=0, keepdims=True)
    pc = jnp.floor((cnt1 + cnt2 + (MOE_GROUP - 1.0)) / MOE_GROUP) * MOE_GROUP
    pj = lax.broadcasted_iota(jnp.int32, (LANES, LANES), 0)
    ps = lax.broadcasted_iota(jnp.int32, (LANES, LANES), 1)
    before = jnp.where(pj < ps, 1.0, 0.0)
    off = jnp.dot(jnp.broadcast_to(pc, (8, LANES)), before, preferred_element_type=F32,
                  precision=lax.Precision.HIGHEST)[0:1]
    pos1 = jnp.sum(jnp.where(oh1, off + cum1, 0.0), axis=-1, keepdims=True)
    pos2 = jnp.sum(jnp.where(oh2, off + cnt1 + cum2, 0.0), axis=-1, keepdims=True)
    meta = jnp.where(lane == 0, pos1,
                     jnp.where(lane == 1, pos2,
                               jnp.where(lane == 2, w1, jnp.where(lane == 3, w2, 0.0))))
    meta_ref[...] = meta
    meta_t_ref[0] = jnp.transpose(meta)[0:8]
    pc_ref[0] = jnp.broadcast_to(pc, (8, LANES))


def _moe_route(a, w_proj, gate_proj, x, gain, shift, scale, router, seq):
    n, d = x.shape
    k = a.shape[1]
    tb = MOE_BLOCK
    nb = n // tb
    bsz = shift.shape[0]
    r1 = router.astype(BF16)
    r2 = (router - r1.astype(F32)).astype(BF16)
    r3 = (router - r1.astype(F32) - r2.astype(F32)).astype(BF16)
    r_pad = jnp.zeros((d, LANES), BF16).at[:, :3 * N_EXPERTS].set(jnp.concatenate([r1, r2, r3], axis=1))
    vec = lambda i: (i * tb // seq, 0, 0)
    return pl.pallas_call(
        _route_kernel,
        out_shape=(
            jax.ShapeDtypeStruct((n, d), F32),
            jax.ShapeDtypeStruct((n, d), BF16),
            jax.ShapeDtypeStruct((n, LANES), F32),
            jax.ShapeDtypeStruct((nb, 8, tb), F32),
            jax.ShapeDtypeStruct((nb, 8, LANES), F32),
        ),
        grid=(nb,),
        in_specs=[
            pl.BlockSpec((tb, k), lambda i: (i, 0)),
            pl.BlockSpec((k, d), lambda i: (0, 0), pipeline_mode=pl.Buffered(1)),
            pl.BlockSpec((1, 1, d), vec),
            pl.BlockSpec((tb, d), lambda i: (i, 0)),
            pl.BlockSpec((1, d), lambda i: (0, 0)),
            pl.BlockSpec((1, 1, d), vec),
            pl.BlockSpec((1, 1, d), vec),
            pl.BlockSpec((d, LANES), lambda i: (0, 0)),
        ],
        out_specs=(
            pl.BlockSpec((tb, d), lambda i: (i, 0)),
            pl.BlockSpec((tb, d), lambda i: (i, 0)),
            pl.BlockSpec((tb, LANES), lambda i: (i, 0)),
            pl.BlockSpec((1, 8, tb), lambda i: (i, 0, 0)),
            pl.BlockSpec((1, 8, LANES), lambda i: (i, 0, 0)),
        ),
        compiler_params=_cparams(("parallel",)),
        name="moe_route",
    )(a, w_proj, gate_proj.reshape(bsz, 1, d), x, gain.reshape(1, d), shift.reshape(bsz, 1, d),
      scale.reshape(bsz, 1, d), r_pad)


def _segment_copies(loc_ref, glb_ref, len_ref, blk, make_copy, wait):
    for e in range(N_EXPERTS):
        idx = blk * N_EXPERTS + e
        loc = loc_ref[idx]
        glb = glb_ref[idx]
        length = len_ref[idx]
        done = jnp.int32(0)
        for sz in SEG_SIZES:
            take = (length & sz) != 0

            @pl.when(take)
            def _(loc=loc, glb=glb, done=done, sz=sz):
                cp = make_copy(pl.multiple_of(loc + done, MOE_GROUP),
                               pl.multiple_of(glb + done, MOE_GROUP), sz)
                if wait:
                    cp.wait()
                else:
                    cp.start()

            done = done + jnp.where(take, sz, 0)


def _dispatch_kernel(loc_ref, glb_ref, len_ref, h_ref, meta_t_ref, hs_ref, buf_ref, zero_ref, sem):
    b = pl.program_id(0)
    tb = MOE_BLOCK
    pos1 = meta_t_ref[0, 0:1, :]
    pos2 = meta_t_ref[0, 1:2, :]
    r = lax.broadcasted_iota(jnp.int32, (MOE_CAP, tb), 0).astype(F32)
    perm = jnp.where((r == pos1) | (r == pos2), 1.0, 0.0).astype(BF16)
    sorted_rows = jnp.dot(perm, h_ref[...], preferred_element_type=F32).astype(BF16)

    def make_copy(loc, glb, sz):
        return pltpu.make_async_copy(buf_ref.at[pl.ds(loc, sz)], hs_ref.at[pl.ds(glb, sz)], sem)

    @pl.when(b > 0)
    def _():
        _segment_copies(loc_ref, glb_ref, len_ref, b - 1, make_copy, wait=True)

    buf_ref[...] = sorted_rows
    _segment_copies(loc_ref, glb_ref, len_ref, b, make_copy, wait=False)

    @pl.when(b == pl.num_programs(0) - 1)
    def _():
        def make_zero_copy(_, glb, sz):
            return pltpu.make_async_copy(zero_ref.at[pl.ds(0, sz)], hs_ref.at[pl.ds(glb, sz)], sem)

        zero_ref[...] = jnp.zeros_like(zero_ref)
        gaps = pl.num_programs(0)
        _segment_copies(loc_ref, glb_ref, len_ref, gaps, make_zero_copy, wait=False)

        tail = (gaps + 1) * N_EXPERTS
        pieces = len_ref[tail]

        def tail_copy(i):
            row = pl.multiple_of(glb_ref[tail] + i * zero_ref.shape[0], zero_ref.shape[0])
            return pltpu.make_async_copy(zero_ref, hs_ref.at[pl.ds(row, zero_ref.shape[0])], sem)

        @pl.loop(0, pieces)
        def _(i):
            tail_copy(i).start()

        _segment_copies(loc_ref, glb_ref, len_ref, b, make_copy, wait=True)
        _segment_copies(loc_ref, glb_ref, len_ref, gaps, make_zero_copy, wait=True)

        @pl.loop(0, pieces)
        def _(i):
            tail_copy(i).wait()


def _moe_dispatch(h, meta_t, loc, glb, seg_len, rows_total):
    n, d = h.shape
    tb = MOE_BLOCK
    nb = n // tb
    return pl.pallas_call(
        _dispatch_kernel,
        out_shape=jax.ShapeDtypeStruct((rows_total, d), BF16),
        grid_spec=pltpu.PrefetchScalarGridSpec(
            num_scalar_prefetch=3,
            grid=(nb,),
            in_specs=[
                pl.BlockSpec((tb, d), lambda i, *_: (i, 0)),
                pl.BlockSpec((1, 8, tb), lambda i, *_: (i, 0, 0)),
            ],
            out_specs=pl.BlockSpec(memory_space=pl.ANY),
            scratch_shapes=[pltpu.VMEM((MOE_CAP, d), BF16), pltpu.VMEM((SEG_SIZES[0], d), BF16),
                            pltpu.SemaphoreType.DMA],
        ),
        compiler_params=_cparams(("arbitrary",)),
        name="moe_dispatch",
    )(loc, glb, seg_len, h, meta_t)


def _expert_kernel(te_ref, tx_ref, tv_ref, x_ref, wa_ref, wb_ref, wo_ref, o_ref, acc_ref):
    i = pl.program_id(0)
    f = pl.program_id(1)

    last = pl.num_programs(1) - 1
    valid = tv_ref[i] == 1

    def run(emit):
        _swiglu(x_ref[...], lambda sl: wa_ref[0, :, sl], lambda sl: wb_ref[0, :, sl],
                lambda sl: wo_ref[0, :, sl], wo_ref.shape[1], emit)

    @pl.when(jnp.logical_and(valid, f == 0))
    def _():
        def emit(cols, y):
            acc_ref[:, cols] = y
        run(emit)

    if FF_EXPERT // FF_EXPERT_TILE > 2:
        @pl.when(jnp.logical_and(valid, jnp.logical_and(f > 0, f < last)))
        def _():
            def emit(cols, y):
                acc_ref[:, cols] += y
            run(emit)

    @pl.when(jnp.logical_and(valid, f == last))
    def _():
        def emit(cols, y):
            o_ref[:, cols] = (acc_ref[:, cols] + y).astype(o_ref.dtype)
        run(emit)

    @pl.when(jnp.logical_and(tv_ref[i] == 0, f == pl.num_programs(1) - 1))
    def _():
        o_ref[...] = jnp.zeros_like(o_ref)


def _moe_experts(hs, w_in, w_out, tile_expert, tile_row, tile_valid):
    rows, d = hs.shape
    tm = ROW_TILE
    tf = FF_EXPERT_TILE
    nf = FF_EXPERT // tf
    nt = rows // tm

    def f_eff(i, f, tv):
        return jnp.where(tv[i] == 1, f, nf - 1)

    return pl.pallas_call(
        _expert_kernel,
        out_shape=jax.ShapeDtypeStruct((rows, d), BF16),
        grid_spec=pltpu.PrefetchScalarGridSpec(
            num_scalar_prefetch=3,
            grid=(nt, nf),
            in_specs=[
                pl.BlockSpec((tm, d), lambda i, f, te, tx, tv: (tx[i], 0)),
                pl.BlockSpec((1, d, tf), lambda i, f, te, tx, tv: (te[i], 0, f_eff(i, f, tv))),
                pl.BlockSpec((1, d, tf), lambda i, f, te, tx, tv: (te[i], 0, nf + f_eff(i, f, tv))),
                pl.BlockSpec((1, tf, d), lambda i, f, te, tx, tv: (te[i], f_eff(i, f, tv), 0)),
            ],
            out_specs=pl.BlockSpec((tm, d), lambda i, f, te, tx, tv: (i, 0)),
            scratch_shapes=[pltpu.VMEM((tm, d), F32)],
        ),
        compiler_params=_cparams(("arbitrary", "arbitrary")),
        name="moe_experts",
    )(tile_expert, tile_row, tile_valid, hs, w_in, w_in, w_out)


def _combine_kernel(loc_ref, glb_ref, len_ref, x_ref, gate_ref, meta_ref, fg_ref, ys_ref, o_ref, buf_ref, sems,
                    *, final_norm):
    b = pl.program_id(0)
    tb = MOE_BLOCK

    def fetch(blk, slot, wait):
        def make_copy(loc, glb, sz):
            return pltpu.make_async_copy(ys_ref.at[pl.ds(glb, sz)], buf_ref.at[slot, pl.ds(loc, sz)],
                                         sems.at[slot])

        _segment_copies(loc_ref, glb_ref, len_ref, blk, make_copy, wait)

    def start(blk, slot):
        buf_ref[slot] = jnp.zeros(buf_ref.shape[1:], buf_ref.dtype)
        fetch(blk, slot, wait=False)

    @pl.when(b == 0)
    def _():
        start(0, 0)

    @pl.when(b + 1 < pl.num_programs(0))
    def _():
        start(b + 1, (b + 1) % 2)

    slot = b % 2
    meta = meta_ref[...]
    pos1 = meta[:, 0:1]
    pos2 = meta[:, 1:2]
    w1 = meta[:, 2:3]
    w2 = meta[:, 3:4]
    r = lax.broadcasted_iota(jnp.int32, (tb, MOE_CAP), 1).astype(F32)
    sel1 = jnp.where(r == pos1, 1.0, 0.0).astype(BF16)
    sel2 = jnp.where(r == pos2, 1.0, 0.0).astype(BF16)
    fetch(b, slot, wait=True)
    ys = buf_ref[slot]
    y = (w1 * jnp.dot(sel1, ys, preferred_element_type=F32)
         + w2 * jnp.dot(sel2, ys, preferred_element_type=F32))
    x = x_ref[...] + gate_ref[0] * y
    if final_norm:
        x = (x * lax.rsqrt(jnp.mean(x * x, axis=-1, keepdims=True) + EPS)) * fg_ref[...]
    o_ref[...] = x


def _moe_combine(x, gate, meta, ys, loc, glb, seg_len, seq, final_gain=None):
    n, d = x.shape
    tb = MOE_BLOCK
    bsz = gate.shape[0]
    fg = jnp.ones((1, d), F32) if final_gain is None else final_gain.reshape(1, d)
    return pl.pallas_call(
        functools.partial(_combine_kernel, final_norm=final_gain is not None),
        out_shape=jax.ShapeDtypeStruct((n, d), F32),
        grid_spec=pltpu.PrefetchScalarGridSpec(
            num_scalar_prefetch=3,
            grid=(n // tb,),
            in_specs=[
                pl.BlockSpec((tb, d), lambda i, *_: (i, 0)),
                pl.BlockSpec((1, 1, d), lambda i, *_: (i * tb // seq, 0, 0)),
                pl.BlockSpec((tb, LANES), lambda i, *_: (i, 0)),
                pl.BlockSpec((1, d), lambda i, *_: (0, 0)),
                pl.BlockSpec(memory_space=pl.ANY),
            ],
            out_specs=pl.BlockSpec((tb, d), lambda i, *_: (i, 0)),
            scratch_shapes=[pltpu.VMEM((2, MOE_CAP, d), BF16), pltpu.SemaphoreType.DMA((2,))],
        ),
        compiler_params=_cparams(("arbitrary",)),
        name="moe_combine",
    )(loc, glb, seg_len, x, gate.reshape(bsz, 1, d), meta, fg, ys)


def _moe_layout(pc):
    nb = pc.shape[0]
    tm = ROW_TILE
    n_tiles = (nb * (2 * MOE_BLOCK + N_EXPERTS * (MOE_GROUP - 1)) + tm - 1) // tm + N_EXPERTS
    loc = jnp.cumsum(pc, axis=1) - pc
    tiles_e = (jnp.sum(pc, axis=0) + tm - 1) // tm
    tile_end = jnp.cumsum(tiles_e)
    base_e = (tile_end - tiles_e) * tm
    glb = base_e[None, :] + jnp.cumsum(pc, axis=0) - pc
    n_valid = tile_end[-1]
    ids = jnp.arange(n_tiles, dtype=jnp.int32)
    tile_row = jnp.maximum(jnp.minimum(ids, n_valid - 1), 0)
    tile_expert = jnp.sum((tile_row[:, None] >= tile_end[None, :]).astype(jnp.int32), axis=1)
    tile_valid = (ids < n_valid).astype(jnp.int32)
    rows_e = jnp.sum(pc, axis=0)
    first = jnp.zeros_like(rows_e).at[0].set(1)
    loc = jnp.concatenate([loc, jnp.zeros_like(rows_e)[None], jnp.zeros_like(rows_e)[None]])
    glb = jnp.concatenate([glb, (base_e + rows_e)[None], (first * n_valid * tm)[None]])
    seg = jnp.concatenate([pc, (tiles_e * tm - rows_e)[None],
                           (first * (n_tiles - n_valid) * (tm // SEG_SIZES[0]))[None]])
    flat = lambda a: a.reshape(-1).astype(jnp.int32)
    return flat(loc), flat(glb), flat(seg), tile_expert.astype(jnp.int32), tile_row.astype(jnp.int32), tile_valid, n_tiles * tm


def _moe_block(a, w_proj, gate_proj, x, gain, shift, scale, gate, router, w_in, w_out, layer, seq,
               final_gain=None):
    x, h, meta, meta_t, pc = _moe_route(a, w_proj, gate_proj, x, gain, shift, scale, router, seq)
    pc_i = pc[:, 0, :N_EXPERTS].astype(jnp.int32)
    loc, glb, seg_len, tile_expert, tile_row, tile_valid, rows_total = _moe_layout(pc_i)
    hs = _moe_dispatch(h, meta_t, loc, glb, seg_len, rows_total)
    ys = _moe_experts(hs, w_in, w_out, tile_expert + layer * N_EXPERTS, tile_row, tile_valid)
    return _moe_combine(x, gate, meta, ys, loc, glb, seg_len, seq, final_gain)


def _final_norm_kernel(x_ref, g_ref, o_ref):
    x = x_ref[...]
    o_ref[...] = (x * lax.rsqrt(jnp.mean(x * x, axis=-1, keepdims=True) + EPS)) * g_ref[...]


def _final_norm(x, gain):
    n, d = x.shape
    tm = min(ROW_TILE, n)
    return pl.pallas_call(
        _final_norm_kernel,
        out_shape=jax.ShapeDtypeStruct((n, d), F32),
        grid=(n // tm,),
        in_specs=[pl.BlockSpec((tm, d), lambda i: (i, 0)), pl.BlockSpec((1, d), lambda i: (0, 0))],
        out_specs=pl.BlockSpec((tm, d), lambda i: (i, 0)),
        compiler_params=_cparams(("parallel",)),
        name="final_norm",
    )(x, gain.reshape(1, d))


def kernel(x, c, positions, ada_w, ada_b, norm_mix, norm_ff, ret_w_in, ret_gn, ret_w_out, kv_ada_w, kv_ada_b, kv_norm, kv_w, sb_w_q, sb_w_out, ff_w_in, ff_w_out, moe_router, moe_w_in, moe_w_out, final_norm):
    bsz, seq, d = x.shape
    n = bsz * seq
    mod = _ada_mod(c, ada_w, ada_b)
    kv_mod = _ada_mod(c, kv_ada_w[None], kv_ada_b[None])[0]
    rope = _rope_tables(positions)
    ret_tables = _retention_tables()
    moe_in = moe_w_in.astype(BF16).reshape(-1, d, 2 * FF_EXPERT)
    moe_out = moe_w_out.astype(BF16).reshape(-1, FF_EXPERT, d)
    xf = x.reshape(n, d)
    kv = None
    for i in range(DEPTH):
        sh_m, sc_m, g_m, sh_f, sc_f, g_f = [mod[i, :, p * d:(p + 1) * d] for p in range(6)]
        if i < N_A:
            proj = _norm_mod_matmul(xf, norm_mix[i], sh_m, sc_m, ret_w_in[i].astype(BF16), seq, rope=rope)
            o = _retention_core(proj.reshape(bsz, seq, -1), ret_gn[i], ret_tables, bsz, seq)
            o, w_o = o.reshape(n, -1), ret_w_out[i].astype(BF16)
        else:
            j = i - N_A
            q = _norm_mod_matmul(xf, norm_mix[i], sh_m, sc_m, sb_w_q[j].astype(BF16), seq)
            o = _stick_breaking_core(q.reshape(bsz, seq, d), kv, bsz, seq)
            o, w_o = o.reshape(n, d), sb_w_out[j].astype(BF16)
        if i % 2 == 0:
            xf = _dense_ffn(o, w_o, g_m, xf, norm_ff[i], sh_f, sc_f, g_f, ff_w_in[i // 2].astype(BF16),
                            ff_w_out[i // 2].astype(BF16), seq)
        else:
            xf = _moe_block(o, w_o, g_m, xf, norm_ff[i], sh_f, sc_f, g_f, moe_router[i // 2],
                            moe_in, moe_out, i // 2, seq,
                            final_gain=final_norm if i == DEPTH - 1 else None)
        if i == N_A - 1:
            kv = _norm_mod_matmul(xf, kv_norm, kv_mod[:, :d], kv_mod[:, d:], kv_w.astype(BF16), seq)
            kv = kv.reshape(bsz, seq, 2 * d)
    if (DEPTH - 1) % 2 == 0:
        xf = _final_norm(xf, final_norm)
    return xf.reshape(bsz, seq, d)
```

```python
import functools

import jax
import jax.numpy as jnp
from jax import lax
from jax.experimental import pallas as pl
from jax.experimental.pallas import tpu as pltpu

F32 = jnp.float32
BF16 = jnp.bfloat16

D_MODEL = 1024
DEPTH = 4
N_A = DEPTH // 2
CHUNK = 64
RET_HEADS = 8
RET_DK = D_MODEL // RET_HEADS
RET_DV = 2 * D_MODEL // RET_HEADS
ROPE_BASE = 10000.0
SB_HEADS = 16
SB_DH = D_MODEL // SB_HEADS
FF_DENSE = 2816
N_EXPERTS = 8
FF_EXPERT = 3584
EPS = 1e-6

LANES = 128
BF16_ROWS = 16
MXU_COLS = 256
VMEM_LIMIT = 56 * 1024 * 1024

ROW_TILE = 1024
NMM_ROWS = 512
NMM_WIDE = 2048
NMM_CHUNK = 512
DENSE_ROWS = 512
FF_EXPERT_TILE = 1792
RET_BLOCK = 256
SB_TILE = 128
MOE_BLOCK = 512
ROUTE_BLOCKS = 2
ROUTE_STRIP = 256
MOE_GROUP = BF16_ROWS
MOE_CAP = 2 * MOE_BLOCK + N_EXPERTS * MOE_GROUP
SEG_SIZES = (512, 256, 128, 64, 32, 16)
SB_LOG_FLOOR = -110.0


def _cparams(sem):
    return pltpu.CompilerParams(dimension_semantics=sem, vmem_limit_bytes=VMEM_LIMIT)


def _sigmoid(v):
    return 1.0 / (1.0 + jnp.exp(-v))


def _rms_mod(x, gain, shift, scale):
    y = x * lax.rsqrt(jnp.mean(x * x, axis=-1, keepdims=True) + EPS)
    return (y * gain) * (1.0 + scale) + shift


def _ada_kernel(c_ref, w_ref, b_ref, o_ref):
    c = c_ref[...]
    cond = (c * _sigmoid(c)).astype(BF16)
    o_ref[0] = jnp.dot(cond, w_ref[0].astype(BF16), preferred_element_type=F32) + b_ref[0]


def _ada_mod(c, w, b):
    nl, d, n = w.shape
    bsz = c.shape[0]
    tn = 1024
    return pl.pallas_call(
        _ada_kernel,
        out_shape=jax.ShapeDtypeStruct((nl, bsz, n), F32),
        grid=(nl, n // tn),
        in_specs=[
            pl.BlockSpec((bsz, d), lambda l, j: (0, 0)),
            pl.BlockSpec((1, d, tn), lambda l, j: (l, 0, j)),
            pl.BlockSpec((1, 1, tn), lambda l, j: (l, 0, j)),
        ],
        out_specs=pl.BlockSpec((1, bsz, tn), lambda l, j: (l, 0, j)),
        compiler_params=_cparams(("parallel", "parallel")),
        name="ada_mod",
    )(c, w, b.reshape(nl, 1, n))


def _rope_table_kernel(pos_ref, freq_ref, cs_ref, sn_ref):
    ang = pos_ref[...] * freq_ref[...]
    lane = lax.broadcasted_iota(jnp.int32, ang.shape, 1)
    cs_ref[...] = jnp.cos(ang)
    sn_ref[...] = jnp.where(lane < RET_DK // 2, -jnp.sin(ang), jnp.sin(ang))


def _rope_tables(positions):
    n = positions.size
    inv_freq = jnp.power(ROPE_BASE, -jnp.arange(0, RET_DK, 2, dtype=F32) / RET_DK)
    freq = jnp.concatenate([inv_freq, inv_freq]).reshape(1, RET_DK)
    pos = jnp.broadcast_to(positions.astype(F32).reshape(n, 1), (n, RET_DK))
    tr = min(n, 2048)
    return pl.pallas_call(
        _rope_table_kernel,
        out_shape=(jax.ShapeDtypeStruct((n, RET_DK), F32),) * 2,
        grid=(n // tr,),
        in_specs=[pl.BlockSpec((tr, RET_DK), lambda i: (i, 0)),
                  pl.BlockSpec((1, RET_DK), lambda i: (0, 0))],
        out_specs=(pl.BlockSpec((tr, RET_DK), lambda i: (i, 0)),) * 2,
        compiler_params=_cparams(("parallel",)),
        name="rope_tables",
    )(pos, freq)


def _nmm_kernel(x_ref, g_ref, sh_ref, sc_ref, w_ref, *rest, rope_cols):
    if rope_cols:
        cs_ref, sn_ref, o_ref = rest
    else:
        (o_ref,) = rest
    cw = NMM_CHUNK
    n_chunks = w_ref.shape[1] // cw
    h = _rms_mod(x_ref[...], g_ref[...], sh_ref[0], sc_ref[0]).astype(BF16)

    def mm(c):
        return jnp.dot(h, w_ref[:, c * cw:(c + 1) * cw], preferred_element_type=F32)

    nxt = mm(0)
    for c in range(n_chunks):
        r = nxt
        if c + 1 < n_chunks:
            nxt = mm(c + 1)
        col0 = c * cw
        if col0 >= rope_cols:
            o_ref[:, col0:col0 + cw] = r.astype(o_ref.dtype)
            continue
        k_scale = RET_DK ** -0.5 if col0 >= rope_cols // 2 else None
        for s in range(cw // RET_DK):
            t = r[:, s * RET_DK:(s + 1) * RET_DK]
            rot = t * cs_ref[...] + pltpu.roll(t, RET_DK // 2, 1) * sn_ref[...]
            if k_scale is not None:
                rot = rot * k_scale
            o_ref[:, col0 + s * RET_DK:col0 + (s + 1) * RET_DK] = rot.astype(o_ref.dtype)


def _norm_mod_matmul(x, gain, shift, scale, w, seq, rope=None):
    n, d = x.shape
    m = w.shape[1]
    tm = min(NMM_ROWS if m > NMM_WIDE else ROW_TILE, seq)
    bsz = shift.shape[0]
    vec = lambda i: (i * tm // seq, 0, 0)
    in_specs = [
        pl.BlockSpec((tm, d), lambda i: (i, 0)),
        pl.BlockSpec((1, d), lambda i: (0, 0)),
        pl.BlockSpec((1, 1, d), vec),
        pl.BlockSpec((1, 1, d), vec),
        pl.BlockSpec((d, m), lambda i: (0, 0), pipeline_mode=pl.Buffered(1)),
    ]
    args = [x, gain.reshape(1, d), shift.reshape(bsz, 1, d), scale.reshape(bsz, 1, d), w]
    rope_cols = 0
    if rope is not None:
        rope_cols = 2 * RET_HEADS * RET_DK
        in_specs += [pl.BlockSpec((tm, RET_DK), lambda i: (i, 0))] * 2
        args += list(rope)
    return pl.pallas_call(
        functools.partial(_nmm_kernel, rope_cols=rope_cols),
        out_shape=jax.ShapeDtypeStruct((n, m), BF16),
        grid=(n // tm,),
        in_specs=in_specs,
        out_specs=pl.BlockSpec((tm, m), lambda i: (i, 0)),
        compiler_params=_cparams(("parallel",)),
        name="norm_mod_matmul_rope" if rope_cols else "norm_mod_matmul",
    )(*args)


def _swiglu(h, load_wa, load_wb, load_wo, width, emit):
    def up(c):
        sl = slice(c * MXU_COLS, (c + 1) * MXU_COLS)
        return (jnp.dot(h, load_wa(sl), preferred_element_type=F32),
                jnp.dot(h, load_wb(sl), preferred_element_type=F32))

    n_chunks = width // MXU_COLS
    acts = []
    nxt = up(0)
    for c in range(n_chunks):
        a, b = nxt
        if c + 1 < n_chunks:
            nxt = up(c + 1)
        acts.append((a * _sigmoid(a) * b).astype(BF16))
    act = jnp.concatenate(acts, axis=1)

    def down(c):
        return jnp.dot(act, load_wo(slice(c * MXU_COLS, (c + 1) * MXU_COLS)), preferred_element_type=F32)

    n_out = D_MODEL // MXU_COLS
    nxt = down(0)
    for c in range(n_out):
        y = nxt
        if c + 1 < n_out:
            nxt = down(c + 1)
        emit(slice(c * MXU_COLS, (c + 1) * MXU_COLS), y)


def _ffn_kernel(a_ref, wp_ref, gp_ref, x_ref, g_ref, sh_ref, sc_ref, gate_ref, wa_ref, wb_ref, wo_ref, o_ref,
                x1_ref):
    x1_ref[...] = x_ref[...] + gp_ref[0] * jnp.dot(a_ref[...], wp_ref[...], preferred_element_type=F32)
    h = _rms_mod(x1_ref[...], g_ref[...], sh_ref[0], sc_ref[0]).astype(BF16)

    def emit(cols, y):
        o_ref[:, cols] = x1_ref[:, cols] + gate_ref[0, :, cols] * y

    _swiglu(h, lambda sl: wa_ref[:, sl], lambda sl: wb_ref[:, sl], lambda sl: wo_ref[:, sl],
            wo_ref.shape[0], emit)


def _dense_ffn(a, w_proj, gate_proj, x, gain, shift, scale, gate, w_in, w_out, seq):
    n, d = x.shape
    k = a.shape[1]
    ff = w_out.shape[0]
    tm = min(DENSE_ROWS, seq)
    bsz = shift.shape[0]
    vec = lambda i: (i * tm // seq, 0, 0)
    resident = pl.Buffered(1)
    return pl.pallas_call(
        _ffn_kernel,
        out_shape=jax.ShapeDtypeStruct((n, d), F32),
        grid=(n // tm,),
        in_specs=[
            pl.BlockSpec((tm, k), lambda i: (i, 0)),
            pl.BlockSpec((k, d), lambda i: (0, 0), pipeline_mode=resident),
            pl.BlockSpec((1, 1, d), vec),
            pl.BlockSpec((tm, d), lambda i: (i, 0)),
            pl.BlockSpec((1, d), lambda i: (0, 0)),
            pl.BlockSpec((1, 1, d), vec),
            pl.BlockSpec((1, 1, d), vec),
            pl.BlockSpec((1, 1, d), vec),
            pl.BlockSpec((d, ff), lambda i: (0, 0), pipeline_mode=resident),
            pl.BlockSpec((d, ff), lambda i: (0, 1), pipeline_mode=resident),
            pl.BlockSpec((ff, d), lambda i: (0, 0), pipeline_mode=resident),
        ],
        out_specs=pl.BlockSpec((tm, d), lambda i: (i, 0)),
        scratch_shapes=[pltpu.VMEM((tm, d), F32)],
        compiler_params=_cparams(("parallel",)),
        name="dense_ffn",
    )(a, w_proj, gate_proj.reshape(bsz, 1, d), x, gain.reshape(1, d), shift.reshape(bsz, 1, d),
      scale.reshape(bsz, 1, d), gate.reshape(bsz, 1, d), w_in, w_in, w_out)


def _ret_kernel(q_ref, k_ref, v_ref, g_ref, gn_ref, dm_ref, qd_ref, kd_ref, cd_ref, o_ref, *, nblk):
    dmat = dm_ref[0]
    qdec = qd_ref[0]
    kdec = kd_ref[0]
    cdec = cd_ref[0, 0:1, :]
    gn = gn_ref[...]

    def load(t):
        sl = slice(t * RET_BLOCK, (t + 1) * RET_BLOCK)
        return q_ref[0, sl, :], k_ref[0, sl, :], v_ref[0, sl, :]

    def scores(q, k):
        return lax.dot_general(q, k, (((1,), (1,)), ((), ())), preferred_element_type=F32)

    state = jnp.zeros((RET_DK, RET_DV), F32)
    q, k, v = load(0)
    s = scores(q, k)
    for t in range(nblk):
        qd = (q.astype(F32) * qdec).astype(BF16)
        cross = jnp.dot(qd, state.astype(BF16), preferred_element_type=F32)
        kdt = jnp.transpose(k.astype(F32) * kdec).astype(BF16)
        update = jnp.dot(kdt, v, preferred_element_type=F32)
        if t + 1 < nblk:
            q, k, v_next = load(t + 1)
            s_next = scores(q, k)
        intra = jnp.dot((s * dmat).astype(BF16), v, preferred_element_type=F32)
        state = state * cdec + update
        o = intra + cross
        mu = jnp.mean(o, axis=-1, keepdims=True)
        oc = o - mu
        var = jnp.mean(oc * oc, axis=-1, keepdims=True)
        on = (oc * lax.rsqrt(var + EPS)) * gn
        sl = slice(t * RET_BLOCK, (t + 1) * RET_BLOCK)
        g = g_ref[0, sl, :].astype(F32)
        o_ref[0, sl, :] = ((g * _sigmoid(g)) * on).astype(o_ref.dtype)
        if t + 1 < nblk:
            v, s = v_next, s_next


def _retention_tables():
    t = RET_BLOCK
    log_gamma = jnp.log(1.0 - jnp.exp2(-5.0 - jnp.arange(RET_HEADS, dtype=F32)))
    pos = jnp.arange(t, dtype=F32)
    chunk = jnp.arange(t) // CHUNK
    dist = jnp.abs(pos[:, None] - pos[None, :])
    visible = (chunk[None, :] <= chunk[:, None]).astype(F32)
    dmat = jnp.exp(dist[None] * log_gamma[:, None, None]) * visible[None]
    qdec = jnp.exp((pos[None, :] + 1.0) * log_gamma[:, None])
    kdec = jnp.exp((t - 1.0 - pos)[None, :] * log_gamma[:, None])
    cdec = jnp.exp(t * log_gamma)
    qdec = jnp.broadcast_to(qdec[:, :, None], (RET_HEADS, t, RET_DK))
    kdec = jnp.broadcast_to(kdec[:, :, None], (RET_HEADS, t, RET_DK))
    cdec = jnp.broadcast_to(cdec[:, None, None], (RET_HEADS, 8, RET_DV))
    return dmat, qdec, kdec, cdec


def _retention_core(proj, gn_w, tables, bsz, seq):
    dmat, qdec, kdec, cdec = tables
    h = RET_HEADS
    t = RET_BLOCK
    kq = h * RET_DK // RET_DK
    kv = 2 * h * RET_DK // RET_DV
    return pl.pallas_call(
        functools.partial(_ret_kernel, nblk=seq // t),
        out_shape=jax.ShapeDtypeStruct((bsz, seq, h * RET_DV), BF16),
        grid=(bsz, h),
        in_specs=[
            pl.BlockSpec((1, seq, RET_DK), lambda b, i: (b, 0, i)),
            pl.BlockSpec((1, seq, RET_DK), lambda b, i: (b, 0, kq + i)),
            pl.BlockSpec((1, seq, RET_DV), lambda b, i: (b, 0, kv + i)),
            pl.BlockSpec((1, seq, RET_DV), lambda b, i: (b, 0, kv + h + i)),
            pl.BlockSpec((1, RET_DV), lambda b, i: (0, i)),
            pl.BlockSpec((1, t, t), lambda b, i: (i, 0, 0)),
            pl.BlockSpec((1, t, RET_DK), lambda b, i: (i, 0, 0)),
            pl.BlockSpec((1, t, RET_DK), lambda b, i: (i, 0, 0)),
            pl.BlockSpec((1, 8, RET_DV), lambda b, i: (i, 0, 0)),
        ],
        out_specs=pl.BlockSpec((1, seq, RET_DV), lambda b, i: (b, 0, i)),
        compiler_params=_cparams(("parallel", "parallel")),
        name="retention_core",
    )(proj, proj, proj, proj, gn_w.reshape(1, h * RET_DV), dmat, qdec, kdec, cdec)


def _sb_kernel(q_ref, k_ref, v_ref, o_ref, r_ref, acc_ref):
    t = SB_TILE
    w = 2 * SB_DH
    pairs = SB_HEADS // 2
    qi = pl.program_id(1)
    lane = lax.broadcasted_iota(jnp.int32, (t, w), 1)
    row = lax.broadcasted_iota(jnp.int32, (t, 2 * t), 0)
    col = lax.broadcasted_iota(jnp.int32, (t, 2 * t), 1)
    diag_causal = jnp.where(col >= t, col - t, col) < row
    uj = lax.broadcasted_iota(jnp.int32, (2 * t, 2 * t), 0)
    us = lax.broadcasted_iota(jnp.int32, (2 * t, 2 * t), 1)
    uj = jnp.where(uj >= t, uj - t, uj)
    umat = jnp.where((uj >= us) | (us >= t), -1.0, 0.0).astype(BF16)
    scale = jnp.asarray(SB_DH ** -0.5, BF16)
    qs = [q_ref[0, :, p * w:(p + 1) * w] * scale for p in range(pairs)]

    def split_heads(x):
        zero = jnp.zeros_like(x)
        return jnp.concatenate([jnp.where(lane < SB_DH, x, zero), jnp.where(lane >= SB_DH, x, zero)], axis=0)

    def key_tiles(kbs, causal, first, ps):
        spans = [pl.ds(pl.multiple_of(kb * t, t), t) for kb in kbs]
        tiles = [(j, p) for j in range(len(kbs)) for p in ps]
        zs, his, los, incls, tots = {}, {}, {}, {}, {}
        for j, p in tiles:
            kcat = split_heads(k_ref[0, spans[j], p * w:(p + 1) * w])
            zs[j, p] = lax.dot_general(qs[p], kcat, (((1,), (1,)), ((), ())), preferred_element_type=F32)
        for j, p in tiles:
            z = zs[j, p]
            sp = jnp.maximum(z, 0.0) + jnp.log(1.0 + jnp.exp(-jnp.abs(z)))
            if causal is not None and j == 0:
                sp = jnp.where(causal, sp, 0.0)
            hi = sp.astype(BF16)
            his[j, p] = hi
            los[j, p] = (sp - hi.astype(F32)).astype(BF16)
        for j, p in tiles:
            cs = [jnp.dot(jnp.concatenate([his[j, p][:, h * t:(h + 1) * t], los[j, p][:, h * t:(h + 1) * t]],
                                          axis=1), umat, preferred_element_type=F32) for h in range(2)]
            incls[j, p] = jnp.concatenate([cs[0][:, :t], cs[1][:, :t]], axis=1)
            tots[j, p] = jnp.concatenate([cs[0][:, t:], cs[1][:, t:]], axis=1)
        avs = {}
        rs = {p: None if first else r_ref[p] for p in ps}
        for j, p in tiles:
            e = zs[j, p] + incls[j, p]
            a = jnp.exp(e if rs[p] is None else e + rs[p])
            if causal is not None and j == 0:
                a = jnp.where(causal, a, 0.0)
            avs[j, p] = a.astype(BF16)
            rs[p] = tots[j, p] if rs[p] is None else rs[p] + tots[j, p]
        for p in ps:
            r_ref[p] = rs[p]
        accs = {p: None if first else acc_ref[p] for p in ps}
        for j, p in tiles:
            vcat = split_heads(v_ref[0, spans[j], p * w:(p + 1) * w])
            av = jnp.dot(avs[j, p], vcat, preferred_element_type=F32)
            accs[p] = av if accs[p] is None else accs[p] + av
        for p in ps:
            acc_ref[p] = accs[p]
        return tuple((jnp.max(rs[p]) > SB_LOG_FLOOR).astype(jnp.int32) for p in ps)

    everyone = list(range(pairs))
    left = key_tiles([qi], diag_causal, True, everyone)
    kb = qi - 1

    two = jnp.logical_and(kb >= 1, sum(left) > 0)
    left = lax.cond(two, lambda: key_tiles([kb, kb - 1], None, False, everyone), lambda: left)
    kb = jnp.where(two, kb - 2, kb)

    def cond(carry):
        kb, *left = carry
        return jnp.logical_and(kb >= 0, sum(left) > 0)

    def body(carry):
        kb, *left = carry

        def stragglers():
            return tuple(lax.cond(left[p] > 0, lambda p=p: key_tiles([kb], None, False, [p])[0],
                                  lambda: jnp.int32(0)) for p in everyone)

        left = lax.cond(sum(left) > pairs // 2, lambda: key_tiles([kb], None, False, everyone), stragglers)
        return (kb - 1, *left)

    lax.while_loop(cond, body, (kb, *left))
    for p in range(pairs):
        o_ref[0, :, p * w:(p + 1) * w] = acc_ref[p].astype(o_ref.dtype)


def _stick_breaking_core(q, kv, bsz, seq):
    t = SB_TILE
    return pl.pallas_call(
        _sb_kernel,
        out_shape=jax.ShapeDtypeStruct((bsz, seq, D_MODEL), BF16),
        grid=(bsz, seq // t),
        in_specs=[
            pl.BlockSpec((1, t, D_MODEL), lambda b, i: (b, i, 0)),
            pl.BlockSpec((1, seq, D_MODEL), lambda b, i: (b, 0, 0)),
            pl.BlockSpec((1, seq, D_MODEL), lambda b, i: (b, 0, 1)),
        ],
        out_specs=pl.BlockSpec((1, t, D_MODEL), lambda b, i: (b, i, 0)),
        scratch_shapes=[pltpu.VMEM((SB_HEADS // 2, t, 2 * t), F32),
                        pltpu.VMEM((SB_HEADS // 2, t, 2 * SB_DH), F32)],
        compiler_params=_cparams(("parallel", "parallel")),
        name="stick_breaking_core",
    )(q, kv, kv)


def _route_kernel(a_ref, wp_ref, gp_ref, x_ref, g_ref, sh_ref, sc_ref, r_ref,
                  x1_ref, h_ref, meta_ref, meta_t_ref, pc_ref):
    tb = MOE_BLOCK
    rows = x_ref.shape[0]
    strips = [slice(s, s + ROUTE_STRIP) for s in range(0, rows, ROUTE_STRIP)]
    ys = [jnp.dot(a_ref[rs, :], wp_ref[...], preferred_element_type=F32) for rs in strips]
    parts = []
    for rs, y in zip(strips, ys):
        x1 = x_ref[rs, :] + gp_ref[0] * y
        x1_ref[rs, :] = x1
        h32 = _rms_mod(x1, g_ref[...], sh_ref[0], sc_ref[0])
        h1 = h32.astype(BF16)
        h_ref[rs, :] = h1
        e1 = h32 - h1.astype(F32)
        h2 = e1.astype(BF16)
        parts.append((h1, h2, (e1 - h2.astype(F32)).astype(BF16)))
    r3 = r_ref[...]
    s3 = jnp.concatenate([jnp.dot(h1, r3, preferred_element_type=F32) + jnp.dot(h2, r3, preferred_element_type=F32)
                          + jnp.dot(h3, r3, preferred_element_type=F32) for h1, h2, h3 in parts], axis=0)
    logits = s3 + pltpu.roll(s3, LANES - N_EXPERTS, 1) + pltpu.roll(s3, LANES - 2 * N_EXPERTS, 1)
    lane = lax.broadcasted_iota(jnp.int32, (rows, LANES), 1)
    lane_f = lane.astype(F32)
    neg = jnp.float32(-jnp.inf)
    lg = jnp.where(lane < N_EXPERTS, logits, neg)
    m1 = jnp.max(lg, axis=-1, keepdims=True)
    i1 = jnp.min(jnp.where(lg == m1, lane_f, float(LANES)), axis=-1, keepdims=True)
    oh1 = lane_f == i1
    lg2 = jnp.where(oh1, neg, lg)
    m2 = jnp.max(lg2, axis=-1, keepdims=True)
    i2 = jnp.min(jnp.where(lg2 == m2, lane_f, float(LANES)), axis=-1, keepdims=True)
    oh2 = lane_f == i2
    e = jnp.exp(m2 - m1)
    w1 = 1.0 / (1.0 + e)
    w2 = e / (1.0 + e)

    tr = lax.broadcasted_iota(jnp.int32, (tb, tb), 0)
    tc = lax.broadcasted_iota(jnp.int32, (tb, tb), 1)
    lower = jnp.where(tc < tr, 1.0, 0.0).astype(BF16)
    pj = lax.broadcasted_iota(jnp.int32, (LANES, LANES), 0)
    ps = lax.broadcasted_iota(jnp.int32, (LANES, LANES), 1)
    before = jnp.where(pj < ps, 1.0, 0.0)
    oh1f = jnp.where(oh1, 1.0, 0.0)
    oh2f = jnp.where(oh2, 1.0, 0.0)
    blocks = [slice(b * tb, (b + 1) * tb) for b in range(rows // tb)]
    cum1 = [jnp.dot(lower, oh1f[rs].astype(BF16), preferred_element_type=F32) for rs in blocks]
    cum2 = [jnp.dot(lower, oh2f[rs].astype(BF16), preferred_element_type=F32) for rs in blocks]
    cnt1 = [jnp.sum(oh1f[rs], axis=0, keepdims=True) for rs in blocks]
    cnt2 = [jnp.sum(oh2f[rs], axis=0, keepdims=True) for rs in blocks]
    pc = [jnp.floor((c1 + c2 + (MOE_GROUP - 1.0)) / MOE_GROUP) * MOE_GROUP
          for c1, c2 in zip(cnt1, cnt2)]
    off = [jnp.dot(jnp.broadcast_to(c, (8, LANES)), before, preferred_element_type=F32,
                   precision=lax.Precision.HIGHEST)[0:1] for c in pc]
    pos1 = jnp.concatenate([jnp.sum(oh1f[rs] * (off[b] + cum1[b]), axis=-1, keepdims=True)
                            for b, rs in enumerate(blocks)], axis=0)
    pos2 = jnp.concatenate([jnp.sum(oh2f[rs] * (off[b] + cnt1[b] + cum2[b]), axis=-1, keepdims=True)
                            for b, rs in enumerate(blocks)], axis=0)
    meta = jnp.where(lane == 0, pos1,
                     jnp.where(lane == 1, pos2,
                               jnp.where(lane == 2, w1, jnp.where(lane == 3, w2, 0.0))))
    meta_ref[...] = meta
    for b, rs in enumerate(blocks):
        meta_t_ref[b] = jnp.transpose(meta[rs])[0:8]
        pc_ref[b] = jnp.broadcast_to(pc[b], (8, LANES))


def _moe_route(a, w_proj, gate_proj, x, gain, shift, scale, router, seq):
    n, d = x.shape
    k = a.shape[1]
    tb = MOE_BLOCK
    nb = n // tb
    rb = min(ROUTE_BLOCKS, seq // tb)
    rows = rb * tb
    bsz = shift.shape[0]
    r1 = router.astype(BF16)
    r2 = (router - r1.astype(F32)).astype(BF16)
    r3 = (router - r1.astype(F32) - r2.astype(F32)).astype(BF16)
    r_pad = jnp.zeros((d, LANES), BF16).at[:, :3 * N_EXPERTS].set(jnp.concatenate([r1, r2, r3], axis=1))
    vec = lambda i: (i * rows // seq, 0, 0)
    return pl.pallas_call(
        _route_kernel,
        out_shape=(
            jax.ShapeDtypeStruct((n, d), F32),
            jax.ShapeDtypeStruct((n, d), BF16),
            jax.ShapeDtypeStruct((n, LANES), F32),
            jax.ShapeDtypeStruct((nb, 8, tb), F32),
            jax.ShapeDtypeStruct((nb, 8, LANES), F32),
        ),
        grid=(nb // rb,),
        in_specs=[
            pl.BlockSpec((rows, k), lambda i: (i, 0)),
            pl.BlockSpec((k, d), lambda i: (0, 0), pipeline_mode=pl.Buffered(1)),
            pl.BlockSpec((1, 1, d), vec),
            pl.BlockSpec((rows, d), lambda i: (i, 0)),
            pl.BlockSpec((1, d), lambda i: (0, 0)),
            pl.BlockSpec((1, 1, d), vec),
            pl.BlockSpec((1, 1, d), vec),
            pl.BlockSpec((d, LANES), lambda i: (0, 0)),
        ],
        out_specs=(
            pl.BlockSpec((rows, d), lambda i: (i, 0)),
            pl.BlockSpec((rows, d), lambda i: (i, 0)),
            pl.BlockSpec((rows, LANES), lambda i: (i, 0)),
            pl.BlockSpec((rb, 8, tb), lambda i: (i, 0, 0)),
            pl.BlockSpec((rb, 8, LANES), lambda i: (i, 0, 0)),
        ),
        compiler_params=_cparams(("parallel",)),
        name="moe_route",
    )(a, w_proj, gate_proj.reshape(bsz, 1, d), x, gain.reshape(1, d), shift.reshape(bsz, 1, d),
      scale.reshape(bsz, 1, d), r_pad)


def _segment_copies(loc_ref, glb_ref, len_ref, blk, make_copy, wait):
    for e in range(N_EXPERTS):
        idx = blk * N_EXPERTS + e
        loc = loc_ref[idx]
        glb = glb_ref[idx]
        length = len_ref[idx]
        done = jnp.int32(0)
        for sz in SEG_SIZES:
            take = (length & sz) != 0

            @pl.when(take)
            def _(loc=loc, glb=glb, done=done, sz=sz):
                cp = make_copy(pl.multiple_of(loc + done, MOE_GROUP),
                               pl.multiple_of(glb + done, MOE_GROUP), sz)
                if wait:
                    cp.wait()
                else:
                    cp.start()

            done = done + jnp.where(take, sz, 0)


def _dispatch_kernel(loc_ref, glb_ref, len_ref, h_ref, meta_t_ref, hs_ref, buf_ref, zero_ref, sem):
    b = pl.program_id(0)
    tb = MOE_BLOCK
    pos1 = meta_t_ref[0, 0:1, :]
    pos2 = meta_t_ref[0, 1:2, :]
    r = lax.broadcasted_iota(jnp.int32, (MOE_CAP, tb), 0).astype(F32)
    perm = jnp.where((r == pos1) | (r == pos2), 1.0, 0.0).astype(BF16)
    sorted_rows = jnp.dot(perm, h_ref[...], preferred_element_type=F32).astype(BF16)

    def make_copy(loc, glb, sz):
        return pltpu.make_async_copy(buf_ref.at[pl.ds(loc, sz)], hs_ref.at[pl.ds(glb, sz)], sem)

    @pl.when(b > 0)
    def _():
        _segment_copies(loc_ref, glb_ref, len_ref, b - 1, make_copy, wait=True)

    buf_ref[...] = sorted_rows
    _segment_copies(loc_ref, glb_ref, len_ref, b, make_copy, wait=False)

    @pl.when(b == pl.num_programs(0) - 1)
    def _():
        def make_zero_copy(_, glb, sz):
            return pltpu.make_async_copy(zero_ref.at[pl.ds(0, sz)], hs_ref.at[pl.ds(glb, sz)], sem)

        zero_ref[...] = jnp.zeros_like(zero_ref)
        gaps = pl.num_programs(0)
        _segment_copies(loc_ref, glb_ref, len_ref, gaps, make_zero_copy, wait=False)

        tail = (gaps + 1) * N_EXPERTS
        pieces = len_ref[tail]

        def tail_copy(i):
            row = pl.multiple_of(glb_ref[tail] + i * zero_ref.shape[0], zero_ref.shape[0])
            return pltpu.make_async_copy(zero_ref, hs_ref.at[pl.ds(row, zero_ref.shape[0])], sem)

        @pl.loop(0, pieces)
        def _(i):
            tail_copy(i).start()

        _segment_copies(loc_ref, glb_ref, len_ref, b, make_copy, wait=True)
        _segment_copies(loc_ref, glb_ref, len_ref, gaps, make_zero_copy, wait=True)

        @pl.loop(0, pieces)
        def _(i):
            tail_copy(i).wait()


def _moe_dispatch(h, meta_t, loc, glb, seg_len, rows_total):
    n, d = h.shape
    tb = MOE_BLOCK
    nb = n // tb
    return pl.pallas_call(
        _dispatch_kernel,
        out_shape=jax.ShapeDtypeStruct((rows_total, d), BF16),
        grid_spec=pltpu.PrefetchScalarGridSpec(
            num_scalar_prefetch=3,
            grid=(nb,),
            in_specs=[
                pl.BlockSpec((tb, d), lambda i, *_: (i, 0)),
                pl.BlockSpec((1, 8, tb), lambda i, *_: (i, 0, 0)),
            ],
            out_specs=pl.BlockSpec(memory_space=pl.ANY),
            scratch_shapes=[pltpu.VMEM((MOE_CAP, d), BF16), pltpu.VMEM((SEG_SIZES[0], d), BF16),
                            pltpu.SemaphoreType.DMA],
        ),
        compiler_params=_cparams(("arbitrary",)),
        name="moe_dispatch",
    )(loc, glb, seg_len, h, meta_t)


def _expert_kernel(te_ref, tx_ref, tv_ref, x_ref, wa_ref, wb_ref, wo_ref, o_ref, acc_ref):
    i = pl.program_id(0)
    f = pl.program_id(1)

    last = pl.num_programs(1) - 1
    valid = tv_ref[i] == 1

    def run(emit):
        _swiglu(x_ref[...], lambda sl: wa_ref[0, :, sl], lambda sl: wb_ref[0, :, sl],
                lambda sl: wo_ref[0, :, sl], wo_ref.shape[1], emit)

    @pl.when(jnp.logical_and(valid, f == 0))
    def _():
        def emit(cols, y):
            acc_ref[:, cols] = y
        run(emit)

    if FF_EXPERT // FF_EXPERT_TILE > 2:
        @pl.when(jnp.logical_and(valid, jnp.logical_and(f > 0, f < last)))
        def _():
            def emit(cols, y):
                acc_ref[:, cols] += y
            run(emit)

    @pl.when(jnp.logical_and(valid, f == last))
    def _():
        def emit(cols, y):
            o_ref[:, cols] = (acc_ref[:, cols] + y).astype(o_ref.dtype)
        run(emit)

    @pl.when(jnp.logical_and(tv_ref[i] == 0, f == pl.num_programs(1) - 1))
    def _():
        o_ref[...] = jnp.zeros_like(o_ref)


def _moe_experts(hs, w_in, w_out, tile_expert, tile_row, tile_valid):
    rows, d = hs.shape
    tm = ROW_TILE
    tf = FF_EXPERT_TILE
    nf = FF_EXPERT // tf
    nt = rows // tm

    def f_eff(i, f, tv):
        return jnp.where(tv[i] == 1, f, nf - 1)

    return pl.pallas_call(
        _expert_kernel,
        out_shape=jax.ShapeDtypeStruct((rows, d), BF16),
        grid_spec=pltpu.PrefetchScalarGridSpec(
            num_scalar_prefetch=3,
            grid=(nt, nf),
            in_specs=[
                pl.BlockSpec((tm, d), lambda i, f, te, tx, tv: (tx[i], 0)),
                pl.BlockSpec((1, d, tf), lambda i, f, te, tx, tv: (te[i], 0, f_eff(i, f, tv))),
                pl.BlockSpec((1, d, tf), lambda i, f, te, tx, tv: (te[i], 0, nf + f_eff(i, f, tv))),
                pl.BlockSpec((1, tf, d), lambda i, f, te, tx, tv: (te[i], f_eff(i, f, tv), 0)),
            ],
            out_specs=pl.BlockSpec((tm, d), lambda i, f, te, tx, tv: (i, 0)),
            scratch_shapes=[pltpu.VMEM((tm, d), F32)],
        ),
        compiler_params=_cparams(("arbitrary", "arbitrary")),
        name="moe_experts",
    )(tile_expert, tile_row, tile_valid, hs, w_in, w_in, w_out)


def _combine_kernel(loc_ref, glb_ref, len_ref, x_ref, gate_ref, meta_ref, fg_ref, ys_ref, o_ref, buf_ref, sems,
                    *, final_norm):
    b = pl.program_id(0)
    tb = MOE_BLOCK

    def fetch(blk, slot, wait):
        def make_copy(loc, glb, sz):
            return pltpu.make_async_copy(ys_ref.at[pl.ds(glb, sz)], buf_ref.at[slot, pl.ds(loc, sz)],
                                         sems.at[slot])

        _segment_copies(loc_ref, glb_ref, len_ref, blk, make_copy, wait)

    def start(blk, slot):
        buf_ref[slot] = jnp.zeros(buf_ref.shape[1:], buf_ref.dtype)
        fetch(blk, slot, wait=False)

    @pl.when(b == 0)
    def _():
        start(0, 0)

    @pl.when(b + 1 < pl.num_programs(0))
    def _():
        start(b + 1, (b + 1) % 2)

    slot = b % 2
    meta = meta_ref[...]
    pos1 = meta[:, 0:1]
    pos2 = meta[:, 1:2]
    w1 = meta[:, 2:3]
    w2 = meta[:, 3:4]
    r = lax.broadcasted_iota(jnp.int32, (tb, MOE_CAP), 1).astype(F32)
    sel1 = jnp.where(r == pos1, 1.0, 0.0).astype(BF16)
    sel2 = jnp.where(r == pos2, 1.0, 0.0).astype(BF16)
    fetch(b, slot, wait=True)
    ys = buf_ref[slot]
    y = (w1 * jnp.dot(sel1, ys, preferred_element_type=F32)
         + w2 * jnp.dot(sel2, ys, preferred_element_type=F32))
    x = x_ref[...] + gate_ref[0] * y
    if final_norm:
        x = (x * lax.rsqrt(jnp.mean(x * x, axis=-1, keepdims=True) + EPS)) * fg_ref[...]
    o_ref[...] = x


def _moe_combine(x, gate, meta, ys, loc, glb, seg_len, seq, final_gain=None):
    n, d = x.shape
    tb = MOE_BLOCK
    bsz = gate.shape[0]
    fg = jnp.ones((1, d), F32) if final_gain is None else final_gain.reshape(1, d)
    return pl.pallas_call(
        functools.partial(_combine_kernel, final_norm=final_gain is not None),
        out_shape=jax.ShapeDtypeStruct((n, d), F32),
        grid_spec=pltpu.PrefetchScalarGridSpec(
            num_scalar_prefetch=3,
            grid=(n // tb,),
            in_specs=[
                pl.BlockSpec((tb, d), lambda i, *_: (i, 0)),
                pl.BlockSpec((1, 1, d), lambda i, *_: (i * tb // seq, 0, 0)),
                pl.BlockSpec((tb, LANES), lambda i, *_: (i, 0)),
                pl.BlockSpec((1, d), lambda i, *_: (0, 0)),
                pl.BlockSpec(memory_space=pl.ANY),
            ],
            out_specs=pl.BlockSpec((tb, d), lambda i, *_: (i, 0)),
            scratch_shapes=[pltpu.VMEM((2, MOE_CAP, d), BF16), pltpu.SemaphoreType.DMA((2,))],
        ),
        compiler_params=_cparams(("arbitrary",)),
        name="moe_combine",
    )(loc, glb, seg_len, x, gate.reshape(bsz, 1, d), meta, fg, ys)


def _moe_layout(pc):
    nb = pc.shape[0]
    tm = ROW_TILE
    n_tiles = (nb * (2 * MOE_BLOCK + N_EXPERTS * (MOE_GROUP - 1)) + tm - 1) // tm + N_EXPERTS
    loc = jnp.cumsum(pc, axis=1) - pc
    tiles_e = (jnp.sum(pc, axis=0) + tm - 1) // tm
    tile_end = jnp.cumsum(tiles_e)
    base_e = (tile_end - tiles_e) * tm
    glb = base_e[None, :] + jnp.cumsum(pc, axis=0) - pc
    n_valid = tile_end[-1]
    ids = jnp.arange(n_tiles, dtype=jnp.int32)
    tile_row = jnp.maximum(jnp.minimum(ids, n_valid - 1), 0)
    tile_expert = jnp.sum((tile_row[:, None] >= tile_end[None, :]).astype(jnp.int32), axis=1)
    tile_valid = (ids < n_valid).astype(jnp.int32)
    rows_e = jnp.sum(pc, axis=0)
    first = jnp.zeros_like(rows_e).at[0].set(1)
    loc = jnp.concatenate([loc, jnp.zeros_like(rows_e)[None], jnp.zeros_like(rows_e)[None]])
    glb = jnp.concatenate([glb, (base_e + rows_e)[None], (first * n_valid * tm)[None]])
    seg = jnp.concatenate([pc, (tiles_e * tm - rows_e)[None],
                           (first * (n_tiles - n_valid) * (tm // SEG_SIZES[0]))[None]])
    flat = lambda a: a.reshape(-1).astype(jnp.int32)
    return flat(loc), flat(glb), flat(seg), tile_expert.astype(jnp.int32), tile_row.astype(jnp.int32), tile_valid, n_tiles * tm


def _moe_block(a, w_proj, gate_proj, x, gain, shift, scale, gate, router, w_in, w_out, layer, seq,
               final_gain=None):
    x, h, meta, meta_t, pc = _moe_route(a, w_proj, gate_proj, x, gain, shift, scale, router, seq)
    pc_i = pc[:, 0, :N_EXPERTS].astype(jnp.int32)
    loc, glb, seg_len, tile_expert, tile_row, tile_valid, rows_total = _moe_layout(pc_i)
    hs = _moe_dispatch(h, meta_t, loc, glb, seg_len, rows_total)
    ys = _moe_experts(hs, w_in, w_out, tile_expert + layer * N_EXPERTS, tile_row, tile_valid)
    return _moe_combine(x, gate, meta, ys, loc, glb, seg_len, seq, final_gain)


def _final_norm_kernel(x_ref, g_ref, o_ref):
    x = x_ref[...]
    o_ref[...] = (x * lax.rsqrt(jnp.mean(x * x, axis=-1, keepdims=True) + EPS)) * g_ref[...]


def _final_norm(x, gain):
    n, d = x.shape
    tm = min(ROW_TILE, n)
    return pl.pallas_call(
        _final_norm_kernel,
        out_shape=jax.ShapeDtypeStruct((n, d), F32),
        grid=(n // tm,),
        in_specs=[pl.BlockSpec((tm, d), lambda i: (i, 0)), pl.BlockSpec((1, d), lambda i: (0, 0))],
        out_specs=pl.BlockSpec((tm, d), lambda i: (i, 0)),
        compiler_params=_cparams(("parallel",)),
        name="final_norm",
    )(x, gain.reshape(1, d))


def kernel(x, c, positions, ada_w, ada_b, norm_mix, norm_ff, ret_w_in, ret_gn, ret_w_out, kv_ada_w, kv_ada_b, kv_norm, kv_w, sb_w_q, sb_w_out, ff_w_in, ff_w_out, moe_router, moe_w_in, moe_w_out, final_norm):
    bsz, seq, d = x.shape
    n = bsz * seq
    mod = _ada_mod(c, ada_w, ada_b)
    kv_mod = _ada_mod(c, kv_ada_w[None], kv_ada_b[None])[0]
    rope = _rope_tables(positions)
    ret_tables = _retention_tables()
    moe_in = moe_w_in.astype(BF16).reshape(-1, d, 2 * FF_EXPERT)
    moe_out = moe_w_out.astype(BF16).reshape(-1, FF_EXPERT, d)
    xf = x.reshape(n, d)
    kv = None
    for i in range(DEPTH):
        sh_m, sc_m, g_m, sh_f, sc_f, g_f = [mod[i, :, p * d:(p + 1) * d] for p in range(6)]
        if i < N_A:
            proj = _norm_mod_matmul(xf, norm_mix[i], sh_m, sc_m, ret_w_in[i].astype(BF16), seq, rope=rope)
            o = _retention_core(proj.reshape(bsz, seq, -1), ret_gn[i], ret_tables, bsz, seq)
            o, w_o = o.reshape(n, -1), ret_w_out[i].astype(BF16)
        else:
            j = i - N_A
            q = _norm_mod_matmul(xf, norm_mix[i], sh_m, sc_m, sb_w_q[j].astype(BF16), seq)
            o = _stick_breaking_core(q.reshape(bsz, seq, d), kv, bsz, seq)
            o, w_o = o.reshape(n, d), sb_w_out[j].astype(BF16)
        if i % 2 == 0:
            xf = _dense_ffn(o, w_o, g_m, xf, norm_ff[i], sh_f, sc_f, g_f, ff_w_in[i // 2].astype(BF16),
                            ff_w_out[i // 2].astype(BF16), seq)
        else:
            xf = _moe_block(o, w_o, g_m, xf, norm_ff[i], sh_f, sc_f, g_f, moe_router[i // 2],
                            moe_in, moe_out, i // 2, seq,
                            final_gain=final_norm if i == DEPTH - 1 else None)
        if i == N_A - 1:
            kv = _norm_mod_matmul(xf, kv_norm, kv_mod[:, :d], kv_mod[:, d:], kv_w.astype(BF16), seq)
            kv = kv.reshape(bsz, seq, 2 * d)
    if (DEPTH - 1) % 2 == 0:
        xf = _final_norm(xf, final_norm)
    return xf.reshape(bsz, seq, d)
```

```python
import functools

import jax
import jax.numpy as jnp
from jax import lax
from jax.experimental import pallas as pl
from jax.experimental.pallas import tpu as pltpu

F32 = jnp.float32
BF16 = jnp.bfloat16

D_MODEL = 1024
DEPTH = 4
N_A = DEPTH // 2
CHUNK = 64
RET_HEADS = 8
RET_DK = D_MODEL // RET_HEADS
RET_DV = 2 * D_MODEL // RET_HEADS
ROPE_BASE = 10000.0
SB_HEADS = 16
SB_DH = D_MODEL // SB_HEADS
FF_DENSE = 2816
N_EXPERTS = 8
FF_EXPERT = 3584
EPS = 1e-6

LANES = 128
BF16_ROWS = 16
MXU_COLS = 256
VMEM_LIMIT = 56 * 1024 * 1024

ROW_TILE = 1024
NMM_ROWS = 512
NMM_WIDE = 2048
NMM_CHUNK = 512
DENSE_ROWS = 512
FF_EXPERT_TILE = 1792
RET_BLOCK = 256
SB_TILE = 128
MOE_BLOCK = 512
ROUTE_BLOCKS = 2
ROUTE_STRIP = 256
MOE_GROUP = BF16_ROWS
MOE_CAP = 2 * MOE_BLOCK + N_EXPERTS * MOE_GROUP
SEG_SIZES = (512, 256, 128, 64, 32, 16)
SB_LOG_FLOOR = -110.0


def _cparams(sem):
    return pltpu.CompilerParams(dimension_semantics=sem, vmem_limit_bytes=VMEM_LIMIT)


def _sigmoid(v):
    return 1.0 / (1.0 + jnp.exp(-v))


def _rms_mod(x, gain, shift, scale):
    y = x * lax.rsqrt(jnp.mean(x * x, axis=-1, keepdims=True) + EPS)
    return (y * gain) * (1.0 + scale) + shift


def _ada_kernel(c_ref, w_ref, b_ref, o_ref):
    c = c_ref[...]
    cond = (c * _sigmoid(c)).astype(BF16)
    o_ref[0] = jnp.dot(cond, w_ref[0].astype(BF16), preferred_element_type=F32) + b_ref[0]


def _ada_mod(c, w, b):
    nl, d, n = w.shape
    bsz = c.shape[0]
    tn = 1024
    return pl.pallas_call(
        _ada_kernel,
        out_shape=jax.ShapeDtypeStruct((nl, bsz, n), F32),
        grid=(nl, n // tn),
        in_specs=[
            pl.BlockSpec((bsz, d), lambda l, j: (0, 0)),
            pl.BlockSpec((1, d, tn), lambda l, j: (l, 0, j)),
            pl.BlockSpec((1, 1, tn), lambda l, j: (l, 0, j)),
        ],
        out_specs=pl.BlockSpec((1, bsz, tn), lambda l, j: (l, 0, j)),
        compiler_params=_cparams(("parallel", "parallel")),
        name="ada_mod",
    )(c, w, b.reshape(nl, 1, n))


def _rope_table_kernel(pos_ref, freq_ref, cs_ref, sn_ref):
    ang = pos_ref[...] * freq_ref[...]
    lane = lax.broadcasted_iota(jnp.int32, ang.shape, 1)
    cs_ref[...] = jnp.cos(ang)
    sn_ref[...] = jnp.where(lane < RET_DK // 2, -jnp.sin(ang), jnp.sin(ang))


def _rope_tables(positions):
    n = positions.size
    inv_freq = jnp.power(ROPE_BASE, -jnp.arange(0, RET_DK, 2, dtype=F32) / RET_DK)
    freq = jnp.concatenate([inv_freq, inv_freq]).reshape(1, RET_DK)
    pos = jnp.broadcast_to(positions.astype(F32).reshape(n, 1), (n, RET_DK))
    tr = min(n, 2048)
    return pl.pallas_call(
        _rope_table_kernel,
        out_shape=(jax.ShapeDtypeStruct((n, RET_DK), F32),) * 2,
        grid=(n // tr,),
        in_specs=[pl.BlockSpec((tr, RET_DK), lambda i: (i, 0)),
                  pl.BlockSpec((1, RET_DK), lambda i: (0, 0))],
        out_specs=(pl.BlockSpec((tr, RET_DK), lambda i: (i, 0)),) * 2,
        compiler_params=_cparams(("parallel",)),
        name="rope_tables",
    )(pos, freq)


def _nmm_kernel(x_ref, g_ref, sh_ref, sc_ref, w_ref, *rest, rope_cols):
    if rope_cols:
        cs_ref, sn_ref, o_ref = rest
    else:
        (o_ref,) = rest
    cw = NMM_CHUNK
    n_chunks = w_ref.shape[1] // cw
    h = _rms_mod(x_ref[...], g_ref[...], sh_ref[0], sc_ref[0]).astype(BF16)

    def mm(c):
        return jnp.dot(h, w_ref[:, c * cw:(c + 1) * cw], preferred_element_type=F32)

    nxt = mm(0)
    for c in range(n_chunks):
        r = nxt
        if c + 1 < n_chunks:
            nxt = mm(c + 1)
        col0 = c * cw
        if col0 >= rope_cols:
            o_ref[:, col0:col0 + cw] = r.astype(o_ref.dtype)
            continue
        k_scale = RET_DK ** -0.5 if col0 >= rope_cols // 2 else None
        for s in range(cw // RET_DK):
            t = r[:, s * RET_DK:(s + 1) * RET_DK]
            rot = t * cs_ref[...] + pltpu.roll(t, RET_DK // 2, 1) * sn_ref[...]
            if k_scale is not None:
                rot = rot * k_scale
            o_ref[:, col0 + s * RET_DK:col0 + (s + 1) * RET_DK] = rot.astype(o_ref.dtype)


def _norm_mod_matmul(x, gain, shift, scale, w, seq, rope=None):
    n, d = x.shape
    m = w.shape[1]
    tm = min(NMM_ROWS if m > NMM_WIDE else ROW_TILE, seq)
    bsz = shift.shape[0]
    vec = lambda i: (i * tm // seq, 0, 0)
    in_specs = [
        pl.BlockSpec((tm, d), lambda i: (i, 0)),
        pl.BlockSpec((1, d), lambda i: (0, 0)),
        pl.BlockSpec((1, 1, d), vec),
        pl.BlockSpec((1, 1, d), vec),
        pl.BlockSpec((d, m), lambda i: (0, 0), pipeline_mode=pl.Buffered(1)),
    ]
    args = [x, gain.reshape(1, d), shift.reshape(bsz, 1, d), scale.reshape(bsz, 1, d), w]
    rope_cols = 0
    if rope is not None:
        rope_cols = 2 * RET_HEADS * RET_DK
        in_specs += [pl.BlockSpec((tm, RET_DK), lambda i: (i, 0))] * 2
        args += list(rope)
    return pl.pallas_call(
        functools.partial(_nmm_kernel, rope_cols=rope_cols),
        out_shape=jax.ShapeDtypeStruct((n, m), BF16),
        grid=(n // tm,),
        in_specs=in_specs,
        out_specs=pl.BlockSpec((tm, m), lambda i: (i, 0)),
        compiler_params=_cparams(("parallel",)),
        name="norm_mod_matmul_rope" if rope_cols else "norm_mod_matmul",
    )(*args)


def _swiglu(h, load_wa, load_wb, load_wo, width, emit):
    def up(c):
        sl = slice(c * MXU_COLS, (c + 1) * MXU_COLS)
        return (jnp.dot(h, load_wa(sl), preferred_element_type=F32),
                jnp.dot(h, load_wb(sl), preferred_element_type=F32))

    n_chunks = width // MXU_COLS
    acts = []
    nxt = up(0)
    for c in range(n_chunks):
        a, b = nxt
        if c + 1 < n_chunks:
            nxt = up(c + 1)
        acts.append((a * _sigmoid(a) * b).astype(BF16))
    act = jnp.concatenate(acts, axis=1)

    def down(c):
        return jnp.dot(act, load_wo(slice(c * MXU_COLS, (c + 1) * MXU_COLS)), preferred_element_type=F32)

    n_out = D_MODEL // MXU_COLS
    nxt = down(0)
    for c in range(n_out):
        y = nxt
        if c + 1 < n_out:
            nxt = down(c + 1)
        emit(slice(c * MXU_COLS, (c + 1) * MXU_COLS), y)


def _ffn_kernel(a_ref, wp_ref, gp_ref, x_ref, g_ref, sh_ref, sc_ref, gate_ref, wa_ref, wb_ref, wo_ref, o_ref,
                x1_ref):
    x1_ref[...] = x_ref[...] + gp_ref[0] * jnp.dot(a_ref[...], wp_ref[...], preferred_element_type=F32)
    h = _rms_mod(x1_ref[...], g_ref[...], sh_ref[0], sc_ref[0]).astype(BF16)

    def emit(cols, y):
        o_ref[:, cols] = x1_ref[:, cols] + gate_ref[0, :, cols] * y

    _swiglu(h, lambda sl: wa_ref[:, sl], lambda sl: wb_ref[:, sl], lambda sl: wo_ref[:, sl],
            wo_ref.shape[0], emit)


def _dense_ffn(a, w_proj, gate_proj, x, gain, shift, scale, gate, w_in, w_out, seq):
    n, d = x.shape
    k = a.shape[1]
    ff = w_out.shape[0]
    tm = min(DENSE_ROWS, seq)
    bsz = shift.shape[0]
    vec = lambda i: (i * tm // seq, 0, 0)
    resident = pl.Buffered(1)
    return pl.pallas_call(
        _ffn_kernel,
        out_shape=jax.ShapeDtypeStruct((n, d), F32),
        grid=(n // tm,),
        in_specs=[
            pl.BlockSpec((tm, k), lambda i: (i, 0)),
            pl.BlockSpec((k, d), lambda i: (0, 0), pipeline_mode=resident),
            pl.BlockSpec((1, 1, d), vec),
            pl.BlockSpec((tm, d), lambda i: (i, 0)),
            pl.BlockSpec((1, d), lambda i: (0, 0)),
            pl.BlockSpec((1, 1, d), vec),
            pl.BlockSpec((1, 1, d), vec),
            pl.BlockSpec((1, 1, d), vec),
            pl.BlockSpec((d, ff), lambda i: (0, 0), pipeline_mode=resident),
            pl.BlockSpec((d, ff), lambda i: (0, 1), pipeline_mode=resident),
            pl.BlockSpec((ff, d), lambda i: (0, 0), pipeline_mode=resident),
        ],
        out_specs=pl.BlockSpec((tm, d), lambda i: (i, 0)),
        scratch_shapes=[pltpu.VMEM((tm, d), F32)],
        compiler_params=_cparams(("parallel",)),
        name="dense_ffn",
    )(a, w_proj, gate_proj.reshape(bsz, 1, d), x, gain.reshape(1, d), shift.reshape(bsz, 1, d),
      scale.reshape(bsz, 1, d), gate.reshape(bsz, 1, d), w_in, w_in, w_out)


def _ret_kernel(q_ref, k_ref, v_ref, g_ref, gn_ref, dm_ref, qd_ref, kd_ref, cd_ref, o_ref, *, nblk):
    dmat = dm_ref[0]
    qdec = qd_ref[0]
    kdec = kd_ref[0]
    cdec = cd_ref[0, 0:1, :]
    gn = gn_ref[...]

    def load(t):
        sl = slice(t * RET_BLOCK, (t + 1) * RET_BLOCK)
        return q_ref[0, sl, :], k_ref[0, sl, :], v_ref[0, sl, :]

    def scores(q, k):
        return lax.dot_general(q, k, (((1,), (1,)), ((), ())), preferred_element_type=F32)

    state = jnp.zeros((RET_DK, RET_DV), F32)
    q, k, v = load(0)
    s = scores(q, k)
    for t in range(nblk):
        qd = (q.astype(F32) * qdec).astype(BF16)
        cross = jnp.dot(qd, state.astype(BF16), preferred_element_type=F32)
        kdt = jnp.transpose(k.astype(F32) * kdec).astype(BF16)
        update = jnp.dot(kdt, v, preferred_element_type=F32)
        if t + 1 < nblk:
            q, k, v_next = load(t + 1)
            s_next = scores(q, k)
        intra = jnp.dot((s * dmat).astype(BF16), v, preferred_element_type=F32)
        state = state * cdec + update
        o = intra + cross
        mu = jnp.mean(o, axis=-1, keepdims=True)
        oc = o - mu
        var = jnp.mean(oc * oc, axis=-1, keepdims=True)
        on = (oc * lax.rsqrt(var + EPS)) * gn
        sl = slice(t * RET_BLOCK, (t + 1) * RET_BLOCK)
        g = g_ref[0, sl, :].astype(F32)
        o_ref[0, sl, :] = ((g * _sigmoid(g)) * on).astype(o_ref.dtype)
        if t + 1 < nblk:
            v, s = v_next, s_next


def _retention_tables():
    t = RET_BLOCK
    log_gamma = jnp.log(1.0 - jnp.exp2(-5.0 - jnp.arange(RET_HEADS, dtype=F32)))
    pos = jnp.arange(t, dtype=F32)
    chunk = jnp.arange(t) // CHUNK
    dist = jnp.abs(pos[:, None] - pos[None, :])
    visible = (chunk[None, :] <= chunk[:, None]).astype(F32)
    dmat = jnp.exp(dist[None] * log_gamma[:, None, None]) * visible[None]
    qdec = jnp.exp((pos[None, :] + 1.0) * log_gamma[:, None])
    kdec = jnp.exp((t - 1.0 - pos)[None, :] * log_gamma[:, None])
    cdec = jnp.exp(t * log_gamma)
    qdec = jnp.broadcast_to(qdec[:, :, None], (RET_HEADS, t, RET_DK))
    kdec = jnp.broadcast_to(kdec[:, :, None], (RET_HEADS, t, RET_DK))
    cdec = jnp.broadcast_to(cdec[:, None, None], (RET_HEADS, 8, RET_DV))
    return dmat, qdec, kdec, cdec


def _retention_core(proj, gn_w, tables, bsz, seq):
    dmat, qdec, kdec, cdec = tables
    h = RET_HEADS
    t = RET_BLOCK
    kq = h * RET_DK // RET_DK
    kv = 2 * h * RET_DK // RET_DV
    return pl.pallas_call(
        functools.partial(_ret_kernel, nblk=seq // t),
        out_shape=jax.ShapeDtypeStruct((bsz, seq, h * RET_DV), BF16),
        grid=(bsz, h),
        in_specs=[
            pl.BlockSpec((1, seq, RET_DK), lambda b, i: (b, 0, i)),
            pl.BlockSpec((1, seq, RET_DK), lambda b, i: (b, 0, kq + i)),
            pl.BlockSpec((1, seq, RET_DV), lambda b, i: (b, 0, kv + i)),
            pl.BlockSpec((1, seq, RET_DV), lambda b, i: (b, 0, kv + h + i)),
            pl.BlockSpec((1, RET_DV), lambda b, i: (0, i)),
            pl.BlockSpec((1, t, t), lambda b, i: (i, 0, 0)),
            pl.BlockSpec((1, t, RET_DK), lambda b, i: (i, 0, 0)),
            pl.BlockSpec((1, t, RET_DK), lambda b, i: (i, 0, 0)),
            pl.BlockSpec((1, 8, RET_DV), lambda b, i: (i, 0, 0)),
        ],
        out_specs=pl.BlockSpec((1, seq, RET_DV), lambda b, i: (b, 0, i)),
        compiler_params=_cparams(("parallel", "parallel")),
        name="retention_core",
    )(proj, proj, proj, proj, gn_w.reshape(1, h * RET_DV), dmat, qdec, kdec, cdec)


def _sb_kernel(q_ref, k_ref, v_ref, o_ref, r_ref, acc_ref):
    t = SB_TILE
    w = 2 * SB_DH
    pairs = SB_HEADS // 2
    qi = pl.program_id(1)
    lane = lax.broadcasted_iota(jnp.int32, (t, w), 1)
    row = lax.broadcasted_iota(jnp.int32, (t, 2 * t), 0)
    col = lax.broadcasted_iota(jnp.int32, (t, 2 * t), 1)
    diag_causal = jnp.where(col >= t, col - t, col) < row
    uj = lax.broadcasted_iota(jnp.int32, (2 * t, 2 * t), 0)
    us = lax.broadcasted_iota(jnp.int32, (2 * t, 2 * t), 1)
    uj = jnp.where(uj >= t, uj - t, uj)
    umat = jnp.where((uj >= us) | (us >= t), -1.0, 0.0).astype(BF16)
    scale = jnp.asarray(SB_DH ** -0.5, BF16)
    qs = [q_ref[0, :, p * w:(p + 1) * w] * scale for p in range(pairs)]

    def split_heads(x):
        zero = jnp.zeros_like(x)
        return jnp.concatenate([jnp.where(lane < SB_DH, x, zero), jnp.where(lane >= SB_DH, x, zero)], axis=0)

    whole, top, bottom = slice(0, t), slice(0, t // 2), slice(t // 2, t)

    def key_tiles(kbs, causal, first, ps, rows=None):
        rows = rows or [whole] * len(kbs)
        spans = [pl.ds(pl.multiple_of(kb * t, t), t) for kb in kbs]
        tiles = [(j, p) for j in range(len(kbs)) for p in ps]
        zs, his, los, incls, tots = {}, {}, {}, {}, {}
        for j, p in tiles:
            kcat = split_heads(k_ref[0, spans[j], p * w:(p + 1) * w])
            zs[j, p] = lax.dot_general(qs[p][rows[j]], kcat, (((1,), (1,)), ((), ())),
                                       preferred_element_type=F32)
        for j, p in tiles:
            z = zs[j, p]
            sp = jnp.maximum(z, 0.0) + jnp.log(1.0 + jnp.exp(-jnp.abs(z)))
            if causal is not None and j == 0:
                sp = jnp.where(causal, sp, 0.0)
            hi = sp.astype(BF16)
            his[j, p] = hi
            los[j, p] = (sp - hi.astype(F32)).astype(BF16)
        for j, p in tiles:
            cs = [jnp.dot(jnp.concatenate([his[j, p][:, h * t:(h + 1) * t], los[j, p][:, h * t:(h + 1) * t]],
                                          axis=1), umat, preferred_element_type=F32) for h in range(2)]
            incls[j, p] = jnp.concatenate([cs[0][:, :t], cs[1][:, :t]], axis=1)
            tots[j, p] = jnp.concatenate([cs[0][:, t:], cs[1][:, t:]], axis=1)
        avs = {}
        for j, p in tiles:
            e = zs[j, p] + incls[j, p]
            if first:
                r_new = tots[j, p]
            else:
                r_old = r_ref[p, rows[j], :]
                e = e + r_old
                r_new = r_old + tots[j, p]
            a = jnp.exp(e)
            if causal is not None and j == 0:
                a = jnp.where(causal, a, 0.0)
            avs[j, p] = a.astype(BF16)
            r_ref[p, rows[j], :] = r_new
        for j, p in tiles:
            vcat = split_heads(v_ref[0, spans[j], p * w:(p + 1) * w])
            av = jnp.dot(avs[j, p], vcat, preferred_element_type=F32)
            acc_ref[p, rows[j], :] = av if first else acc_ref[p, rows[j], :] + av
        below = [jnp.max(r_ref[p, bottom, :]) > SB_LOG_FLOOR for p in ps]
        above = [jnp.max(r_ref[p, top, :]) > SB_LOG_FLOOR for p in ps]
        return (tuple(jnp.logical_or(a, b).astype(jnp.int32) for a, b in zip(above, below)),
                tuple(b.astype(jnp.int32) for b in below))

    everyone = list(range(pairs))
    none = tuple(jnp.int32(0) for _ in everyone)
    left, _ = key_tiles([qi], diag_causal, True, everyone)
    kb = qi - 1

    two = jnp.logical_and(kb >= 1, sum(left) > 0)
    left, below = lax.cond(two, lambda: key_tiles([kb, kb - 1], None, False, everyone, [whole, top]),
                           lambda: (left, none))
    left = tuple(lax.cond(below[p] > 0, lambda p=p: key_tiles([kb - 1], None, False, [p], [bottom])[0][0],
                          lambda p=p: left[p]) for p in everyone)
    kb = jnp.where(two, kb - 2, kb)

    def cond(carry):
        kb, *left = carry
        return jnp.logical_and(kb >= 0, sum(left) > 0)

    def body(carry):
        kb, *left = carry

        def stragglers():
            return tuple(lax.cond(left[p] > 0, lambda p=p: key_tiles([kb], None, False, [p])[0][0],
                                  lambda: jnp.int32(0)) for p in everyone)

        left = lax.cond(sum(left) > pairs // 2, lambda: key_tiles([kb], None, False, everyone)[0], stragglers)
        return (kb - 1, *left)

    lax.while_loop(cond, body, (kb, *left))
    for p in range(pairs):
        o_ref[0, :, p * w:(p + 1) * w] = acc_ref[p].astype(o_ref.dtype)


def _stick_breaking_core(q, kv, bsz, seq):
    t = SB_TILE
    return pl.pallas_call(
        _sb_kernel,
        out_shape=jax.ShapeDtypeStruct((bsz, seq, D_MODEL), BF16),
        grid=(bsz, seq // t),
        in_specs=[
            pl.BlockSpec((1, t, D_MODEL), lambda b, i: (b, i, 0)),
            pl.BlockSpec((1, seq, D_MODEL), lambda b, i: (b, 0, 0)),
            pl.BlockSpec((1, seq, D_MODEL), lambda b, i: (b, 0, 1)),
        ],
        out_specs=pl.BlockSpec((1, t, D_MODEL), lambda b, i: (b, i, 0)),
        scratch_shapes=[pltpu.VMEM((SB_HEADS // 2, t, 2 * t), F32),
                        pltpu.VMEM((SB_HEADS // 2, t, 2 * SB_DH), F32)],
        compiler_params=_cparams(("parallel", "parallel")),
        name="stick_breaking_core",
    )(q, kv, kv)


def _route_kernel(a_ref, wp_ref, gp_ref, x_ref, g_ref, sh_ref, sc_ref, r_ref,
                  x1_ref, h_ref, meta_ref, meta_t_ref, pc_ref):
    tb = MOE_BLOCK
    rows = x_ref.shape[0]
    strips = [slice(s, s + ROUTE_STRIP) for s in range(0, rows, ROUTE_STRIP)]
    ys = [jnp.dot(a_ref[rs, :], wp_ref[...], preferred_element_type=F32) for rs in strips]
    parts = []
    for rs, y in zip(strips, ys):
        x1 = x_ref[rs, :] + gp_ref[0] * y
        x1_ref[rs, :] = x1
        h32 = _rms_mod(x1, g_ref[...], sh_ref[0], sc_ref[0])
        h1 = h32.astype(BF16)
        h_ref[rs, :] = h1
        e1 = h32 - h1.astype(F32)
        h2 = e1.astype(BF16)
        parts.append((h1, h2, (e1 - h2.astype(F32)).astype(BF16)))
    r3 = r_ref[...]
    s3 = jnp.concatenate([jnp.dot(h1, r3, preferred_element_type=F32) + jnp.dot(h2, r3, preferred_element_type=F32)
                          + jnp.dot(h3, r3, preferred_element_type=F32) for h1, h2, h3 in parts], axis=0)
    logits = s3 + pltpu.roll(s3, LANES - N_EXPERTS, 1) + pltpu.roll(s3, LANES - 2 * N_EXPERTS, 1)
    lane = lax.broadcasted_iota(jnp.int32, (rows, LANES), 1)
    lane_f = lane.astype(F32)
    neg = jnp.float32(-jnp.inf)
    lg = jnp.where(lane < N_EXPERTS, logits, neg)
    m1 = jnp.max(lg, axis=-1, keepdims=True)
    i1 = jnp.min(jnp.where(lg == m1, lane_f, float(LANES)), axis=-1, keepdims=True)
    oh1 = lane_f == i1
    lg2 = jnp.where(oh1, neg, lg)
    m2 = jnp.max(lg2, axis=-1, keepdims=True)
    i2 = jnp.min(jnp.where(lg2 == m2, lane_f, float(LANES)), axis=-1, keepdims=True)
    oh2 = lane_f == i2
    e = jnp.exp(m2 - m1)
    w1 = 1.0 / (1.0 + e)
    w2 = e / (1.0 + e)

    tr = lax.broadcasted_iota(jnp.int32, (tb, tb), 0)
    tc = lax.broadcasted_iota(jnp.int32, (tb, tb), 1)
    lower = jnp.where(tc < tr, 1.0, 0.0).astype(BF16)
    pj = lax.broadcasted_iota(jnp.int32, (LANES, LANES), 0)
    ps = lax.broadcasted_iota(jnp.int32, (LANES, LANES), 1)
    before = jnp.where(pj < ps, 1.0, 0.0)
    oh1f = jnp.where(oh1, 1.0, 0.0)
    oh2f = jnp.where(oh2, 1.0, 0.0)
    blocks = [slice(b * tb, (b + 1) * tb) for b in range(rows // tb)]
    cum1 = [jnp.dot(lower, oh1f[rs].astype(BF16), preferred_element_type=F32) for rs in blocks]
    cum2 = [jnp.dot(lower, oh2f[rs].astype(BF16), preferred_element_type=F32) for rs in blocks]
    cnt1 = [jnp.sum(oh1f[rs], axis=0, keepdims=True) for rs in blocks]
    cnt2 = [jnp.sum(oh2f[rs], axis=0, keepdims=True) for rs in blocks]
    pc = [jnp.floor((c1 + c2 + (MOE_GROUP - 1.0)) / MOE_GROUP) * MOE_GROUP
          for c1, c2 in zip(cnt1, cnt2)]
    off = [jnp.dot(jnp.broadcast_to(c, (8, LANES)), before, preferred_element_type=F32,
                   precision=lax.Precision.HIGHEST)[0:1] for c in pc]
    pos1 = jnp.concatenate([jnp.sum(oh1f[rs] * (off[b] + cum1[b]), axis=-1, keepdims=True)
                            for b, rs in enumerate(blocks)], axis=0)
    pos2 = jnp.concatenate([jnp.sum(oh2f[rs] * (off[b] + cnt1[b] + cum2[b]), axis=-1, keepdims=True)
                            for b, rs in enumerate(blocks)], axis=0)
    meta = jnp.where(lane == 0, pos1,
                     jnp.where(lane == 1, pos2,
                               jnp.where(lane == 2, w1, jnp.where(lane == 3, w2, 0.0))))
    meta_ref[...] = meta
    for b, rs in enumerate(blocks):
        meta_t_ref[b] = jnp.transpose(meta[rs])[0:8]
        pc_ref[b] = jnp.broadcast_to(pc[b], (8, LANES))


def _moe_route(a, w_proj, gate_proj, x, gain, shift, scale, router, seq):
    n, d = x.shape
    k = a.shape[1]
    tb = MOE_BLOCK
    nb = n // tb
    rb = min(ROUTE_BLOCKS, seq // tb)
    rows = rb * tb
    bsz = shift.shape[0]
    r1 = router.astype(BF16)
    r2 = (router - r1.astype(F32)).astype(BF16)
    r3 = (router - r1.astype(F32) - r2.astype(F32)).astype(BF16)
    r_pad = jnp.zeros((d, LANES), BF16).at[:, :3 * N_EXPERTS].set(jnp.concatenate([r1, r2, r3], axis=1))
    vec = lambda i: (i * rows // seq, 0, 0)
    return pl.pallas_call(
        _route_kernel,
        out_shape=(
            jax.ShapeDtypeStruct((n, d), F32),
            jax.ShapeDtypeStruct((n, d), BF16),
            jax.ShapeDtypeStruct((n, LANES), F32),
            jax.ShapeDtypeStruct((nb, 8, tb), F32),
            jax.ShapeDtypeStruct((nb, 8, LANES), F32),
        ),
        grid=(nb // rb,),
        in_specs=[
            pl.BlockSpec((rows, k), lambda i: (i, 0)),
            pl.BlockSpec((k, d), lambda i: (0, 0), pipeline_mode=pl.Buffered(1)),
            pl.BlockSpec((1, 1, d), vec),
            pl.BlockSpec((rows, d), lambda i: (i, 0)),
            pl.BlockSpec((1, d), lambda i: (0, 0)),
            pl.BlockSpec((1, 1, d), vec),
            pl.BlockSpec((1, 1, d), vec),
            pl.BlockSpec((d, LANES), lambda i: (0, 0)),
        ],
        out_specs=(
            pl.BlockSpec((rows, d), lambda i: (i, 0)),
            pl.BlockSpec((rows, d), lambda i: (i, 0)),
            pl.BlockSpec((rows, LANES), lambda i: (i, 0)),
            pl.BlockSpec((rb, 8, tb), lambda i: (i, 0, 0)),
            pl.BlockSpec((rb, 8, LANES), lambda i: (i, 0, 0)),
        ),
        compiler_params=_cparams(("parallel",)),
        name="moe_route",
    )(a, w_proj, gate_proj.reshape(bsz, 1, d), x, gain.reshape(1, d), shift.reshape(bsz, 1, d),
      scale.reshape(bsz, 1, d), r_pad)


def _segment_copies(loc_ref, glb_ref, len_ref, blk, make_copy, wait):
    for e in range(N_EXPERTS):
        idx = blk * N_EXPERTS + e
        loc = loc_ref[idx]
        glb = glb_ref[idx]
        length = len_ref[idx]
        done = jnp.int32(0)
        for sz in SEG_SIZES:
            take = (length & sz) != 0

            @pl.when(take)
            def _(loc=loc, glb=glb, done=done, sz=sz):
                cp = make_copy(pl.multiple_of(loc + done, MOE_GROUP),
                               pl.multiple_of(glb + done, MOE_GROUP), sz)
                if wait:
                    cp.wait()
                else:
                    cp.start()

            done = done + jnp.where(take, sz, 0)


def _dispatch_kernel(loc_ref, glb_ref, len_ref, h_ref, meta_t_ref, hs_ref, buf_ref, zero_ref, sem):
    b = pl.program_id(0)
    tb = MOE_BLOCK
    pos1 = meta_t_ref[0, 0:1, :]
    pos2 = meta_t_ref[0, 1:2, :]
    r = lax.broadcasted_iota(jnp.int32, (MOE_CAP, tb), 0).astype(F32)
    perm = jnp.where((r == pos1) | (r == pos2), 1.0, 0.0).astype(BF16)
    sorted_rows = jnp.dot(perm, h_ref[...], preferred_element_type=F32).astype(BF16)

    def make_copy(loc, glb, sz):
        return pltpu.make_async_copy(buf_ref.at[pl.ds(loc, sz)], hs_ref.at[pl.ds(glb, sz)], sem)

    @pl.when(b > 0)
    def _():
        _segment_copies(loc_ref, glb_ref, len_ref, b - 1, make_copy, wait=True)

    buf_ref[...] = sorted_rows
    _segment_copies(loc_ref, glb_ref, len_ref, b, make_copy, wait=False)

    @pl.when(b == pl.num_programs(0) - 1)
    def _():
        def make_zero_copy(_, glb, sz):
            return pltpu.make_async_copy(zero_ref.at[pl.ds(0, sz)], hs_ref.at[pl.ds(glb, sz)], sem)

        zero_ref[...] = jnp.zeros_like(zero_ref)
        gaps = pl.num_programs(0)
        _segment_copies(loc_ref, glb_ref, len_ref, gaps, make_zero_copy, wait=False)

        tail = (gaps + 1) * N_EXPERTS
        pieces = len_ref[tail]

        def tail_copy(i):
            row = pl.multiple_of(glb_ref[tail] + i * zero_ref.shape[0], zero_ref.shape[0])
            return pltpu.make_async_copy(zero_ref, hs_ref.at[pl.ds(row, zero_ref.shape[0])], sem)

        @pl.loop(0, pieces)
        def _(i):
            tail_copy(i).start()

        _segment_copies(loc_ref, glb_ref, len_ref, b, make_copy, wait=True)
        _segment_copies(loc_ref, glb_ref, len_ref, gaps, make_zero_copy, wait=True)

        @pl.loop(0, pieces)
        def _(i):
            tail_copy(i).wait()


def _moe_dispatch(h, meta_t, loc, glb, seg_len, rows_total):
    n, d = h.shape
    tb = MOE_BLOCK
    nb = n // tb
    return pl.pallas_call(
        _dispatch_kernel,
        out_shape=jax.ShapeDtypeStruct((rows_total, d), BF16),
        grid_spec=pltpu.PrefetchScalarGridSpec(
            num_scalar_prefetch=3,
            grid=(nb,),
            in_specs=[
                pl.BlockSpec((tb, d), lambda i, *_: (i, 0)),
                pl.BlockSpec((1, 8, tb), lambda i, *_: (i, 0, 0)),
            ],
            out_specs=pl.BlockSpec(memory_space=pl.ANY),
            scratch_shapes=[pltpu.VMEM((MOE_CAP, d), BF16), pltpu.VMEM((SEG_SIZES[0], d), BF16),
                            pltpu.SemaphoreType.DMA],
        ),
        compiler_params=_cparams(("arbitrary",)),
        name="moe_dispatch",
    )(loc, glb, seg_len, h, meta_t)


def _expert_kernel(te_ref, tx_ref, tv_ref, x_ref, wa_ref, wb_ref, wo_ref, o_ref, acc_ref):
    i = pl.program_id(0)
    f = pl.program_id(1)

    last = pl.num_programs(1) - 1
    valid = tv_ref[i] == 1

    def run(emit):
        _swiglu(x_ref[...], lambda sl: wa_ref[0, :, sl], lambda sl: wb_ref[0, :, sl],
                lambda sl: wo_ref[0, :, sl], wo_ref.shape[1], emit)

    @pl.when(jnp.logical_and(valid, f == 0))
    def _():
        def emit(cols, y):
            acc_ref[:, cols] = y
        run(emit)

    if FF_EXPERT // FF_EXPERT_TILE > 2:
        @pl.when(jnp.logical_and(valid, jnp.logical_and(f > 0, f < last)))
        def _():
            def emit(cols, y):
                acc_ref[:, cols] += y
            run(emit)

    @pl.when(jnp.logical_and(valid, f == last))
    def _():
        def emit(cols, y):
            o_ref[:, cols] = (acc_ref[:, cols] + y).astype(o_ref.dtype)
        run(emit)

    @pl.when(jnp.logical_and(tv_ref[i] == 0, f == pl.num_programs(1) - 1))
    def _():
        o_ref[...] = jnp.zeros_like(o_ref)


def _moe_experts(hs, w_in, w_out, tile_expert, tile_row, tile_valid):
    rows, d = hs.shape
    tm = ROW_TILE
    tf = FF_EXPERT_TILE
    nf = FF_EXPERT // tf
    nt = rows // tm

    def f_eff(i, f, tv):
        return jnp.where(tv[i] == 1, f, nf - 1)

    return pl.pallas_call(
        _expert_kernel,
        out_shape=jax.ShapeDtypeStruct((rows, d), BF16),
        grid_spec=pltpu.PrefetchScalarGridSpec(
            num_scalar_prefetch=3,
            grid=(nt, nf),
            in_specs=[
                pl.BlockSpec((tm, d), lambda i, f, te, tx, tv: (tx[i], 0)),
                pl.BlockSpec((1, d, tf), lambda i, f, te, tx, tv: (te[i], 0, f_eff(i, f, tv))),
                pl.BlockSpec((1, d, tf), lambda i, f, te, tx, tv: (te[i], 0, nf + f_eff(i, f, tv))),
                pl.BlockSpec((1, tf, d), lambda i, f, te, tx, tv: (te[i], f_eff(i, f, tv), 0)),
            ],
            out_specs=pl.BlockSpec((tm, d), lambda i, f, te, tx, tv: (i, 0)),
            scratch_shapes=[pltpu.VMEM((tm, d), F32)],
        ),
        compiler_params=_cparams(("arbitrary", "arbitrary")),
        name="moe_experts",
    )(tile_expert, tile_row, tile_valid, hs, w_in, w_in, w_out)


def _combine_kernel(loc_ref, glb_ref, len_ref, x_ref, gate_ref, meta_ref, fg_ref, ys_ref, o_ref, buf_ref, sems,
                    *, final_norm):
    b = pl.program_id(0)
    tb = MOE_BLOCK

    def fetch(blk, slot, wait):
        def make_copy(loc, glb, sz):
            return pltpu.make_async_copy(ys_ref.at[pl.ds(glb, sz)], buf_ref.at[slot, pl.ds(loc, sz)],
                                         sems.at[slot])

        _segment_copies(loc_ref, glb_ref, len_ref, blk, make_copy, wait)

    def start(blk, slot):
        buf_ref[slot] = jnp.zeros(buf_ref.shape[1:], buf_ref.dtype)
        fetch(blk, slot, wait=False)

    @pl.when(b == 0)
    def _():
        start(0, 0)

    @pl.when(b + 1 < pl.num_programs(0))
    def _():
        start(b + 1, (b + 1) % 2)

    slot = b % 2
    meta = meta_ref[...]
    pos1 = meta[:, 0:1]
    pos2 = meta[:, 1:2]
    w1 = meta[:, 2:3]
    w2 = meta[:, 3:4]
    r = lax.broadcasted_iota(jnp.int32, (tb, MOE_CAP), 1).astype(F32)
    sel1 = jnp.where(r == pos1, 1.0, 0.0).astype(BF16)
    sel2 = jnp.where(r == pos2, 1.0, 0.0).astype(BF16)
    fetch(b, slot, wait=True)
    ys = buf_ref[slot]
    y = (w1 * jnp.dot(sel1, ys, preferred_element_type=F32)
         + w2 * jnp.dot(sel2, ys, preferred_element_type=F32))
    x = x_ref[...] + gate_ref[0] * y
    if final_norm:
        x = (x * lax.rsqrt(jnp.mean(x * x, axis=-1, keepdims=True) + EPS)) * fg_ref[...]
    o_ref[...] = x


def _moe_combine(x, gate, meta, ys, loc, glb, seg_len, seq, final_gain=None):
    n, d = x.shape
    tb = MOE_BLOCK
    bsz = gate.shape[0]
    fg = jnp.ones((1, d), F32) if final_gain is None else final_gain.reshape(1, d)
    return pl.pallas_call(
        functools.partial(_combine_kernel, final_norm=final_gain is not None),
        out_shape=jax.ShapeDtypeStruct((n, d), F32),
        grid_spec=pltpu.PrefetchScalarGridSpec(
            num_scalar_prefetch=3,
            grid=(n // tb,),
            in_specs=[
                pl.BlockSpec((tb, d), lambda i, *_: (i, 0)),
                pl.BlockSpec((1, 1, d), lambda i, *_: (i * tb // seq, 0, 0)),
                pl.BlockSpec((tb, LANES), lambda i, *_: (i, 0)),
                pl.BlockSpec((1, d), lambda i, *_: (0, 0)),
                pl.BlockSpec(memory_space=pl.ANY),
            ],
            out_specs=pl.BlockSpec((tb, d), lambda i, *_: (i, 0)),
            scratch_shapes=[pltpu.VMEM((2, MOE_CAP, d), BF16), pltpu.SemaphoreType.DMA((2,))],
        ),
        compiler_params=_cparams(("arbitrary",)),
        name="moe_combine",
    )(loc, glb, seg_len, x, gate.reshape(bsz, 1, d), meta, fg, ys)


def _moe_layout(pc):
    nb = pc.shape[0]
    tm = ROW_TILE
    n_tiles = (nb * (2 * MOE_BLOCK + N_EXPERTS * (MOE_GROUP - 1)) + tm - 1) // tm + N_EXPERTS
    loc = jnp.cumsum(pc, axis=1) - pc
    tiles_e = (jnp.sum(pc, axis=0) + tm - 1) // tm
    tile_end = jnp.cumsum(tiles_e)
    base_e = (tile_end - tiles_e) * tm
    glb = base_e[None, :] + jnp.cumsum(pc, axis=0) - pc
    n_valid = tile_end[-1]
    ids = jnp.arange(n_tiles, dtype=jnp.int32)
    tile_row = jnp.maximum(jnp.minimum(ids, n_valid - 1), 0)
    tile_expert = jnp.sum((tile_row[:, None] >= tile_end[None, :]).astype(jnp.int32), axis=1)
    tile_valid = (ids < n_valid).astype(jnp.int32)
    rows_e = jnp.sum(pc, axis=0)
    first = jnp.zeros_like(rows_e).at[0].set(1)
    loc = jnp.concatenate([loc, jnp.zeros_like(rows_e)[None], jnp.zeros_like(rows_e)[None]])
    glb = jnp.concatenate([glb, (base_e + rows_e)[None], (first * n_valid * tm)[None]])
    seg = jnp.concatenate([pc, (tiles_e * tm - rows_e)[None],
                           (first * (n_tiles - n_valid) * (tm // SEG_SIZES[0]))[None]])
    flat = lambda a: a.reshape(-1).astype(jnp.int32)
    return flat(loc), flat(glb), flat(seg), tile_expert.astype(jnp.int32), tile_row.astype(jnp.int32), tile_valid, n_tiles * tm


def _moe_block(a, w_proj, gate_proj, x, gain, shift, scale, gate, router, w_in, w_out, layer, seq,
               final_gain=None):
    x, h, meta, meta_t, pc = _moe_route(a, w_proj, gate_proj, x, gain, shift, scale, router, seq)
    pc_i = pc[:, 0, :N_EXPERTS].astype(jnp.int32)
    loc, glb, seg_len, tile_expert, tile_row, tile_valid, rows_total = _moe_layout(pc_i)
    hs = _moe_dispatch(h, meta_t, loc, glb, seg_len, rows_total)
    ys = _moe_experts(hs, w_in, w_out, tile_expert + layer * N_EXPERTS, tile_row, tile_valid)
    return _moe_combine(x, gate, meta, ys, loc, glb, seg_len, seq, final_gain)


def _final_norm_kernel(x_ref, g_ref, o_ref):
    x = x_ref[...]
    o_ref[...] = (x * lax.rsqrt(jnp.mean(x * x, axis=-1, keepdims=True) + EPS)) * g_ref[...]


def _final_norm(x, gain):
    n, d = x.shape
    tm = min(ROW_TILE, n)
    return pl.pallas_call(
        _final_norm_kernel,
        out_shape=jax.ShapeDtypeStruct((n, d), F32),
        grid=(n // tm,),
        in_specs=[pl.BlockSpec((tm, d), lambda i: (i, 0)), pl.BlockSpec((1, d), lambda i: (0, 0))],
        out_specs=pl.BlockSpec((tm, d), lambda i: (i, 0)),
        compiler_params=_cparams(("parallel",)),
        name="final_norm",
    )(x, gain.reshape(1, d))


def kernel(x, c, positions, ada_w, ada_b, norm_mix, norm_ff, ret_w_in, ret_gn, ret_w_out, kv_ada_w, kv_ada_b, kv_norm, kv_w, sb_w_q, sb_w_out, ff_w_in, ff_w_out, moe_router, moe_w_in, moe_w_out, final_norm):
    bsz, seq, d = x.shape
    n = bsz * seq
    mod = _ada_mod(c, ada_w, ada_b)
    kv_mod = _ada_mod(c, kv_ada_w[None], kv_ada_b[None])[0]
    rope = _rope_tables(positions)
    ret_tables = _retention_tables()
    moe_in = moe_w_in.astype(BF16).reshape(-1, d, 2 * FF_EXPERT)
    moe_out = moe_w_out.astype(BF16).reshape(-1, FF_EXPERT, d)
    xf = x.reshape(n, d)
    kv = None
    for i in range(DEPTH):
        sh_m, sc_m, g_m, sh_f, sc_f, g_f = [mod[i, :, p * d:(p + 1) * d] for p in range(6)]
        if i < N_A:
            proj = _norm_mod_matmul(xf, norm_mix[i], sh_m, sc_m, ret_w_in[i].astype(BF16), seq, rope=rope)
            o = _retention_core(proj.reshape(bsz, seq, -1), ret_gn[i], ret_tables, bsz, seq)
            o, w_o = o.reshape(n, -1), ret_w_out[i].astype(BF16)
        else:
            j = i - N_A
            q = _norm_mod_matmul(xf, norm_mix[i], sh_m, sc_m, sb_w_q[j].astype(BF16), seq)
            o = _stick_breaking_core(q.reshape(bsz, seq, d), kv, bsz, seq)
            o, w_o = o.reshape(n, d), sb_w_out[j].astype(BF16)
        if i % 2 == 0:
            xf = _dense_ffn(o, w_o, g_m, xf, norm_ff[i], sh_f, sc_f, g_f, ff_w_in[i // 2].astype(BF16),
                            ff_w_out[i // 2].astype(BF16), seq)
        else:
            xf = _moe_block(o, w_o, g_m, xf, norm_ff[i], sh_f, sc_f, g_f, moe_router[i // 2],
                            moe_in, moe_out, i // 2, seq,
                            final_gain=final_norm if i == DEPTH - 1 else None)
        if i == N_A - 1:
            kv = _norm_mod_matmul(xf, kv_norm, kv_mod[:, :d], kv_mod[:, d:], kv_w.astype(BF16), seq)
            kv = kv.reshape(bsz, seq, 2 * d)
    if (DEPTH - 1) % 2 == 0:
        xf = _final_norm(xf, final_norm)
    return xf.reshape(bsz, seq, d)
```

```python
import functools

import jax
import jax.numpy as jnp
from jax import lax
from jax.experimental import pallas as pl
from jax.experimental.pallas import tpu as pltpu

F32 = jnp.float32
BF16 = jnp.bfloat16

D_MODEL = 1024
DEPTH = 4
N_A = DEPTH // 2
CHUNK = 64
RET_HEADS = 8
RET_DK = D_MODEL // RET_HEADS
RET_DV = 2 * D_MODEL // RET_HEADS
ROPE_BASE = 10000.0
SB_HEADS = 16
SB_DH = D_MODEL // SB_HEADS
FF_DENSE = 2816
N_EXPERTS = 8
FF_EXPERT = 3584
EPS = 1e-6

LANES = 128
BF16_ROWS = 16
MXU_COLS = 256
VMEM_LIMIT = 56 * 1024 * 1024

ROW_TILE = 1024
NMM_ROWS = 512
NMM_WIDE = 2048
NMM_CHUNK = 512
DENSE_ROWS = 512
FF_EXPERT_TILE = 1792
RET_BLOCK = 256
SB_TILE = 128
MOE_BLOCK = 512
ROUTE_BLOCKS = 2
ROUTE_STRIP = 256
MOE_GROUP = BF16_ROWS
MOE_CAP = 2 * MOE_BLOCK + N_EXPERTS * MOE_GROUP
SEG_SIZES = (512, 256, 128, 64, 32, 16)
SB_LOG_FLOOR = -110.0


def _cparams(sem):
    return pltpu.CompilerParams(dimension_semantics=sem, vmem_limit_bytes=VMEM_LIMIT)


def _sigmoid(v):
    return 1.0 / (1.0 + jnp.exp(-v))


def _rms_mod(x, gain, shift, scale):
    y = x * lax.rsqrt(jnp.mean(x * x, axis=-1, keepdims=True) + EPS)
    return (y * gain) * (1.0 + scale) + shift


def _ada_kernel(c_ref, w_ref, b_ref, o_ref):
    c = c_ref[...]
    cond = (c * _sigmoid(c)).astype(BF16)
    o_ref[0] = jnp.dot(cond, w_ref[0].astype(BF16), preferred_element_type=F32) + b_ref[0]


def _ada_mod(c, w, b):
    nl, d, n = w.shape
    bsz = c.shape[0]
    tn = 1024
    return pl.pallas_call(
        _ada_kernel,
        out_shape=jax.ShapeDtypeStruct((nl, bsz, n), F32),
        grid=(nl, n // tn),
        in_specs=[
            pl.BlockSpec((bsz, d), lambda l, j: (0, 0)),
            pl.BlockSpec((1, d, tn), lambda l, j: (l, 0, j)),
            pl.BlockSpec((1, 1, tn), lambda l, j: (l, 0, j)),
        ],
        out_specs=pl.BlockSpec((1, bsz, tn), lambda l, j: (l, 0, j)),
        compiler_params=_cparams(("parallel", "parallel")),
        name="ada_mod",
    )(c, w, b.reshape(nl, 1, n))


def _rope_table_kernel(pos_ref, freq_ref, cs_ref, sn_ref):
    ang = pos_ref[...] * freq_ref[...]
    lane = lax.broadcasted_iota(jnp.int32, ang.shape, 1)
    cs_ref[...] = jnp.cos(ang)
    sn_ref[...] = jnp.where(lane < RET_DK // 2, -jnp.sin(ang), jnp.sin(ang))


def _rope_tables(positions):
    n = positions.size
    inv_freq = jnp.power(ROPE_BASE, -jnp.arange(0, RET_DK, 2, dtype=F32) / RET_DK)
    freq = jnp.concatenate([inv_freq, inv_freq]).reshape(1, RET_DK)
    pos = jnp.broadcast_to(positions.astype(F32).reshape(n, 1), (n, RET_DK))
    tr = min(n, 2048)
    return pl.pallas_call(
        _rope_table_kernel,
        out_shape=(jax.ShapeDtypeStruct((n, RET_DK), F32),) * 2,
        grid=(n // tr,),
        in_specs=[pl.BlockSpec((tr, RET_DK), lambda i: (i, 0)),
                  pl.BlockSpec((1, RET_DK), lambda i: (0, 0))],
        out_specs=(pl.BlockSpec((tr, RET_DK), lambda i: (i, 0)),) * 2,
        compiler_params=_cparams(("parallel",)),
        name="rope_tables",
    )(pos, freq)


def _nmm_kernel(x_ref, g_ref, sh_ref, sc_ref, w_ref, *rest, rope_cols):
    if rope_cols:
        cs_ref, sn_ref, o_ref = rest
    else:
        (o_ref,) = rest
    cw = NMM_CHUNK
    n_chunks = w_ref.shape[1] // cw
    h = _rms_mod(x_ref[...], g_ref[...], sh_ref[0], sc_ref[0]).astype(BF16)

    def mm(c):
        return jnp.dot(h, w_ref[:, c * cw:(c + 1) * cw], preferred_element_type=F32)

    nxt = mm(0)
    for c in range(n_chunks):
        r = nxt
        if c + 1 < n_chunks:
            nxt = mm(c + 1)
        col0 = c * cw
        if col0 >= rope_cols:
            o_ref[:, col0:col0 + cw] = r.astype(o_ref.dtype)
            continue
        k_scale = RET_DK ** -0.5 if col0 >= rope_cols // 2 else None
        for s in range(cw // RET_DK):
            t = r[:, s * RET_DK:(s + 1) * RET_DK]
            rot = t * cs_ref[...] + pltpu.roll(t, RET_DK // 2, 1) * sn_ref[...]
            if k_scale is not None:
                rot = rot * k_scale
            o_ref[:, col0 + s * RET_DK:col0 + (s + 1) * RET_DK] = rot.astype(o_ref.dtype)


def _norm_mod_matmul(x, gain, shift, scale, w, seq, rope=None):
    n, d = x.shape
    m = w.shape[1]
    tm = min(NMM_ROWS if m > NMM_WIDE else ROW_TILE, seq)
    bsz = shift.shape[0]
    vec = lambda i: (i * tm // seq, 0, 0)
    in_specs = [
        pl.BlockSpec((tm, d), lambda i: (i, 0)),
        pl.BlockSpec((1, d), lambda i: (0, 0)),
        pl.BlockSpec((1, 1, d), vec),
        pl.BlockSpec((1, 1, d), vec),
        pl.BlockSpec((d, m), lambda i: (0, 0), pipeline_mode=pl.Buffered(1)),
    ]
    args = [x, gain.reshape(1, d), shift.reshape(bsz, 1, d), scale.reshape(bsz, 1, d), w]
    rope_cols = 0
    if rope is not None:
        rope_cols = 2 * RET_HEADS * RET_DK
        in_specs += [pl.BlockSpec((tm, RET_DK), lambda i: (i, 0))] * 2
        args += list(rope)
    return pl.pallas_call(
        functools.partial(_nmm_kernel, rope_cols=rope_cols),
        out_shape=jax.ShapeDtypeStruct((n, m), BF16),
        grid=(n // tm,),
        in_specs=in_specs,
        out_specs=pl.BlockSpec((tm, m), lambda i: (i, 0)),
        compiler_params=_cparams(("parallel",)),
        name="norm_mod_matmul_rope" if rope_cols else "norm_mod_matmul",
    )(*args)


def _swiglu(h, load_wa, load_wb, load_wo, width, emit):
    def up(c):
        sl = slice(c * MXU_COLS, (c + 1) * MXU_COLS)
        return (jnp.dot(h, load_wa(sl), preferred_element_type=F32),
                jnp.dot(h, load_wb(sl), preferred_element_type=F32))

    n_chunks = width // MXU_COLS
    acts = []
    nxt = up(0)
    for c in range(n_chunks):
        a, b = nxt
        if c + 1 < n_chunks:
            nxt = up(c + 1)
        acts.append((a * _sigmoid(a) * b).astype(BF16))
    act = jnp.concatenate(acts, axis=1)

    def down(c):
        return jnp.dot(act, load_wo(slice(c * MXU_COLS, (c + 1) * MXU_COLS)), preferred_element_type=F32)

    n_out = D_MODEL // MXU_COLS
    nxt = down(0)
    for c in range(n_out):
        y = nxt
        if c + 1 < n_out:
            nxt = down(c + 1)
        emit(slice(c * MXU_COLS, (c + 1) * MXU_COLS), y)


def _ffn_kernel(a_ref, wp_ref, gp_ref, x_ref, g_ref, sh_ref, sc_ref, gate_ref, wa_ref, wb_ref, wo_ref, o_ref,
                x1_ref):
    x1_ref[...] = x_ref[...] + gp_ref[0] * jnp.dot(a_ref[...], wp_ref[...], preferred_element_type=F32)
    h = _rms_mod(x1_ref[...], g_ref[...], sh_ref[0], sc_ref[0]).astype(BF16)

    def emit(cols, y):
        o_ref[:, cols] = x1_ref[:, cols] + gate_ref[0, :, cols] * y

    _swiglu(h, lambda sl: wa_ref[:, sl], lambda sl: wb_ref[:, sl], lambda sl: wo_ref[:, sl],
            wo_ref.shape[0], emit)


def _dense_ffn(a, w_proj, gate_proj, x, gain, shift, scale, gate, w_in, w_out, seq):
    n, d = x.shape
    k = a.shape[1]
    ff = w_out.shape[0]
    tm = min(DENSE_ROWS, seq)
    bsz = shift.shape[0]
    vec = lambda i: (i * tm // seq, 0, 0)
    resident = pl.Buffered(1)
    return pl.pallas_call(
        _ffn_kernel,
        out_shape=jax.ShapeDtypeStruct((n, d), F32),
        grid=(n // tm,),
        in_specs=[
            pl.BlockSpec((tm, k), lambda i: (i, 0)),
            pl.BlockSpec((k, d), lambda i: (0, 0), pipeline_mode=resident),
            pl.BlockSpec((1, 1, d), vec),
            pl.BlockSpec((tm, d), lambda i: (i, 0)),
            pl.BlockSpec((1, d), lambda i: (0, 0)),
            pl.BlockSpec((1, 1, d), vec),
            pl.BlockSpec((1, 1, d), vec),
            pl.BlockSpec((1, 1, d), vec),
            pl.BlockSpec((d, ff), lambda i: (0, 0), pipeline_mode=resident),
            pl.BlockSpec((d, ff), lambda i: (0, 1), pipeline_mode=resident),
            pl.BlockSpec((ff, d), lambda i: (0, 0), pipeline_mode=resident),
        ],
        out_specs=pl.BlockSpec((tm, d), lambda i: (i, 0)),
        scratch_shapes=[pltpu.VMEM((tm, d), F32)],
        compiler_params=_cparams(("parallel",)),
        name="dense_ffn",
    )(a, w_proj, gate_proj.reshape(bsz, 1, d), x, gain.reshape(1, d), shift.reshape(bsz, 1, d),
      scale.reshape(bsz, 1, d), gate.reshape(bsz, 1, d), w_in, w_in, w_out)


def _ret_kernel(q_ref, k_ref, v_ref, g_ref, gn_ref, dm_ref, qd_ref, kd_ref, cd_ref, o_ref, *, nblk):
    dmat = dm_ref[0]
    qdec = qd_ref[0]
    kdec = kd_ref[0]
    cdec = cd_ref[0, 0:1, :]
    gn = gn_ref[...]

    def load(t):
        sl = slice(t * RET_BLOCK, (t + 1) * RET_BLOCK)
        return q_ref[0, sl, :], k_ref[0, sl, :], v_ref[0, sl, :]

    def scores(q, k):
        return lax.dot_general(q, k, (((1,), (1,)), ((), ())), preferred_element_type=F32)

    state = jnp.zeros((RET_DK, RET_DV), F32)
    q, k, v = load(0)
    s = scores(q, k)
    for t in range(nblk):
        qd = (q.astype(F32) * qdec).astype(BF16)
        cross = jnp.dot(qd, state.astype(BF16), preferred_element_type=F32)
        kdt = jnp.transpose(k.astype(F32) * kdec).astype(BF16)
        update = jnp.dot(kdt, v, preferred_element_type=F32)
        if t + 1 < nblk:
            q, k, v_next = load(t + 1)
            s_next = scores(q, k)
        intra = jnp.dot((s * dmat).astype(BF16), v, preferred_element_type=F32)
        state = state * cdec + update
        o = intra + cross
        mu = jnp.mean(o, axis=-1, keepdims=True)
        oc = o - mu
        var = jnp.mean(oc * oc, axis=-1, keepdims=True)
        on = (oc * lax.rsqrt(var + EPS)) * gn
        sl = slice(t * RET_BLOCK, (t + 1) * RET_BLOCK)
        g = g_ref[0, sl, :].astype(F32)
        o_ref[0, sl, :] = ((g * _sigmoid(g)) * on).astype(o_ref.dtype)
        if t + 1 < nblk:
            v, s = v_next, s_next


def _retention_tables():
    t = RET_BLOCK
    log_gamma = jnp.log(1.0 - jnp.exp2(-5.0 - jnp.arange(RET_HEADS, dtype=F32)))
    pos = jnp.arange(t, dtype=F32)
    chunk = jnp.arange(t) // CHUNK
    dist = jnp.abs(pos[:, None] - pos[None, :])
    visible = (chunk[None, :] <= chunk[:, None]).astype(F32)
    dmat = jnp.exp(dist[None] * log_gamma[:, None, None]) * visible[None]
    qdec = jnp.exp((pos[None, :] + 1.0) * log_gamma[:, None])
    kdec = jnp.exp((t - 1.0 - pos)[None, :] * log_gamma[:, None])
    cdec = jnp.exp(t * log_gamma)
    qdec = jnp.broadcast_to(qdec[:, :, None], (RET_HEADS, t, RET_DK))
    kdec = jnp.broadcast_to(kdec[:, :, None], (RET_HEADS, t, RET_DK))
    cdec = jnp.broadcast_to(cdec[:, None, None], (RET_HEADS, 8, RET_DV))
    return dmat, qdec, kdec, cdec


def _retention_core(proj, gn_w, tables, bsz, seq):
    dmat, qdec, kdec, cdec = tables
    h = RET_HEADS
    t = RET_BLOCK
    kq = h * RET_DK // RET_DK
    kv = 2 * h * RET_DK // RET_DV
    return pl.pallas_call(
        functools.partial(_ret_kernel, nblk=seq // t),
        out_shape=jax.ShapeDtypeStruct((bsz, seq, h * RET_DV), BF16),
        grid=(bsz, h),
        in_specs=[
            pl.BlockSpec((1, seq, RET_DK), lambda b, i: (b, 0, i)),
            pl.BlockSpec((1, seq, RET_DK), lambda b, i: (b, 0, kq + i)),
            pl.BlockSpec((1, seq, RET_DV), lambda b, i: (b, 0, kv + i)),
            pl.BlockSpec((1, seq, RET_DV), lambda b, i: (b, 0, kv + h + i)),
            pl.BlockSpec((1, RET_DV), lambda b, i: (0, i)),
            pl.BlockSpec((1, t, t), lambda b, i: (i, 0, 0)),
            pl.BlockSpec((1, t, RET_DK), lambda b, i: (i, 0, 0)),
            pl.BlockSpec((1, t, RET_DK), lambda b, i: (i, 0, 0)),
            pl.BlockSpec((1, 8, RET_DV), lambda b, i: (i, 0, 0)),
        ],
        out_specs=pl.BlockSpec((1, seq, RET_DV), lambda b, i: (b, 0, i)),
        compiler_params=_cparams(("parallel", "parallel")),
        name="retention_core",
    )(proj, proj, proj, proj, gn_w.reshape(1, h * RET_DV), dmat, qdec, kdec, cdec)


def _sb_kernel(q_ref, k_ref, v_ref, o_ref, r_ref, acc_ref):
    t = SB_TILE
    w = 2 * SB_DH
    pairs = SB_HEADS // 2
    qi = pl.program_id(1)
    lane = lax.broadcasted_iota(jnp.int32, (t, w), 1)
    row = lax.broadcasted_iota(jnp.int32, (t, 2 * t), 0)
    col = lax.broadcasted_iota(jnp.int32, (t, 2 * t), 1)
    diag_causal = jnp.where(col >= t, col - t, col) < row
    uj = lax.broadcasted_iota(jnp.int32, (2 * t, 2 * t), 0)
    us = lax.broadcasted_iota(jnp.int32, (2 * t, 2 * t), 1)
    uj = jnp.where(uj >= t, uj - t, uj)
    umat = jnp.where((uj >= us) | (us >= t), -1.0, 0.0).astype(BF16)
    scale = jnp.asarray(SB_DH ** -0.5, BF16)
    qs = [q_ref[0, :, p * w:(p + 1) * w] * scale for p in range(pairs)]

    def split_heads(x):
        zero = jnp.zeros_like(x)
        return jnp.concatenate([jnp.where(lane < SB_DH, x, zero), jnp.where(lane >= SB_DH, x, zero)], axis=0)

    def key_tiles(kbs, causal, first, ps):
        spans = [pl.ds(pl.multiple_of(kb * t, t), t) for kb in kbs]
        tiles = [(j, p) for j in range(len(kbs)) for p in ps]
        zs, his, los, incls, tots = {}, {}, {}, {}, {}
        for j, p in tiles:
            kcat = split_heads(k_ref[0, spans[j], p * w:(p + 1) * w])
            zs[j, p] = lax.dot_general(qs[p], kcat, (((1,), (1,)), ((), ())), preferred_element_type=F32)
        for j, p in tiles:
            z = zs[j, p]
            sp = jnp.maximum(z, 0.0) + jnp.log(1.0 + jnp.exp(-jnp.abs(z)))
            if causal is not None and j == 0:
                sp = jnp.where(causal, sp, 0.0)
            hi = sp.astype(BF16)
            his[j, p] = hi
            los[j, p] = (sp - hi.astype(F32)).astype(BF16)
        for j, p in tiles:
            cs = [jnp.dot(jnp.concatenate([his[j, p][:, h * t:(h + 1) * t], los[j, p][:, h * t:(h + 1) * t]],
                                          axis=1), umat, preferred_element_type=F32) for h in range(2)]
            incls[j, p] = jnp.concatenate([cs[0][:, :t], cs[1][:, :t]], axis=1)
            tots[j, p] = jnp.concatenate([cs[0][:, t:], cs[1][:, t:]], axis=1)
        avs = {}
        rs = {p: None if first else r_ref[p] for p in ps}
        for j, p in tiles:
            e = zs[j, p] + incls[j, p]
            a = jnp.exp(e if rs[p] is None else e + rs[p])
            if causal is not None and j == 0:
                a = jnp.where(causal, a, 0.0)
            avs[j, p] = a.astype(BF16)
            rs[p] = tots[j, p] if rs[p] is None else rs[p] + tots[j, p]
        for p in ps:
            r_ref[p] = rs[p]
        accs = {p: None if first else acc_ref[p] for p in ps}
        for j, p in tiles:
            vcat = split_heads(v_ref[0, spans[j], p * w:(p + 1) * w])
            av = jnp.dot(avs[j, p], vcat, preferred_element_type=F32)
            accs[p] = av if accs[p] is None else accs[p] + av
        for p in ps:
            acc_ref[p] = accs[p]
        return tuple((jnp.max(rs[p]) > SB_LOG_FLOOR).astype(jnp.int32) for p in ps)

    everyone = list(range(pairs))
    left = key_tiles([qi], diag_causal, True, everyone)
    kb = qi - 1

    two = jnp.logical_and(kb >= 1, sum(left) > 0)
    left = lax.cond(two, lambda: key_tiles([kb, kb - 1], None, False, everyone), lambda: left)
    kb = jnp.where(two, kb - 2, kb)

    def cond(carry):
        kb, *left = carry
        return jnp.logical_and(kb >= 0, sum(left) > 0)

    def body(carry):
        kb, *left = carry

        def stragglers():
            return tuple(lax.cond(left[p] > 0, lambda p=p: key_tiles([kb], None, False, [p])[0],
                                  lambda: jnp.int32(0)) for p in everyone)

        left = lax.cond(sum(left) > pairs // 2, lambda: key_tiles([kb], None, False, everyone), stragglers)
        return (kb - 1, *left)

    lax.while_loop(cond, body, (kb, *left))
    for p in range(pairs):
        o_ref[0, :, p * w:(p + 1) * w] = acc_ref[p].astype(o_ref.dtype)


def _stick_breaking_core(q, kv, bsz, seq):
    t = SB_TILE
    return pl.pallas_call(
        _sb_kernel,
        out_shape=jax.ShapeDtypeStruct((bsz, seq, D_MODEL), BF16),
        grid=(bsz, seq // t),
        in_specs=[
            pl.BlockSpec((1, t, D_MODEL), lambda b, i: (b, i, 0)),
            pl.BlockSpec((1, seq, D_MODEL), lambda b, i: (b, 0, 0)),
            pl.BlockSpec((1, seq, D_MODEL), lambda b, i: (b, 0, 1)),
        ],
        out_specs=pl.BlockSpec((1, t, D_MODEL), lambda b, i: (b, i, 0)),
        scratch_shapes=[pltpu.VMEM((SB_HEADS // 2, t, 2 * t), F32),
                        pltpu.VMEM((SB_HEADS // 2, t, 2 * SB_DH), F32)],
        compiler_params=_cparams(("parallel", "parallel")),
        name="stick_breaking_core",
    )(q, kv, kv)


def _route_kernel(a_ref, wp_ref, gp_ref, x_ref, g_ref, sh_ref, sc_ref, r_ref,
                  x1_ref, h_ref, meta_ref, meta_t_ref, pc_ref):
    tb = MOE_BLOCK
    rows = x_ref.shape[0]
    strips = [slice(s, s + ROUTE_STRIP) for s in range(0, rows, ROUTE_STRIP)]
    ys = [jnp.dot(a_ref[rs, :], wp_ref[...], preferred_element_type=F32) for rs in strips]
    parts = []
    for rs, y in zip(strips, ys):
        x1 = x_ref[rs, :] + gp_ref[0] * y
        x1_ref[rs, :] = x1
        h32 = _rms_mod(x1, g_ref[...], sh_ref[0], sc_ref[0])
        h1 = h32.astype(BF16)
        h_ref[rs, :] = h1
        e1 = h32 - h1.astype(F32)
        h2 = e1.astype(BF16)
        parts.append((h1, h2, (e1 - h2.astype(F32)).astype(BF16)))
    r3 = r_ref[...]
    s3 = jnp.concatenate([jnp.dot(h1, r3, preferred_element_type=F32) + jnp.dot(h2, r3, preferred_element_type=F32)
                          + jnp.dot(h3, r3, preferred_element_type=F32) for h1, h2, h3 in parts], axis=0)
    logits = s3 + pltpu.roll(s3, LANES - N_EXPERTS, 1) + pltpu.roll(s3, LANES - 2 * N_EXPERTS, 1)
    lane = lax.broadcasted_iota(jnp.int32, (rows, LANES), 1)
    lane_f = lane.astype(F32)
    neg = jnp.float32(-jnp.inf)
    lg = jnp.where(lane < N_EXPERTS, logits, neg)
    m1 = jnp.max(lg, axis=-1, keepdims=True)
    i1 = jnp.min(jnp.where(lg == m1, lane_f, float(LANES)), axis=-1, keepdims=True)
    oh1 = lane_f == i1
    lg2 = jnp.where(oh1, neg, lg)
    m2 = jnp.max(lg2, axis=-1, keepdims=True)
    i2 = jnp.min(jnp.where(lg2 == m2, lane_f, float(LANES)), axis=-1, keepdims=True)
    oh2 = lane_f == i2
    e = jnp.exp(m2 - m1)
    w1 = 1.0 / (1.0 + e)
    w2 = e / (1.0 + e)

    tr = lax.broadcasted_iota(jnp.int32, (tb, tb), 0)
    tc = lax.broadcasted_iota(jnp.int32, (tb, tb), 1)
    lower = jnp.where(tc < tr, 1.0, 0.0).astype(BF16)
    pj = lax.broadcasted_iota(jnp.int32, (LANES, LANES), 0)
    ps = lax.broadcasted_iota(jnp.int32, (LANES, LANES), 1)
    before = jnp.where(pj < ps, 1.0, 0.0)
    oh1f = jnp.where(oh1, 1.0, 0.0)
    oh2f = jnp.where(oh2, 1.0, 0.0)
    blocks = [slice(b * tb, (b + 1) * tb) for b in range(rows // tb)]
    cum1 = [jnp.dot(lower, oh1f[rs].astype(BF16), preferred_element_type=F32) for rs in blocks]
    cum2 = [jnp.dot(lower, oh2f[rs].astype(BF16), preferred_element_type=F32) for rs in blocks]
    cnt1 = [jnp.sum(oh1f[rs], axis=0, keepdims=True) for rs in blocks]
    cnt2 = [jnp.sum(oh2f[rs], axis=0, keepdims=True) for rs in blocks]
    pc = [jnp.floor((c1 + c2 + (MOE_GROUP - 1.0)) / MOE_GROUP) * MOE_GROUP
          for c1, c2 in zip(cnt1, cnt2)]
    off = [jnp.dot(jnp.broadcast_to(c, (8, LANES)), before, preferred_element_type=F32,
                   precision=lax.Precision.HIGHEST)[0:1] for c in pc]
    pos1 = jnp.concatenate([jnp.sum(oh1f[rs] * (off[b] + cum1[b]), axis=-1, keepdims=True)
                            for b, rs in enumerate(blocks)], axis=0)
    pos2 = jnp.concatenate([jnp.sum(oh2f[rs] * (off[b] + cnt1[b] + cum2[b]), axis=-1, keepdims=True)
                            for b, rs in enumerate(blocks)], axis=0)
    meta = jnp.where(lane == 0, pos1,
                     jnp.where(lane == 1, pos2,
                               jnp.where(lane == 2, w1, jnp.where(lane == 3, w2, 0.0))))
    meta_ref[...] = meta
    for b, rs in enumerate(blocks):
        meta_t_ref[b] = jnp.transpose(meta[rs])[0:8]
        pc_ref[b] = jnp.broadcast_to(pc[b], (8, LANES))


def _moe_route(a, w_proj, gate_proj, x, gain, shift, scale, router, seq):
    n, d = x.shape
    k = a.shape[1]
    tb = MOE_BLOCK
    nb = n // tb
    rb = min(ROUTE_BLOCKS, seq // tb)
    rows = rb * tb
    bsz = shift.shape[0]
    r1 = router.astype(BF16)
    r2 = (router - r1.astype(F32)).astype(BF16)
    r3 = (router - r1.astype(F32) - r2.astype(F32)).astype(BF16)
    r_pad = jnp.zeros((d, LANES), BF16).at[:, :3 * N_EXPERTS].set(jnp.concatenate([r1, r2, r3], axis=1))
    vec = lambda i: (i * rows // seq, 0, 0)
    return pl.pallas_call(
        _route_kernel,
        out_shape=(
            jax.ShapeDtypeStruct((n, d), F32),
            jax.ShapeDtypeStruct((n, d), BF16),
            jax.ShapeDtypeStruct((n, LANES), F32),
            jax.ShapeDtypeStruct((nb, 8, tb), F32),
            jax.ShapeDtypeStruct((nb, 8, LANES), F32),
        ),
        grid=(nb // rb,),
        in_specs=[
            pl.BlockSpec((rows, k), lambda i: (i, 0)),
            pl.BlockSpec((k, d), lambda i: (0, 0), pipeline_mode=pl.Buffered(1)),
            pl.BlockSpec((1, 1, d), vec),
            pl.BlockSpec((rows, d), lambda i: (i, 0)),
            pl.BlockSpec((1, d), lambda i: (0, 0)),
            pl.BlockSpec((1, 1, d), vec),
            pl.BlockSpec((1, 1, d), vec),
            pl.BlockSpec((d, LANES), lambda i: (0, 0)),
        ],
        out_specs=(
            pl.BlockSpec((rows, d), lambda i: (i, 0)),
            pl.BlockSpec((rows, d), lambda i: (i, 0)),
            pl.BlockSpec((rows, LANES), lambda i: (i, 0)),
            pl.BlockSpec((rb, 8, tb), lambda i: (i, 0, 0)),
            pl.BlockSpec((rb, 8, LANES), lambda i: (i, 0, 0)),
        ),
        compiler_params=_cparams(("parallel",)),
        name="moe_route",
    )(a, w_proj, gate_proj.reshape(bsz, 1, d), x, gain.reshape(1, d), shift.reshape(bsz, 1, d),
      scale.reshape(bsz, 1, d), r_pad)


def _segment_copies(loc_ref, glb_ref, len_ref, blk, make_copy, wait):
    for e in range(N_EXPERTS):
        idx = blk * N_EXPERTS + e
        loc = loc_ref[idx]
        glb = glb_ref[idx]
        length = len_ref[idx]
        done = jnp.int32(0)
        for sz in SEG_SIZES:
            take = (length & sz) != 0

            @pl.when(take)
            def _(loc=loc, glb=glb, done=done, sz=sz):
                cp = make_copy(pl.multiple_of(loc + done, MOE_GROUP),
                               pl.multiple_of(glb + done, MOE_GROUP), sz)
                if wait:
                    cp.wait()
                else:
                    cp.start()

            done = done + jnp.where(take, sz, 0)


def _dispatch_kernel(loc_ref, glb_ref, len_ref, h_ref, meta_t_ref, hs_ref, buf_ref, zero_ref, sem):
    b = pl.program_id(0)
    tb = MOE_BLOCK
    pos1 = meta_t_ref[0, 0:1, :]
    pos2 = meta_t_ref[0, 1:2, :]
    r = lax.broadcasted_iota(jnp.int32, (MOE_CAP, tb), 0).astype(F32)
    perm = jnp.where((r == pos1) | (r == pos2), 1.0, 0.0).astype(BF16)
    sorted_rows = jnp.dot(perm, h_ref[...], preferred_element_type=F32).astype(BF16)

    def make_copy(loc, glb, sz):
        return pltpu.make_async_copy(buf_ref.at[pl.ds(loc, sz)], hs_ref.at[pl.ds(glb, sz)], sem)

    @pl.when(b > 0)
    def _():
        _segment_copies(loc_ref, glb_ref, len_ref, b - 1, make_copy, wait=True)

    buf_ref[...] = sorted_rows
    _segment_copies(loc_ref, glb_ref, len_ref, b, make_copy, wait=False)

    @pl.when(b == pl.num_programs(0) - 1)
    def _():
        def make_zero_copy(_, glb, sz):
            return pltpu.make_async_copy(zero_ref.at[pl.ds(0, sz)], hs_ref.at[pl.ds(glb, sz)], sem)

        zero_ref[...] = jnp.zeros_like(zero_ref)
        gaps = pl.num_programs(0)
        _segment_copies(loc_ref, glb_ref, len_ref, gaps, make_zero_copy, wait=False)

        tail = (gaps + 1) * N_EXPERTS
        pieces = len_ref[tail]

        def tail_copy(i):
            row = pl.multiple_of(glb_ref[tail] + i * zero_ref.shape[0], zero_ref.shape[0])
            return pltpu.make_async_copy(zero_ref, hs_ref.at[pl.ds(row, zero_ref.shape[0])], sem)

        @pl.loop(0, pieces)
        def _(i):
            tail_copy(i).start()

        _segment_copies(loc_ref, glb_ref, len_ref, b, make_copy, wait=True)
        _segment_copies(loc_ref, glb_ref, len_ref, gaps, make_zero_copy, wait=True)

        @pl.loop(0, pieces)
        def _(i):
            tail_copy(i).wait()


def _moe_dispatch(h, meta_t, loc, glb, seg_len, rows_total):
    n, d = h.shape
    tb = MOE_BLOCK
    nb = n // tb
    return pl.pallas_call(
        _dispatch_kernel,
        out_shape=jax.ShapeDtypeStruct((rows_total, d), BF16),
        grid_spec=pltpu.PrefetchScalarGridSpec(
            num_scalar_prefetch=3,
            grid=(nb,),
            in_specs=[
                pl.BlockSpec((tb, d), lambda i, *_: (i, 0)),
                pl.BlockSpec((1, 8, tb), lambda i, *_: (i, 0, 0)),
            ],
            out_specs=pl.BlockSpec(memory_space=pl.ANY),
            scratch_shapes=[pltpu.VMEM((MOE_CAP, d), BF16), pltpu.VMEM((SEG_SIZES[0], d), BF16),
                            pltpu.SemaphoreType.DMA],
        ),
        compiler_params=_cparams(("arbitrary",)),
        name="moe_dispatch",
    )(loc, glb, seg_len, h, meta_t)


def _expert_kernel(te_ref, tx_ref, tv_ref, x_ref, wa_ref, wb_ref, wo_ref, o_ref, acc_ref):
    i = pl.program_id(0)
    f = pl.program_id(1)

    last = pl.num_programs(1) - 1
    valid = tv_ref[i] == 1

    def run(emit):
        _swiglu(x_ref[...], lambda sl: wa_ref[0, :, sl], lambda sl: wb_ref[0, :, sl],
                lambda sl: wo_ref[0, :, sl].astype(BF16), wo_ref.shape[1], emit)

    @pl.when(jnp.logical_and(valid, f == 0))
    def _():
        def emit(cols, y):
            acc_ref[:, cols] = y
        run(emit)

    if FF_EXPERT // FF_EXPERT_TILE > 2:
        @pl.when(jnp.logical_and(valid, jnp.logical_and(f > 0, f < last)))
        def _():
            def emit(cols, y):
                acc_ref[:, cols] += y
            run(emit)

    @pl.when(jnp.logical_and(valid, f == last))
    def _():
        def emit(cols, y):
            o_ref[:, cols] = (acc_ref[:, cols] + y).astype(o_ref.dtype)
        run(emit)

    @pl.when(jnp.logical_and(tv_ref[i] == 0, f == pl.num_programs(1) - 1))
    def _():
        o_ref[...] = jnp.zeros_like(o_ref)


def _moe_experts(hs, w_in, w_out, tile_expert, tile_row, tile_valid):
    rows, d = hs.shape
    tm = ROW_TILE
    tf = FF_EXPERT_TILE
    nf = FF_EXPERT // tf
    nt = rows // tm

    def f_eff(i, f, tv):
        return jnp.where(tv[i] == 1, f, nf - 1)

    return pl.pallas_call(
        _expert_kernel,
        out_shape=jax.ShapeDtypeStruct((rows, d), BF16),
        grid_spec=pltpu.PrefetchScalarGridSpec(
            num_scalar_prefetch=3,
            grid=(nt, nf),
            in_specs=[
                pl.BlockSpec((tm, d), lambda i, f, te, tx, tv: (tx[i], 0)),
                pl.BlockSpec((1, d, tf), lambda i, f, te, tx, tv: (te[i], 0, f_eff(i, f, tv))),
                pl.BlockSpec((1, d, tf), lambda i, f, te, tx, tv: (te[i], 0, nf + f_eff(i, f, tv))),
                pl.BlockSpec((1, tf, d), lambda i, f, te, tx, tv: (te[i], f_eff(i, f, tv), 0)),
            ],
            out_specs=pl.BlockSpec((tm, d), lambda i, f, te, tx, tv: (i, 0)),
            scratch_shapes=[pltpu.VMEM((tm, d), F32)],
        ),
        compiler_params=_cparams(("arbitrary", "arbitrary")),
        name="moe_experts",
    )(tile_expert, tile_row, tile_valid, hs, w_in, w_in, w_out)


def _combine_kernel(loc_ref, glb_ref, len_ref, x_ref, gate_ref, meta_ref, fg_ref, ys_ref, o_ref, buf_ref, sems,
                    *, final_norm):
    b = pl.program_id(0)
    tb = MOE_BLOCK

    def fetch(blk, slot, wait):
        def make_copy(loc, glb, sz):
            return pltpu.make_async_copy(ys_ref.at[pl.ds(glb, sz)], buf_ref.at[slot, pl.ds(loc, sz)],
                                         sems.at[slot])

        _segment_copies(loc_ref, glb_ref, len_ref, blk, make_copy, wait)

    def start(blk, slot):
        buf_ref[slot] = jnp.zeros(buf_ref.shape[1:], buf_ref.dtype)
        fetch(blk, slot, wait=False)

    @pl.when(b == 0)
    def _():
        start(0, 0)

    @pl.when(b + 1 < pl.num_programs(0))
    def _():
        start(b + 1, (b + 1) % 2)

    slot = b % 2
    meta = meta_ref[...]
    pos1 = meta[:, 0:1]
    pos2 = meta[:, 1:2]
    w1 = meta[:, 2:3]
    w2 = meta[:, 3:4]
    r = lax.broadcasted_iota(jnp.int32, (tb, MOE_CAP), 1).astype(F32)
    sel1 = jnp.where(r == pos1, 1.0, 0.0).astype(BF16)
    sel2 = jnp.where(r == pos2, 1.0, 0.0).astype(BF16)
    fetch(b, slot, wait=True)
    ys = buf_ref[slot]
    y = (w1 * jnp.dot(sel1, ys, preferred_element_type=F32)
         + w2 * jnp.dot(sel2, ys, preferred_element_type=F32))
    x = x_ref[...] + gate_ref[0] * y
    if final_norm:
        x = (x * lax.rsqrt(jnp.mean(x * x, axis=-1, keepdims=True) + EPS)) * fg_ref[...]
    o_ref[...] = x


def _moe_combine(x, gate, meta, ys, loc, glb, seg_len, seq, final_gain=None):
    n, d = x.shape
    tb = MOE_BLOCK
    bsz = gate.shape[0]
    fg = jnp.ones((1, d), F32) if final_gain is None else final_gain.reshape(1, d)
    return pl.pallas_call(
        functools.partial(_combine_kernel, final_norm=final_gain is not None),
        out_shape=jax.ShapeDtypeStruct((n, d), F32),
        grid_spec=pltpu.PrefetchScalarGridSpec(
            num_scalar_prefetch=3,
            grid=(n // tb,),
            in_specs=[
                pl.BlockSpec((tb, d), lambda i, *_: (i, 0)),
                pl.BlockSpec((1, 1, d), lambda i, *_: (i * tb // seq, 0, 0)),
                pl.BlockSpec((tb, LANES), lambda i, *_: (i, 0)),
                pl.BlockSpec((1, d), lambda i, *_: (0, 0)),
                pl.BlockSpec(memory_space=pl.ANY),
            ],
            out_specs=pl.BlockSpec((tb, d), lambda i, *_: (i, 0)),
            scratch_shapes=[pltpu.VMEM((2, MOE_CAP, d), BF16), pltpu.SemaphoreType.DMA((2,))],
        ),
        compiler_params=_cparams(("arbitrary",)),
        name="moe_combine",
    )(loc, glb, seg_len, x, gate.reshape(bsz, 1, d), meta, fg, ys)


def _moe_layout(pc):
    nb = pc.shape[0]
    tm = ROW_TILE
    n_tiles = (nb * (2 * MOE_BLOCK + N_EXPERTS * (MOE_GROUP - 1)) + tm - 1) // tm + N_EXPERTS
    loc = jnp.cumsum(pc, axis=1) - pc
    tiles_e = (jnp.sum(pc, axis=0) + tm - 1) // tm
    tile_end = jnp.cumsum(tiles_e)
    base_e = (tile_end - tiles_e) * tm
    glb = base_e[None, :] + jnp.cumsum(pc, axis=0) - pc
    n_valid = tile_end[-1]
    ids = jnp.arange(n_tiles, dtype=jnp.int32)
    tile_row = jnp.maximum(jnp.minimum(ids, n_valid - 1), 0)
    tile_expert = jnp.sum((tile_row[:, None] >= tile_end[None, :]).astype(jnp.int32), axis=1)
    tile_valid = (ids < n_valid).astype(jnp.int32)
    rows_e = jnp.sum(pc, axis=0)
    first = jnp.zeros_like(rows_e).at[0].set(1)
    loc = jnp.concatenate([loc, jnp.zeros_like(rows_e)[None], jnp.zeros_like(rows_e)[None]])
    glb = jnp.concatenate([glb, (base_e + rows_e)[None], (first * n_valid * tm)[None]])
    seg = jnp.concatenate([pc, (tiles_e * tm - rows_e)[None],
                           (first * (n_tiles - n_valid) * (tm // SEG_SIZES[0]))[None]])
    flat = lambda a: a.reshape(-1).astype(jnp.int32)
    return flat(loc), flat(glb), flat(seg), tile_expert.astype(jnp.int32), tile_row.astype(jnp.int32), tile_valid, n_tiles * tm


def _moe_block(a, w_proj, gate_proj, x, gain, shift, scale, gate, router, w_in, w_out, layer, seq,
               final_gain=None):
    x, h, meta, meta_t, pc = _moe_route(a, w_proj, gate_proj, x, gain, shift, scale, router, seq)
    pc_i = pc[:, 0, :N_EXPERTS].astype(jnp.int32)
    loc, glb, seg_len, tile_expert, tile_row, tile_valid, rows_total = _moe_layout(pc_i)
    hs = _moe_dispatch(h, meta_t, loc, glb, seg_len, rows_total)
    ys = _moe_experts(hs, w_in, w_out, tile_expert + layer * N_EXPERTS, tile_row, tile_valid)
    return _moe_combine(x, gate, meta, ys, loc, glb, seg_len, seq, final_gain)


def _final_norm_kernel(x_ref, g_ref, o_ref):
    x = x_ref[...]
    o_ref[...] = (x * lax.rsqrt(jnp.mean(x * x, axis=-1, keepdims=True) + EPS)) * g_ref[...]


def _final_norm(x, gain):
    n, d = x.shape
    tm = min(ROW_TILE, n)
    return pl.pallas_call(
        _final_norm_kernel,
        out_shape=jax.ShapeDtypeStruct((n, d), F32),
        grid=(n // tm,),
        in_specs=[pl.BlockSpec((tm, d), lambda i: (i, 0)), pl.BlockSpec((1, d), lambda i: (0, 0))],
        out_specs=pl.BlockSpec((tm, d), lambda i: (i, 0)),
        compiler_params=_cparams(("parallel",)),
        name="final_norm",
    )(x, gain.reshape(1, d))


def kernel(x, c, positions, ada_w, ada_b, norm_mix, norm_ff, ret_w_in, ret_gn, ret_w_out, kv_ada_w, kv_ada_b, kv_norm, kv_w, sb_w_q, sb_w_out, ff_w_in, ff_w_out, moe_router, moe_w_in, moe_w_out, final_norm):
    bsz, seq, d = x.shape
    n = bsz * seq
    mod = _ada_mod(c, ada_w, ada_b)
    kv_mod = _ada_mod(c, kv_ada_w[None], kv_ada_b[None])[0]
    rope = _rope_tables(positions)
    ret_tables = _retention_tables()
    moe_in = moe_w_in.astype(BF16).reshape(-1, d, 2 * FF_EXPERT)
    moe_out = moe_w_out.reshape(-1, FF_EXPERT, d)
    xf = x.reshape(n, d)
    kv = None
    for i in range(DEPTH):
        sh_m, sc_m, g_m, sh_f, sc_f, g_f = [mod[i, :, p * d:(p + 1) * d] for p in range(6)]
        if i < N_A:
            proj = _norm_mod_matmul(xf, norm_mix[i], sh_m, sc_m, ret_w_in[i].astype(BF16), seq, rope=rope)
            o = _retention_core(proj.reshape(bsz, seq, -1), ret_gn[i], ret_tables, bsz, seq)
            o, w_o = o.reshape(n, -1), ret_w_out[i].astype(BF16)
        else:
            j = i - N_A
            q = _norm_mod_matmul(xf, norm_mix[i], sh_m, sc_m, sb_w_q[j].astype(BF16), seq)
            o = _stick_breaking_core(q.reshape(bsz, seq, d), kv, bsz, seq)
            o, w_o = o.reshape(n, d), sb_w_out[j].astype(BF16)
        if i % 2 == 0:
            xf = _dense_ffn(o, w_o, g_m, xf, norm_ff[i], sh_f, sc_f, g_f, ff_w_in[i // 2].astype(BF16),
                            ff_w_out[i // 2].astype(BF16), seq)
        else:
            xf = _moe_block(o, w_o, g_m, xf, norm_ff[i], sh_f, sc_f, g_f, moe_router[i // 2],
                            moe_in, moe_out, i // 2, seq,
                            final_gain=final_norm if i == DEPTH - 1 else None)
        if i == N_A - 1:
            kv = _norm_mod_matmul(xf, kv_norm, kv_mod[:, :d], kv_mod[:, d:], kv_w.astype(BF16), seq)
            kv = kv.reshape(bsz, seq, 2 * d)
    if (DEPTH - 1) % 2 == 0:
        xf = _final_norm(xf, final_norm)
    return xf.reshape(bsz, seq, d)
```

```python
import functools

import jax
import jax.numpy as jnp
from jax import lax
from jax.experimental import pallas as pl
from jax.experimental.pallas import tpu as pltpu

F32 = jnp.float32
BF16 = jnp.bfloat16

D_MODEL = 1024
DEPTH = 4
N_A = DEPTH // 2
CHUNK = 64
RET_HEADS = 8
RET_DK = D_MODEL // RET_HEADS
RET_DV = 2 * D_MODEL // RET_HEADS
ROPE_BASE = 10000.0
SB_HEADS = 16
SB_DH = D_MODEL // SB_HEADS
FF_DENSE = 2816
N_EXPERTS = 8
FF_EXPERT = 3584
EPS = 1e-6

LANES = 128
BF16_ROWS = 16
MXU_COLS = 256
VMEM_LIMIT = 56 * 1024 * 1024

ROW_TILE = 1024
NMM_ROWS = 512
NMM_WIDE = 2048
NMM_CHUNK = 512
DENSE_ROWS = 512
FF_EXPERT_TILE = 1792
RET_BLOCK = 256
SB_TILE = 128
MOE_BLOCK = 512
ROUTE_BLOCKS = 2
ROUTE_STRIP = 256
MOE_GROUP = BF16_ROWS
MOE_CAP = 2 * MOE_BLOCK + N_EXPERTS * MOE_GROUP
SEG_SIZES = (512, 256, 128, 64, 32, 16)
SB_LOG_FLOOR = -110.0


def _cparams(sem):
    return pltpu.CompilerParams(dimension_semantics=sem, vmem_limit_bytes=VMEM_LIMIT)


def _sigmoid(v):
    return 1.0 / (1.0 + jnp.exp(-v))


def _rms_mod(x, gain, shift, scale):
    y = x * lax.rsqrt(jnp.mean(x * x, axis=-1, keepdims=True) + EPS)
    return (y * gain) * (1.0 + scale) + shift


def _ada_kernel(c_ref, w_ref, b_ref, o_ref):
    c = c_ref[...]
    cond = (c * _sigmoid(c)).astype(BF16)
    o_ref[0] = jnp.dot(cond, w_ref[0].astype(BF16), preferred_element_type=F32) + b_ref[0]


def _ada_mod(c, w, b):
    nl, d, n = w.shape
    bsz = c.shape[0]
    tn = 1024
    return pl.pallas_call(
        _ada_kernel,
        out_shape=jax.ShapeDtypeStruct((nl, bsz, n), F32),
        grid=(nl, n // tn),
        in_specs=[
            pl.BlockSpec((bsz, d), lambda l, j: (0, 0)),
            pl.BlockSpec((1, d, tn), lambda l, j: (l, 0, j)),
            pl.BlockSpec((1, 1, tn), lambda l, j: (l, 0, j)),
        ],
        out_specs=pl.BlockSpec((1, bsz, tn), lambda l, j: (l, 0, j)),
        compiler_params=_cparams(("parallel", "parallel")),
        name="ada_mod",
    )(c, w, b.reshape(nl, 1, n))


def _rope_table_kernel(pos_ref, freq_ref, cs_ref, sn_ref):
    ang = pos_ref[...] * freq_ref[...]
    lane = lax.broadcasted_iota(jnp.int32, ang.shape, 1)
    cs_ref[...] = jnp.cos(ang)
    sn_ref[...] = jnp.where(lane < RET_DK // 2, -jnp.sin(ang), jnp.sin(ang))


def _rope_tables(positions):
    n = positions.size
    inv_freq = jnp.power(ROPE_BASE, -jnp.arange(0, RET_DK, 2, dtype=F32) / RET_DK)
    freq = jnp.concatenate([inv_freq, inv_freq]).reshape(1, RET_DK)
    pos = jnp.broadcast_to(positions.astype(F32).reshape(n, 1), (n, RET_DK))
    tr = min(n, 2048)
    return pl.pallas_call(
        _rope_table_kernel,
        out_shape=(jax.ShapeDtypeStruct((n, RET_DK), F32),) * 2,
        grid=(n // tr,),
        in_specs=[pl.BlockSpec((tr, RET_DK), lambda i: (i, 0)),
                  pl.BlockSpec((1, RET_DK), lambda i: (0, 0))],
        out_specs=(pl.BlockSpec((tr, RET_DK), lambda i: (i, 0)),) * 2,
        compiler_params=_cparams(("parallel",)),
        name="rope_tables",
    )(pos, freq)


def _nmm_kernel(x_ref, g_ref, sh_ref, sc_ref, w_ref, *rest, rope_cols):
    if rope_cols:
        cs_ref, sn_ref, o_ref = rest
    else:
        (o_ref,) = rest
    cw = NMM_CHUNK
    n_chunks = w_ref.shape[1] // cw
    h = _rms_mod(x_ref[...], g_ref[...], sh_ref[0], sc_ref[0]).astype(BF16)

    def mm(c):
        return jnp.dot(h, w_ref[:, c * cw:(c + 1) * cw], preferred_element_type=F32)

    nxt = mm(0)
    for c in range(n_chunks):
        r = nxt
        if c + 1 < n_chunks:
            nxt = mm(c + 1)
        col0 = c * cw
        if col0 >= rope_cols:
            o_ref[:, col0:col0 + cw] = r.astype(o_ref.dtype)
            continue
        k_scale = RET_DK ** -0.5 if col0 >= rope_cols // 2 else None
        for s in range(cw // RET_DK):
            t = r[:, s * RET_DK:(s + 1) * RET_DK]
            rot = t * cs_ref[...] + pltpu.roll(t, RET_DK // 2, 1) * sn_ref[...]
            if k_scale is not None:
                rot = rot * k_scale
            o_ref[:, col0 + s * RET_DK:col0 + (s + 1) * RET_DK] = rot.astype(o_ref.dtype)


def _norm_mod_matmul(x, gain, shift, scale, w, seq, rope=None):
    n, d = x.shape
    m = w.shape[1]
    tm = min(NMM_ROWS if m > NMM_WIDE else ROW_TILE, seq)
    bsz = shift.shape[0]
    vec = lambda i: (i * tm // seq, 0, 0)
    in_specs = [
        pl.BlockSpec((tm, d), lambda i: (i, 0)),
        pl.BlockSpec((1, d), lambda i: (0, 0)),
        pl.BlockSpec((1, 1, d), vec),
        pl.BlockSpec((1, 1, d), vec),
        pl.BlockSpec((d, m), lambda i: (0, 0), pipeline_mode=pl.Buffered(1)),
    ]
    args = [x, gain.reshape(1, d), shift.reshape(bsz, 1, d), scale.reshape(bsz, 1, d), w]
    rope_cols = 0
    if rope is not None:
        rope_cols = 2 * RET_HEADS * RET_DK
        in_specs += [pl.BlockSpec((tm, RET_DK), lambda i: (i, 0))] * 2
        args += list(rope)
    return pl.pallas_call(
        functools.partial(_nmm_kernel, rope_cols=rope_cols),
        out_shape=jax.ShapeDtypeStruct((n, m), BF16),
        grid=(n // tm,),
        in_specs=in_specs,
        out_specs=pl.BlockSpec((tm, m), lambda i: (i, 0)),
        compiler_params=_cparams(("parallel",)),
        name="norm_mod_matmul_rope" if rope_cols else "norm_mod_matmul",
    )(*args)


def _swiglu(h, load_wa, load_wb, load_wo, width, emit):
    def up(c):
        sl = slice(c * MXU_COLS, (c + 1) * MXU_COLS)
        return (jnp.dot(h, load_wa(sl), preferred_element_type=F32),
                jnp.dot(h, load_wb(sl), preferred_element_type=F32))

    n_chunks = width // MXU_COLS
    acts = []
    nxt = up(0)
    for c in range(n_chunks):
        a, b = nxt
        if c + 1 < n_chunks:
            nxt = up(c + 1)
        acts.append((a * _sigmoid(a) * b).astype(BF16))
    act = jnp.concatenate(acts, axis=1)

    def down(c):
        return jnp.dot(act, load_wo(slice(c * MXU_COLS, (c + 1) * MXU_COLS)), preferred_element_type=F32)

    n_out = D_MODEL // MXU_COLS
    nxt = down(0)
    for c in range(n_out):
        y = nxt
        if c + 1 < n_out:
            nxt = down(c + 1)
        emit(slice(c * MXU_COLS, (c + 1) * MXU_COLS), y)


def _ffn_kernel(a_ref, wp_ref, gp_ref, x_ref, g_ref, sh_ref, sc_ref, gate_ref, wa_ref, wb_ref, wo_ref, o_ref,
                x1_ref):
    x1_ref[...] = x_ref[...] + gp_ref[0] * jnp.dot(a_ref[...], wp_ref[...], preferred_element_type=F32)
    h = _rms_mod(x1_ref[...], g_ref[...], sh_ref[0], sc_ref[0]).astype(BF16)

    def emit(cols, y):
        o_ref[:, cols] = x1_ref[:, cols] + gate_ref[0, :, cols] * y

    _swiglu(h, lambda sl: wa_ref[:, sl], lambda sl: wb_ref[:, sl], lambda sl: wo_ref[:, sl],
            wo_ref.shape[0], emit)


def _dense_ffn(a, w_proj, gate_proj, x, gain, shift, scale, gate, w_in, w_out, seq):
    n, d = x.shape
    k = a.shape[1]
    ff = w_out.shape[0]
    tm = min(DENSE_ROWS, seq)
    bsz = shift.shape[0]
    vec = lambda i: (i * tm // seq, 0, 0)
    resident = pl.Buffered(1)
    return pl.pallas_call(
        _ffn_kernel,
        out_shape=jax.ShapeDtypeStruct((n, d), F32),
        grid=(n // tm,),
        in_specs=[
            pl.BlockSpec((tm, k), lambda i: (i, 0)),
            pl.BlockSpec((k, d), lambda i: (0, 0), pipeline_mode=resident),
            pl.BlockSpec((1, 1, d), vec),
            pl.BlockSpec((tm, d), lambda i: (i, 0)),
            pl.BlockSpec((1, d), lambda i: (0, 0)),
            pl.BlockSpec((1, 1, d), vec),
            pl.BlockSpec((1, 1, d), vec),
            pl.BlockSpec((1, 1, d), vec),
            pl.BlockSpec((d, ff), lambda i: (0, 0), pipeline_mode=resident),
            pl.BlockSpec((d, ff), lambda i: (0, 1), pipeline_mode=resident),
            pl.BlockSpec((ff, d), lambda i: (0, 0), pipeline_mode=resident),
        ],
        out_specs=pl.BlockSpec((tm, d), lambda i: (i, 0)),
        scratch_shapes=[pltpu.VMEM((tm, d), F32)],
        compiler_params=_cparams(("parallel",)),
        name="dense_ffn",
    )(a, w_proj, gate_proj.reshape(bsz, 1, d), x, gain.reshape(1, d), shift.reshape(bsz, 1, d),
      scale.reshape(bsz, 1, d), gate.reshape(bsz, 1, d), w_in, w_in, w_out)


def _ret_kernel(q_ref, k_ref, v_ref, g_ref, gn_ref, dm_ref, qd_ref, kd_ref, cd_ref, o_ref, *, nblk):
    dmat = dm_ref[0]
    qdec = qd_ref[0]
    kdec = kd_ref[0]
    cdec = cd_ref[0, 0:1, :]
    gn = gn_ref[...]

    def load(t):
        sl = slice(t * RET_BLOCK, (t + 1) * RET_BLOCK)
        return q_ref[0, sl, :], k_ref[0, sl, :], v_ref[0, sl, :]

    def scores(q, k):
        return lax.dot_general(q, k, (((1,), (1,)), ((), ())), preferred_element_type=F32)

    state = jnp.zeros((RET_DK, RET_DV), F32)
    q, k, v = load(0)
    s = scores(q, k)
    for t in range(nblk):
        qd = (q.astype(F32) * qdec).astype(BF16)
        cross = jnp.dot(qd, state.astype(BF16), preferred_element_type=F32)
        kdt = jnp.transpose(k.astype(F32) * kdec).astype(BF16)
        update = jnp.dot(kdt, v, preferred_element_type=F32)
        if t + 1 < nblk:
            q, k, v_next = load(t + 1)
            s_next = scores(q, k)
        intra = jnp.dot((s * dmat).astype(BF16), v, preferred_element_type=F32)
        state = state * cdec + update
        o = intra + cross
        mu = jnp.mean(o, axis=-1, keepdims=True)
        oc = o - mu
        var = jnp.mean(oc * oc, axis=-1, keepdims=True)
        on = (oc * lax.rsqrt(var + EPS)) * gn
        sl = slice(t * RET_BLOCK, (t + 1) * RET_BLOCK)
        g = g_ref[0, sl, :].astype(F32)
        o_ref[0, sl, :] = ((g * _sigmoid(g)) * on).astype(o_ref.dtype)
        if t + 1 < nblk:
            v, s = v_next, s_next


def _retention_tables():
    t = RET_BLOCK
    log_gamma = jnp.log(1.0 - jnp.exp2(-5.0 - jnp.arange(RET_HEADS, dtype=F32)))
    pos = jnp.arange(t, dtype=F32)
    chunk = jnp.arange(t) // CHUNK
    dist = jnp.abs(pos[:, None] - pos[None, :])
    visible = (chunk[None, :] <= chunk[:, None]).astype(F32)
    dmat = jnp.exp(dist[None] * log_gamma[:, None, None]) * visible[None]
    qdec = jnp.exp((pos[None, :] + 1.0) * log_gamma[:, None])
    kdec = jnp.exp((t - 1.0 - pos)[None, :] * log_gamma[:, None])
    cdec = jnp.exp(t * log_gamma)
    qdec = jnp.broadcast_to(qdec[:, :, None], (RET_HEADS, t, RET_DK))
    kdec = jnp.broadcast_to(kdec[:, :, None], (RET_HEADS, t, RET_DK))
    cdec = jnp.broadcast_to(cdec[:, None, None], (RET_HEADS, 8, RET_DV))
    return dmat, qdec, kdec, cdec


def _retention_core(proj, gn_w, tables, bsz, seq):
    dmat, qdec, kdec, cdec = tables
    h = RET_HEADS
    t = RET_BLOCK
    kq = h * RET_DK // RET_DK
    kv = 2 * h * RET_DK // RET_DV
    return pl.pallas_call(
        functools.partial(_ret_kernel, nblk=seq // t),
        out_shape=jax.ShapeDtypeStruct((bsz, seq, h * RET_DV), BF16),
        grid=(bsz, h),
        in_specs=[
            pl.BlockSpec((1, seq, RET_DK), lambda b, i: (b, 0, i)),
            pl.BlockSpec((1, seq, RET_DK), lambda b, i: (b, 0, kq + i)),
            pl.BlockSpec((1, seq, RET_DV), lambda b, i: (b, 0, kv + i)),
            pl.BlockSpec((1, seq, RET_DV), lambda b, i: (b, 0, kv + h + i)),
            pl.BlockSpec((1, RET_DV), lambda b, i: (0, i)),
            pl.BlockSpec((1, t, t), lambda b, i: (i, 0, 0)),
            pl.BlockSpec((1, t, RET_DK), lambda b, i: (i, 0, 0)),
            pl.BlockSpec((1, t, RET_DK), lambda b, i: (i, 0, 0)),
            pl.BlockSpec((1, 8, RET_DV), lambda b, i: (i, 0, 0)),
        ],
        out_specs=pl.BlockSpec((1, seq, RET_DV), lambda b, i: (b, 0, i)),
        compiler_params=_cparams(("parallel", "parallel")),
        name="retention_core",
    )(proj, proj, proj, proj, gn_w.reshape(1, h * RET_DV), dmat, qdec, kdec, cdec)


def _sb_kernel(q_ref, k_ref, v_ref, o_ref, r_ref, acc_ref):
    t = SB_TILE
    w = 2 * SB_DH
    pairs = SB_HEADS // 2
    qi = pl.program_id(1)
    lane = lax.broadcasted_iota(jnp.int32, (t, w), 1)
    row = lax.broadcasted_iota(jnp.int32, (t, 2 * t), 0)
    col = lax.broadcasted_iota(jnp.int32, (t, 2 * t), 1)
    diag_causal = jnp.where(col >= t, col - t, col) < row
    uj = lax.broadcasted_iota(jnp.int32, (2 * t, 2 * t), 0)
    us = lax.broadcasted_iota(jnp.int32, (2 * t, 2 * t), 1)
    uj = jnp.where(uj >= t, uj - t, uj)
    umat = jnp.where((uj >= us) | (us >= t), -1.0, 0.0).astype(BF16)
    scale = jnp.asarray(SB_DH ** -0.5, BF16)
    qs = [q_ref[0, :, p * w:(p + 1) * w] * scale for p in range(pairs)]

    def split_heads(x):
        zero = jnp.zeros_like(x)
        return jnp.concatenate([jnp.where(lane < SB_DH, x, zero), jnp.where(lane >= SB_DH, x, zero)], axis=0)

    def key_tiles(kbs, causal, first, ps):
        spans = [pl.ds(pl.multiple_of(kb * t, t), t) for kb in kbs]
        tiles = [(j, p) for j in range(len(kbs)) for p in ps]
        zs, his, los, incls, tots = {}, {}, {}, {}, {}
        for j, p in tiles:
            kcat = split_heads(k_ref[0, spans[j], p * w:(p + 1) * w])
            zs[j, p] = lax.dot_general(qs[p], kcat, (((1,), (1,)), ((), ())), preferred_element_type=F32)
        for j, p in tiles:
            z = zs[j, p]
            sp = jnp.maximum(z, 0.0) + jnp.log(1.0 + jnp.exp(-jnp.abs(z)))
            if causal is not None and j == 0:
                sp = jnp.where(causal, sp, 0.0)
            hi = sp.astype(BF16)
            his[j, p] = hi
            los[j, p] = (sp - hi.astype(F32)).astype(BF16)
        for j, p in tiles:
            cs = [jnp.dot(jnp.concatenate([his[j, p][:, h * t:(h + 1) * t], los[j, p][:, h * t:(h + 1) * t]],
                                          axis=1), umat, preferred_element_type=F32) for h in range(2)]
            incls[j, p] = jnp.concatenate([cs[0][:, :t], cs[1][:, :t]], axis=1)
            tots[j, p] = jnp.concatenate([cs[0][:, t:], cs[1][:, t:]], axis=1)
        avs = {}
        rs = {p: None if first else r_ref[p] for p in ps}
        for j, p in tiles:
            e = zs[j, p] + incls[j, p]
            a = jnp.exp(e if rs[p] is None else e + rs[p])
            if causal is not None and j == 0:
                a = jnp.where(causal, a, 0.0)
            avs[j, p] = a.astype(BF16)
            rs[p] = tots[j, p] if rs[p] is None else rs[p] + tots[j, p]
        for p in ps:
            r_ref[p] = rs[p]
        accs = {p: None if first else acc_ref[p] for p in ps}
        for j, p in tiles:
            vcat = split_heads(v_ref[0, spans[j], p * w:(p + 1) * w])
            av = jnp.dot(avs[j, p], vcat, preferred_element_type=F32)
            accs[p] = av if accs[p] is None else accs[p] + av
        for p in ps:
            acc_ref[p] = accs[p]
        return tuple((jnp.max(rs[p]) > SB_LOG_FLOOR).astype(jnp.int32) for p in ps)

    everyone = list(range(pairs))
    head = qi >= 2
    left = lax.cond(head, lambda: key_tiles([qi, qi - 1, qi - 2], diag_causal, True, everyone),
                    lambda: key_tiles([qi], diag_causal, True, everyone))
    kb = jnp.where(head, qi - 3, qi - 1)

    def cond(carry):
        kb, *left = carry
        return jnp.logical_and(kb >= 0, sum(left) > 0)

    def body(carry):
        kb, *left = carry

        def stragglers():
            return tuple(lax.cond(left[p] > 0, lambda p=p: key_tiles([kb], None, False, [p])[0],
                                  lambda: jnp.int32(0)) for p in everyone)

        left = lax.cond(sum(left) > pairs // 2, lambda: key_tiles([kb], None, False, everyone), stragglers)
        return (kb - 1, *left)

    lax.while_loop(cond, body, (kb, *left))
    for p in range(pairs):
        o_ref[0, :, p * w:(p + 1) * w] = acc_ref[p].astype(o_ref.dtype)


def _stick_breaking_core(q, kv, bsz, seq):
    t = SB_TILE
    return pl.pallas_call(
        _sb_kernel,
        out_shape=jax.ShapeDtypeStruct((bsz, seq, D_MODEL), BF16),
        grid=(bsz, seq // t),
        in_specs=[
            pl.BlockSpec((1, t, D_MODEL), lambda b, i: (b, i, 0)),
            pl.BlockSpec((1, seq, D_MODEL), lambda b, i: (b, 0, 0)),
            pl.BlockSpec((1, seq, D_MODEL), lambda b, i: (b, 0, 1)),
        ],
        out_specs=pl.BlockSpec((1, t, D_MODEL), lambda b, i: (b, i, 0)),
        scratch_shapes=[pltpu.VMEM((SB_HEADS // 2, t, 2 * t), F32),
                        pltpu.VMEM((SB_HEADS // 2, t, 2 * SB_DH), F32)],
        compiler_params=_cparams(("parallel", "parallel")),
        name="stick_breaking_core",
    )(q, kv, kv)


def _route_kernel(a_ref, wp_ref, gp_ref, x_ref, g_ref, sh_ref, sc_ref, r_ref,
                  x1_ref, h_ref, meta_ref, meta_t_ref, pc_ref):
    tb = MOE_BLOCK
    rows = x_ref.shape[0]
    strips = [slice(s, s + ROUTE_STRIP) for s in range(0, rows, ROUTE_STRIP)]
    ys = [jnp.dot(a_ref[rs, :], wp_ref[...], preferred_element_type=F32) for rs in strips]
    parts = []
    for rs, y in zip(strips, ys):
        x1 = x_ref[rs, :] + gp_ref[0] * y
        x1_ref[rs, :] = x1
        h32 = _rms_mod(x1, g_ref[...], sh_ref[0], sc_ref[0])
        h1 = h32.astype(BF16)
        h_ref[rs, :] = h1
        e1 = h32 - h1.astype(F32)
        h2 = e1.astype(BF16)
        parts.append((h1, h2, (e1 - h2.astype(F32)).astype(BF16)))
    r3 = r_ref[...]
    s3 = jnp.concatenate([jnp.dot(h1, r3, preferred_element_type=F32) + jnp.dot(h2, r3, preferred_element_type=F32)
                          + jnp.dot(h3, r3, preferred_element_type=F32) for h1, h2, h3 in parts], axis=0)
    logits = s3 + pltpu.roll(s3, LANES - N_EXPERTS, 1) + pltpu.roll(s3, LANES - 2 * N_EXPERTS, 1)
    lane = lax.broadcasted_iota(jnp.int32, (rows, LANES), 1)
    lane_f = lane.astype(F32)
    neg = jnp.float32(-jnp.inf)
    lg = jnp.where(lane < N_EXPERTS, logits, neg)
    m1 = jnp.max(lg, axis=-1, keepdims=True)
    i1 = jnp.min(jnp.where(lg == m1, lane_f, float(LANES)), axis=-1, keepdims=True)
    oh1 = lane_f == i1
    lg2 = jnp.where(oh1, neg, lg)
    m2 = jnp.max(lg2, axis=-1, keepdims=True)
    i2 = jnp.min(jnp.where(lg2 == m2, lane_f, float(LANES)), axis=-1, keepdims=True)
    oh2 = lane_f == i2
    e = jnp.exp(m2 - m1)
    w1 = 1.0 / (1.0 + e)
    w2 = e / (1.0 + e)

    tr = lax.broadcasted_iota(jnp.int32, (tb, tb), 0)
    tc = lax.broadcasted_iota(jnp.int32, (tb, tb), 1)
    lower = jnp.where(tc < tr, 1.0, 0.0).astype(BF16)
    pj = lax.broadcasted_iota(jnp.int32, (LANES, LANES), 0)
    ps = lax.broadcasted_iota(jnp.int32, (LANES, LANES), 1)
    before = jnp.where(pj < ps, 1.0, 0.0)
    oh1f = jnp.where(oh1, 1.0, 0.0)
    oh2f = jnp.where(oh2, 1.0, 0.0)
    blocks = [slice(b * tb, (b + 1) * tb) for b in range(rows // tb)]
    cum1 = [jnp.dot(lower, oh1f[rs].astype(BF16), preferred_element_type=F32) for rs in blocks]
    cum2 = [jnp.dot(lower, oh2f[rs].astype(BF16), preferred_element_type=F32) for rs in blocks]
    cnt1 = [jnp.sum(oh1f[rs], axis=0, keepdims=True) for rs in blocks]
    cnt2 = [jnp.sum(oh2f[rs], axis=0, keepdims=True) for rs in blocks]
    pc = [jnp.floor((c1 + c2 + (MOE_GROUP - 1.0)) / MOE_GROUP) * MOE_GROUP
          for c1, c2 in zip(cnt1, cnt2)]
    off = [jnp.dot(jnp.broadcast_to(c, (8, LANES)), before, preferred_element_type=F32,
                   precision=lax.Precision.HIGHEST)[0:1] for c in pc]
    pos1 = jnp.concatenate([jnp.sum(oh1f[rs] * (off[b] + cum1[b]), axis=-1, keepdims=True)
                            for b, rs in enumerate(blocks)], axis=0)
    pos2 = jnp.concatenate([jnp.sum(oh2f[rs] * (off[b] + cnt1[b] + cum2[b]), axis=-1, keepdims=True)
                            for b, rs in enumerate(blocks)], axis=0)
    meta = jnp.where(lane == 0, pos1,
                     jnp.where(lane == 1, pos2,
                               jnp.where(lane == 2, w1, jnp.where(lane == 3, w2, 0.0))))
    meta_ref[...] = meta
    for b, rs in enumerate(blocks):
        meta_t_ref[b] = jnp.transpose(meta[rs])[0:8]
        pc_ref[b] = jnp.broadcast_to(pc[b], (8, LANES))


def _moe_route(a, w_proj, gate_proj, x, gain, shift, scale, router, seq):
    n, d = x.shape
    k = a.shape[1]
    tb = MOE_BLOCK
    nb = n // tb
    rb = min(ROUTE_BLOCKS, seq // tb)
    rows = rb * tb
    bsz = shift.shape[0]
    r1 = router.astype(BF16)
    r2 = (router - r1.astype(F32)).astype(BF16)
    r3 = (router - r1.astype(F32) - r2.astype(F32)).astype(BF16)
    r_pad = jnp.zeros((d, LANES), BF16).at[:, :3 * N_EXPERTS].set(jnp.concatenate([r1, r2, r3], axis=1))
    vec = lambda i: (i * rows // seq, 0, 0)
    return pl.pallas_call(
        _route_kernel,
        out_shape=(
            jax.ShapeDtypeStruct((n, d), F32),
            jax.ShapeDtypeStruct((n, d), BF16),
            jax.ShapeDtypeStruct((n, LANES), F32),
            jax.ShapeDtypeStruct((nb, 8, tb), F32),
            jax.ShapeDtypeStruct((nb, 8, LANES), F32),
        ),
        grid=(nb // rb,),
        in_specs=[
            pl.BlockSpec((rows, k), lambda i: (i, 0)),
            pl.BlockSpec((k, d), lambda i: (0, 0), pipeline_mode=pl.Buffered(1)),
            pl.BlockSpec((1, 1, d), vec),
            pl.BlockSpec((rows, d), lambda i: (i, 0)),
            pl.BlockSpec((1, d), lambda i: (0, 0)),
            pl.BlockSpec((1, 1, d), vec),
            pl.BlockSpec((1, 1, d), vec),
            pl.BlockSpec((d, LANES), lambda i: (0, 0)),
        ],
        out_specs=(
            pl.BlockSpec((rows, d), lambda i: (i, 0)),
            pl.BlockSpec((rows, d), lambda i: (i, 0)),
            pl.BlockSpec((rows, LANES), lambda i: (i, 0)),
            pl.BlockSpec((rb, 8, tb), lambda i: (i, 0, 0)),
            pl.BlockSpec((rb, 8, LANES), lambda i: (i, 0, 0)),
        ),
        compiler_params=_cparams(("parallel",)),
        name="moe_route",
    )(a, w_proj, gate_proj.reshape(bsz, 1, d), x, gain.reshape(1, d), shift.reshape(bsz, 1, d),
      scale.reshape(bsz, 1, d), r_pad)


def _segment_copies(loc_ref, glb_ref, len_ref, blk, make_copy, wait):
    for e in range(N_EXPERTS):
        idx = blk * N_EXPERTS + e
        loc = loc_ref[idx]
        glb = glb_ref[idx]
        length = len_ref[idx]
        done = jnp.int32(0)
        for sz in SEG_SIZES:
            take = (length & sz) != 0

            @pl.when(take)
            def _(loc=loc, glb=glb, done=done, sz=sz):
                cp = make_copy(pl.multiple_of(loc + done, MOE_GROUP),
                               pl.multiple_of(glb + done, MOE_GROUP), sz)
                if wait:
                    cp.wait()
                else:
                    cp.start()

            done = done + jnp.where(take, sz, 0)


def _dispatch_kernel(loc_ref, glb_ref, len_ref, h_ref, meta_t_ref, hs_ref, buf_ref, zero_ref, sem):
    b = pl.program_id(0)
    tb = MOE_BLOCK
    pos1 = meta_t_ref[0, 0:1, :]
    pos2 = meta_t_ref[0, 1:2, :]
    r = lax.broadcasted_iota(jnp.int32, (MOE_CAP, tb), 0).astype(F32)
    perm = jnp.where((r == pos1) | (r == pos2), 1.0, 0.0).astype(BF16)
    sorted_rows = jnp.dot(perm, h_ref[...], preferred_element_type=F32).astype(BF16)

    def make_copy(loc, glb, sz):
        return pltpu.make_async_copy(buf_ref.at[pl.ds(loc, sz)], hs_ref.at[pl.ds(glb, sz)], sem)

    @pl.when(b > 0)
    def _():
        _segment_copies(loc_ref, glb_ref, len_ref, b - 1, make_copy, wait=True)

    buf_ref[...] = sorted_rows
    _segment_copies(loc_ref, glb_ref, len_ref, b, make_copy, wait=False)

    @pl.when(b == pl.num_programs(0) - 1)
    def _():
        def make_zero_copy(_, glb, sz):
            return pltpu.make_async_copy(zero_ref.at[pl.ds(0, sz)], hs_ref.at[pl.ds(glb, sz)], sem)

        zero_ref[...] = jnp.zeros_like(zero_ref)
        gaps = pl.num_programs(0)
        _segment_copies(loc_ref, glb_ref, len_ref, gaps, make_zero_copy, wait=False)

        tail = (gaps + 1) * N_EXPERTS
        pieces = len_ref[tail]

        def tail_copy(i):
            row = pl.multiple_of(glb_ref[tail] + i * zero_ref.shape[0], zero_ref.shape[0])
            return pltpu.make_async_copy(zero_ref, hs_ref.at[pl.ds(row, zero_ref.shape[0])], sem)

        @pl.loop(0, pieces)
        def _(i):
            tail_copy(i).start()

        _segment_copies(loc_ref, glb_ref, len_ref, b, make_copy, wait=True)
        _segment_copies(loc_ref, glb_ref, len_ref, gaps, make_zero_copy, wait=True)

        @pl.loop(0, pieces)
        def _(i):
            tail_copy(i).wait()


def _moe_dispatch(h, meta_t, loc, glb, seg_len, rows_total):
    n, d = h.shape
    tb = MOE_BLOCK
    nb = n // tb
    return pl.pallas_call(
        _dispatch_kernel,
        out_shape=jax.ShapeDtypeStruct((rows_total, d), BF16),
        grid_spec=pltpu.PrefetchScalarGridSpec(
            num_scalar_prefetch=3,
            grid=(nb,),
            in_specs=[
                pl.BlockSpec((tb, d), lambda i, *_: (i, 0)),
                pl.BlockSpec((1, 8, tb), lambda i, *_: (i, 0, 0)),
            ],
            out_specs=pl.BlockSpec(memory_space=pl.ANY),
            scratch_shapes=[pltpu.VMEM((MOE_CAP, d), BF16), pltpu.VMEM((SEG_SIZES[0], d), BF16),
                            pltpu.SemaphoreType.DMA],
        ),
        compiler_params=_cparams(("arbitrary",)),
        name="moe_dispatch",
    )(loc, glb, seg_len, h, meta_t)


def _expert_kernel(te_ref, tx_ref, tv_ref, x_ref, wa_ref, wb_ref, wo_ref, o_ref, acc_ref):
    i = pl.program_id(0)
    f = pl.program_id(1)

    last = pl.num_programs(1) - 1
    valid = tv_ref[i] == 1

    def run(emit):
        _swiglu(x_ref[...], lambda sl: wa_ref[0, :, sl], lambda sl: wb_ref[0, :, sl],
                lambda sl: wo_ref[0, :, sl].astype(BF16), wo_ref.shape[1], emit)

    @pl.when(jnp.logical_and(valid, f == 0))
    def _():
        def emit(cols, y):
            acc_ref[:, cols] = y
        run(emit)

    if FF_EXPERT // FF_EXPERT_TILE > 2:
        @pl.when(jnp.logical_and(valid, jnp.logical_and(f > 0, f < last)))
        def _():
            def emit(cols, y):
                acc_ref[:, cols] += y
            run(emit)

    @pl.when(jnp.logical_and(valid, f == last))
    def _():
        def emit(cols, y):
            o_ref[:, cols] = (acc_ref[:, cols] + y).astype(o_ref.dtype)
        run(emit)

    @pl.when(jnp.logical_and(tv_ref[i] == 0, f == pl.num_programs(1) - 1))
    def _():
        o_ref[...] = jnp.zeros_like(o_ref)


def _moe_experts(hs, w_in, w_out, tile_expert, tile_row, tile_valid):
    rows, d = hs.shape
    tm = ROW_TILE
    tf = FF_EXPERT_TILE
    nf = FF_EXPERT // tf
    nt = rows // tm

    def f_eff(i, f, tv):
        return jnp.where(tv[i] == 1, f, nf - 1)

    return pl.pallas_call(
        _expert_kernel,
        out_shape=jax.ShapeDtypeStruct((rows, d), BF16),
        grid_spec=pltpu.PrefetchScalarGridSpec(
            num_scalar_prefetch=3,
            grid=(nt, nf),
            in_specs=[
                pl.BlockSpec((tm, d), lambda i, f, te, tx, tv: (tx[i], 0)),
                pl.BlockSpec((1, d, tf), lambda i, f, te, tx, tv: (te[i], 0, f_eff(i, f, tv))),
                pl.BlockSpec((1, d, tf), lambda i, f, te, tx, tv: (te[i], 0, nf + f_eff(i, f, tv))),
                pl.BlockSpec((1, tf, d), lambda i, f, te, tx, tv: (te[i], f_eff(i, f, tv), 0)),
            ],
            out_specs=pl.BlockSpec((tm, d), lambda i, f, te, tx, tv: (i, 0)),
            scratch_shapes=[pltpu.VMEM((tm, d), F32)],
        ),
        compiler_params=_cparams(("arbitrary", "arbitrary")),
        name="moe_experts",
    )(tile_expert, tile_row, tile_valid, hs, w_in, w_in, w_out)


def _combine_kernel(loc_ref, glb_ref, len_ref, x_ref, gate_ref, meta_ref, fg_ref, ys_ref, o_ref, buf_ref, sems,
                    *, final_norm):
    b = pl.program_id(0)
    tb = MOE_BLOCK

    def fetch(blk, slot, wait):
        def make_copy(loc, glb, sz):
            return pltpu.make_async_copy(ys_ref.at[pl.ds(glb, sz)], buf_ref.at[slot, pl.ds(loc, sz)],
                                         sems.at[slot])

        _segment_copies(loc_ref, glb_ref, len_ref, blk, make_copy, wait)

    def start(blk, slot):
        buf_ref[slot] = jnp.zeros(buf_ref.shape[1:], buf_ref.dtype)
        fetch(blk, slot, wait=False)

    @pl.when(b == 0)
    def _():
        start(0, 0)

    @pl.when(b + 1 < pl.num_programs(0))
    def _():
        start(b + 1, (b + 1) % 2)

    slot = b % 2
    meta = meta_ref[...]
    pos1 = meta[:, 0:1]
    pos2 = meta[:, 1:2]
    w1 = meta[:, 2:3]
    w2 = meta[:, 3:4]
    r = lax.broadcasted_iota(jnp.int32, (tb, MOE_CAP), 1).astype(F32)
    sel1 = jnp.where(r == pos1, 1.0, 0.0).astype(BF16)
    sel2 = jnp.where(r == pos2, 1.0, 0.0).astype(BF16)
    fetch(b, slot, wait=True)
    ys = buf_ref[slot]
    y = (w1 * jnp.dot(sel1, ys, preferred_element_type=F32)
         + w2 * jnp.dot(sel2, ys, preferred_element_type=F32))
    x = x_ref[...] + gate_ref[0] * y
    if final_norm:
        x = (x * lax.rsqrt(jnp.mean(x * x, axis=-1, keepdims=True) + EPS)) * fg_ref[...]
    o_ref[...] = x


def _moe_combine(x, gate, meta, ys, loc, glb, seg_len, seq, final_gain=None):
    n, d = x.shape
    tb = MOE_BLOCK
    bsz = gate.shape[0]
    fg = jnp.ones((1, d), F32) if final_gain is None else final_gain.reshape(1, d)
    return pl.pallas_call(
        functools.partial(_combine_kernel, final_norm=final_gain is not None),
        out_shape=jax.ShapeDtypeStruct((n, d), F32),
        grid_spec=pltpu.PrefetchScalarGridSpec(
            num_scalar_prefetch=3,
            grid=(n // tb,),
            in_specs=[
                pl.BlockSpec((tb, d), lambda i, *_: (i, 0)),
                pl.BlockSpec((1, 1, d), lambda i, *_: (i * tb // seq, 0, 0)),
                pl.BlockSpec((tb, LANES), lambda i, *_: (i, 0)),
                pl.BlockSpec((1, d), lambda i, *_: (0, 0)),
                pl.BlockSpec(memory_space=pl.ANY),
            ],
            out_specs=pl.BlockSpec((tb, d), lambda i, *_: (i, 0)),
            scratch_shapes=[pltpu.VMEM((2, MOE_CAP, d), BF16), pltpu.SemaphoreType.DMA((2,))],
        ),
        compiler_params=_cparams(("arbitrary",)),
        name="moe_combine",
    )(loc, glb, seg_len, x, gate.reshape(bsz, 1, d), meta, fg, ys)


def _moe_layout(pc):
    nb = pc.shape[0]
    tm = ROW_TILE
    n_tiles = (nb * (2 * MOE_BLOCK + N_EXPERTS * (MOE_GROUP - 1)) + tm - 1) // tm + N_EXPERTS
    loc = jnp.cumsum(pc, axis=1) - pc
    tiles_e = (jnp.sum(pc, axis=0) + tm - 1) // tm
    tile_end = jnp.cumsum(tiles_e)
    base_e = (tile_end - tiles_e) * tm
    glb = base_e[None, :] + jnp.cumsum(pc, axis=0) - pc
    n_valid = tile_end[-1]
    ids = jnp.arange(n_tiles, dtype=jnp.int32)
    tile_row = jnp.maximum(jnp.minimum(ids, n_valid - 1), 0)
    tile_expert = jnp.sum((tile_row[:, None] >= tile_end[None, :]).astype(jnp.int32), axis=1)
    tile_valid = (ids < n_valid).astype(jnp.int32)
    rows_e = jnp.sum(pc, axis=0)
    first = jnp.zeros_like(rows_e).at[0].set(1)
    loc = jnp.concatenate([loc, jnp.zeros_like(rows_e)[None], jnp.zeros_like(rows_e)[None]])
    glb = jnp.concatenate([glb, (base_e + rows_e)[None], (first * n_valid * tm)[None]])
    seg = jnp.concatenate([pc, (tiles_e * tm - rows_e)[None],
                           (first * (n_tiles - n_valid) * (tm // SEG_SIZES[0]))[None]])
    flat = lambda a: a.reshape(-1).astype(jnp.int32)
    return flat(loc), flat(glb), flat(seg), tile_expert.astype(jnp.int32), tile_row.astype(jnp.int32), tile_valid, n_tiles * tm


def _moe_block(a, w_proj, gate_proj, x, gain, shift, scale, gate, router, w_in, w_out, layer, seq,
               final_gain=None):
    x, h, meta, meta_t, pc = _moe_route(a, w_proj, gate_proj, x, gain, shift, scale, router, seq)
    pc_i = pc[:, 0, :N_EXPERTS].astype(jnp.int32)
    loc, glb, seg_len, tile_expert, tile_row, tile_valid, rows_total = _moe_layout(pc_i)
    hs = _moe_dispatch(h, meta_t, loc, glb, seg_len, rows_total)
    ys = _moe_experts(hs, w_in, w_out, tile_expert + layer * N_EXPERTS, tile_row, tile_valid)
    return _moe_combine(x, gate, meta, ys, loc, glb, seg_len, seq, final_gain)


def _final_norm_kernel(x_ref, g_ref, o_ref):
    x = x_ref[...]
    o_ref[...] = (x * lax.rsqrt(jnp.mean(x * x, axis=-1, keepdims=True) + EPS)) * g_ref[...]


def _final_norm(x, gain):
    n, d = x.shape
    tm = min(ROW_TILE, n)
    return pl.pallas_call(
        _final_norm_kernel,
        out_shape=jax.ShapeDtypeStruct((n, d), F32),
        grid=(n // tm,),
        in_specs=[pl.BlockSpec((tm, d), lambda i: (i, 0)), pl.BlockSpec((1, d), lambda i: (0, 0))],
        out_specs=pl.BlockSpec((tm, d), lambda i: (i, 0)),
        compiler_params=_cparams(("parallel",)),
        name="final_norm",
    )(x, gain.reshape(1, d))


def kernel(x, c, positions, ada_w, ada_b, norm_mix, norm_ff, ret_w_in, ret_gn, ret_w_out, kv_ada_w, kv_ada_b, kv_norm, kv_w, sb_w_q, sb_w_out, ff_w_in, ff_w_out, moe_router, moe_w_in, moe_w_out, final_norm):
    bsz, seq, d = x.shape
    n = bsz * seq
    mod = _ada_mod(c, ada_w, ada_b)
    kv_mod = _ada_mod(c, kv_ada_w[None], kv_ada_b[None])[0]
    rope = _rope_tables(positions)
    ret_tables = _retention_tables()
    moe_in = moe_w_in.astype(BF16).reshape(-1, d, 2 * FF_EXPERT)
    moe_out = moe_w_out.reshape(-1, FF_EXPERT, d)
    xf = x.reshape(n, d)
    kv = None
    for i in range(DEPTH):
        sh_m, sc_m, g_m, sh_f, sc_f, g_f = [mod[i, :, p * d:(p + 1) * d] for p in range(6)]
        if i < N_A:
            proj = _norm_mod_matmul(xf, norm_mix[i], sh_m, sc_m, ret_w_in[i].astype(BF16), seq, rope=rope)
            o = _retention_core(proj.reshape(bsz, seq, -1), ret_gn[i], ret_tables, bsz, seq)
            o, w_o = o.reshape(n, -1), ret_w_out[i].astype(BF16)
        else:
            j = i - N_A
            q = _norm_mod_matmul(xf, norm_mix[i], sh_m, sc_m, sb_w_q[j].astype(BF16), seq)
            o = _stick_breaking_core(q.reshape(bsz, seq, d), kv, bsz, seq)
            o, w_o = o.reshape(n, d), sb_w_out[j].astype(BF16)
        if i % 2 == 0:
            xf = _dense_ffn(o, w_o, g_m, xf, norm_ff[i], sh_f, sc_f, g_f, ff_w_in[i // 2].astype(BF16),
                            ff_w_out[i // 2].astype(BF16), seq)
        else:
            xf = _moe_block(o, w_o, g_m, xf, norm_ff[i], sh_f, sc_f, g_f, moe_router[i // 2],
                            moe_in, moe_out, i // 2, seq,
                            final_gain=final_norm if i == DEPTH - 1 else None)
        if i == N_A - 1:
            kv = _norm_mod_matmul(xf, kv_norm, kv_mod[:, :d], kv_mod[:, d:], kv_w.astype(BF16), seq)
            kv = kv.reshape(bsz, seq, 2 * d)
    if (DEPTH - 1) % 2 == 0:
        xf = _final_norm(xf, final_norm)
    return xf.reshape(bsz, seq, d)
```

```python
import functools

import jax
import jax.numpy as jnp
from jax import lax
from jax.experimental import pallas as pl
from jax.experimental.pallas import tpu as pltpu

F32 = jnp.float32
BF16 = jnp.bfloat16

D_MODEL = 1024
DEPTH = 4
N_A = DEPTH // 2
CHUNK = 64
RET_HEADS = 8
RET_DK = D_MODEL // RET_HEADS
RET_DV = 2 * D_MODEL // RET_HEADS
ROPE_BASE = 10000.0
SB_HEADS = 16
SB_DH = D_MODEL // SB_HEADS
FF_DENSE = 2816
N_EXPERTS = 8
FF_EXPERT = 3584
EPS = 1e-6

LANES = 128
BF16_ROWS = 16
MXU_COLS = 256
VMEM_LIMIT = 56 * 1024 * 1024

ROW_TILE = 1024
NMM_ROWS = 512
NMM_WIDE = 2048
NMM_CHUNK = 512
DENSE_ROWS = 512
FF_EXPERT_TILE = 1792
RET_BLOCK = 256
SB_TILE = 128
MOE_BLOCK = 512
ROUTE_BLOCKS = 2
ROUTE_STRIP = 256
MOE_GROUP = BF16_ROWS
MOE_CAP = 2 * MOE_BLOCK + N_EXPERTS * MOE_GROUP
SEG_SIZES = (512, 256, 128, 64, 32, 16)
SB_LOG_FLOOR = -110.0


def _cparams(sem):
    return pltpu.CompilerParams(dimension_semantics=sem, vmem_limit_bytes=VMEM_LIMIT)


def _sigmoid(v):
    return 1.0 / (1.0 + jnp.exp(-v))


def _rms_mod(x, gain, shift, scale):
    y = x * lax.rsqrt(jnp.mean(x * x, axis=-1, keepdims=True) + EPS)
    return (y * gain) * (1.0 + scale) + shift


def _ada_kernel(c_ref, w_ref, b_ref, o_ref):
    c = c_ref[...]
    cond = (c * _sigmoid(c)).astype(BF16)
    o_ref[0] = jnp.dot(cond, w_ref[0].astype(BF16), preferred_element_type=F32) + b_ref[0]


def _ada_mod(c, w, b):
    nl, d, n = w.shape
    bsz = c.shape[0]
    tn = 1024
    return pl.pallas_call(
        _ada_kernel,
        out_shape=jax.ShapeDtypeStruct((nl, bsz, n), F32),
        grid=(nl, n // tn),
        in_specs=[
            pl.BlockSpec((bsz, d), lambda l, j: (0, 0)),
            pl.BlockSpec((1, d, tn), lambda l, j: (l, 0, j)),
            pl.BlockSpec((1, 1, tn), lambda l, j: (l, 0, j)),
        ],
        out_specs=pl.BlockSpec((1, bsz, tn), lambda l, j: (l, 0, j)),
        compiler_params=_cparams(("parallel", "parallel")),
        name="ada_mod",
    )(c, w, b.reshape(nl, 1, n))


def _rope_table_kernel(pos_ref, freq_ref, cs_ref, sn_ref):
    ang = pos_ref[...] * freq_ref[...]
    lane = lax.broadcasted_iota(jnp.int32, ang.shape, 1)
    cs_ref[...] = jnp.cos(ang)
    sn_ref[...] = jnp.where(lane < RET_DK // 2, -jnp.sin(ang), jnp.sin(ang))


def _rope_tables(positions):
    n = positions.size
    inv_freq = jnp.power(ROPE_BASE, -jnp.arange(0, RET_DK, 2, dtype=F32) / RET_DK)
    freq = jnp.concatenate([inv_freq, inv_freq]).reshape(1, RET_DK)
    pos = jnp.broadcast_to(positions.astype(F32).reshape(n, 1), (n, RET_DK))
    tr = min(n, 2048)
    return pl.pallas_call(
        _rope_table_kernel,
        out_shape=(jax.ShapeDtypeStruct((n, RET_DK), F32),) * 2,
        grid=(n // tr,),
        in_specs=[pl.BlockSpec((tr, RET_DK), lambda i: (i, 0)),
                  pl.BlockSpec((1, RET_DK), lambda i: (0, 0))],
        out_specs=(pl.BlockSpec((tr, RET_DK), lambda i: (i, 0)),) * 2,
        compiler_params=_cparams(("parallel",)),
        name="rope_tables",
    )(pos, freq)


def _nmm_kernel(x_ref, g_ref, sh_ref, sc_ref, w_ref, *rest, rope_cols):
    if rope_cols:
        cs_ref, sn_ref, o_ref = rest
    else:
        (o_ref,) = rest
    cw = NMM_CHUNK
    n_chunks = w_ref.shape[1] // cw
    h = _rms_mod(x_ref[...], g_ref[...], sh_ref[0], sc_ref[0]).astype(BF16)

    def mm(c):
        return jnp.dot(h, w_ref[:, c * cw:(c + 1) * cw], preferred_element_type=F32)

    nxt = mm(0)
    for c in range(n_chunks):
        r = nxt
        if c + 1 < n_chunks:
            nxt = mm(c + 1)
        col0 = c * cw
        if col0 >= rope_cols:
            o_ref[:, col0:col0 + cw] = r.astype(o_ref.dtype)
            continue
        k_scale = RET_DK ** -0.5 if col0 >= rope_cols // 2 else None
        for s in range(cw // RET_DK):
            t = r[:, s * RET_DK:(s + 1) * RET_DK]
            rot = t * cs_ref[...] + pltpu.roll(t, RET_DK // 2, 1) * sn_ref[...]
            if k_scale is not None:
                rot = rot * k_scale
            o_ref[:, col0 + s * RET_DK:col0 + (s + 1) * RET_DK] = rot.astype(o_ref.dtype)


def _norm_mod_matmul(x, gain, shift, scale, w, seq, rope=None):
    n, d = x.shape
    m = w.shape[1]
    tm = min(NMM_ROWS if m > NMM_WIDE else ROW_TILE, seq)
    bsz = shift.shape[0]
    vec = lambda i: (i * tm // seq, 0, 0)
    in_specs = [
        pl.BlockSpec((tm, d), lambda i: (i, 0)),
        pl.BlockSpec((1, d), lambda i: (0, 0)),
        pl.BlockSpec((1, 1, d), vec),
        pl.BlockSpec((1, 1, d), vec),
        pl.BlockSpec((d, m), lambda i: (0, 0), pipeline_mode=pl.Buffered(1)),
    ]
    args = [x, gain.reshape(1, d), shift.reshape(bsz, 1, d), scale.reshape(bsz, 1, d), w]
    rope_cols = 0
    if rope is not None:
        rope_cols = 2 * RET_HEADS * RET_DK
        in_specs += [pl.BlockSpec((tm, RET_DK), lambda i: (i, 0))] * 2
        args += list(rope)
    return pl.pallas_call(
        functools.partial(_nmm_kernel, rope_cols=rope_cols),
        out_shape=jax.ShapeDtypeStruct((n, m), BF16),
        grid=(n // tm,),
        in_specs=in_specs,
        out_specs=pl.BlockSpec((tm, m), lambda i: (i, 0)),
        compiler_params=_cparams(("parallel",)),
        name="norm_mod_matmul_rope" if rope_cols else "norm_mod_matmul",
    )(*args)


def _swiglu(h, load_wa, load_wb, load_wo, width, emit):
    def up(c):
        sl = slice(c * MXU_COLS, (c + 1) * MXU_COLS)
        return (jnp.dot(h, load_wa(sl), preferred_element_type=F32),
                jnp.dot(h, load_wb(sl), preferred_element_type=F32))

    n_chunks = width // MXU_COLS
    acts = []
    nxt = up(0)
    for c in range(n_chunks):
        a, b = nxt
        if c + 1 < n_chunks:
            nxt = up(c + 1)
        acts.append((a * _sigmoid(a) * b).astype(BF16))
    act = jnp.concatenate(acts, axis=1)

    def down(c):
        return jnp.dot(act, load_wo(slice(c * MXU_COLS, (c + 1) * MXU_COLS)), preferred_element_type=F32)

    n_out = D_MODEL // MXU_COLS
    nxt = down(0)
    for c in range(n_out):
        y = nxt
        if c + 1 < n_out:
            nxt = down(c + 1)
        emit(slice(c * MXU_COLS, (c + 1) * MXU_COLS), y)


def _ffn_kernel(a_ref, wp_ref, gp_ref, x_ref, g_ref, sh_ref, sc_ref, gate_ref, wa_ref, wb_ref, wo_ref, o_ref,
                x1_ref):
    x1_ref[...] = x_ref[...] + gp_ref[0] * jnp.dot(a_ref[...], wp_ref[...], preferred_element_type=F32)
    h = _rms_mod(x1_ref[...], g_ref[...], sh_ref[0], sc_ref[0]).astype(BF16)

    def emit(cols, y):
        o_ref[:, cols] = x1_ref[:, cols] + gate_ref[0, :, cols] * y

    _swiglu(h, lambda sl: wa_ref[:, sl], lambda sl: wb_ref[:, sl], lambda sl: wo_ref[:, sl],
            wo_ref.shape[0], emit)


def _dense_ffn(a, w_proj, gate_proj, x, gain, shift, scale, gate, w_in, w_out, seq):
    n, d = x.shape
    k = a.shape[1]
    ff = w_out.shape[0]
    tm = min(DENSE_ROWS, seq)
    bsz = shift.shape[0]
    vec = lambda i: (i * tm // seq, 0, 0)
    resident = pl.Buffered(1)
    return pl.pallas_call(
        _ffn_kernel,
        out_shape=jax.ShapeDtypeStruct((n, d), F32),
        grid=(n // tm,),
        in_specs=[
            pl.BlockSpec((tm, k), lambda i: (i, 0)),
            pl.BlockSpec((k, d), lambda i: (0, 0), pipeline_mode=resident),
            pl.BlockSpec((1, 1, d), vec),
            pl.BlockSpec((tm, d), lambda i: (i, 0)),
            pl.BlockSpec((1, d), lambda i: (0, 0)),
            pl.BlockSpec((1, 1, d), vec),
            pl.BlockSpec((1, 1, d), vec),
            pl.BlockSpec((1, 1, d), vec),
            pl.BlockSpec((d, ff), lambda i: (0, 0), pipeline_mode=resident),
            pl.BlockSpec((d, ff), lambda i: (0, 1), pipeline_mode=resident),
            pl.BlockSpec((ff, d), lambda i: (0, 0), pipeline_mode=resident),
        ],
        out_specs=pl.BlockSpec((tm, d), lambda i: (i, 0)),
        scratch_shapes=[pltpu.VMEM((tm, d), F32)],
        compiler_params=_cparams(("parallel",)),
        name="dense_ffn",
    )(a, w_proj, gate_proj.reshape(bsz, 1, d), x, gain.reshape(1, d), shift.reshape(bsz, 1, d),
      scale.reshape(bsz, 1, d), gate.reshape(bsz, 1, d), w_in, w_in, w_out)


def _ret_kernel(q_ref, k_ref, v_ref, g_ref, gn_ref, dm_ref, qd_ref, kd_ref, cd_ref, o_ref, *, nblk):
    dmat = dm_ref[0]
    qdec = qd_ref[0]
    kdec = kd_ref[0]
    cdec = cd_ref[0, 0:1, :]
    gn = gn_ref[...]

    def load(t):
        sl = slice(t * RET_BLOCK, (t + 1) * RET_BLOCK)
        return q_ref[0, sl, :], k_ref[0, sl, :], v_ref[0, sl, :]

    def scores(q, k):
        return lax.dot_general(q, k, (((1,), (1,)), ((), ())), preferred_element_type=F32)

    state = jnp.zeros((RET_DK, RET_DV), F32)
    q, k, v = load(0)
    s = scores(q, k)
    for t in range(nblk):
        qd = (q.astype(F32) * qdec).astype(BF16)
        cross = jnp.dot(qd, state.astype(BF16), preferred_element_type=F32)
        kdt = jnp.transpose(k.astype(F32) * kdec).astype(BF16)
        update = jnp.dot(kdt, v, preferred_element_type=F32)
        if t + 1 < nblk:
            q, k, v_next = load(t + 1)
            s_next = scores(q, k)
        intra = jnp.dot((s * dmat).astype(BF16), v, preferred_element_type=F32)
        state = state * cdec + update
        o = intra + cross
        mu = jnp.mean(o, axis=-1, keepdims=True)
        oc = o - mu
        var = jnp.mean(oc * oc, axis=-1, keepdims=True)
        on = (oc * lax.rsqrt(var + EPS)) * gn
        sl = slice(t * RET_BLOCK, (t + 1) * RET_BLOCK)
        g = g_ref[0, sl, :].astype(F32)
        o_ref[0, sl, :] = ((g * _sigmoid(g)) * on).astype(o_ref.dtype)
        if t + 1 < nblk:
            v, s = v_next, s_next


def _retention_tables():
    t = RET_BLOCK
    log_gamma = jnp.log(1.0 - jnp.exp2(-5.0 - jnp.arange(RET_HEADS, dtype=F32)))
    pos = jnp.arange(t, dtype=F32)
    chunk = jnp.arange(t) // CHUNK
    dist = jnp.abs(pos[:, None] - pos[None, :])
    visible = (chunk[None, :] <= chunk[:, None]).astype(F32)
    dmat = jnp.exp(dist[None] * log_gamma[:, None, None]) * visible[None]
    qdec = jnp.exp((pos[None, :] + 1.0) * log_gamma[:, None])
    kdec = jnp.exp((t - 1.0 - pos)[None, :] * log_gamma[:, None])
    cdec = jnp.exp(t * log_gamma)
    qdec = jnp.broadcast_to(qdec[:, :, None], (RET_HEADS, t, RET_DK))
    kdec = jnp.broadcast_to(kdec[:, :, None], (RET_HEADS, t, RET_DK))
    cdec = jnp.broadcast_to(cdec[:, None, None], (RET_HEADS, 8, RET_DV))
    return dmat, qdec, kdec, cdec


def _retention_core(proj, gn_w, tables, bsz, seq):
    dmat, qdec, kdec, cdec = tables
    h = RET_HEADS
    t = RET_BLOCK
    kq = h * RET_DK // RET_DK
    kv = 2 * h * RET_DK // RET_DV
    return pl.pallas_call(
        functools.partial(_ret_kernel, nblk=seq // t),
        out_shape=jax.ShapeDtypeStruct((bsz, seq, h * RET_DV), BF16),
        grid=(bsz, h),
        in_specs=[
            pl.BlockSpec((1, seq, RET_DK), lambda b, i: (b, 0, i)),
            pl.BlockSpec((1, seq, RET_DK), lambda b, i: (b, 0, kq + i)),
            pl.BlockSpec((1, seq, RET_DV), lambda b, i: (b, 0, kv + i)),
            pl.BlockSpec((1, seq, RET_DV), lambda b, i: (b, 0, kv + h + i)),
            pl.BlockSpec((1, RET_DV), lambda b, i: (0, i)),
            pl.BlockSpec((1, t, t), lambda b, i: (i, 0, 0)),
            pl.BlockSpec((1, t, RET_DK), lambda b, i: (i, 0, 0)),
            pl.BlockSpec((1, t, RET_DK), lambda b, i: (i, 0, 0)),
            pl.BlockSpec((1, 8, RET_DV), lambda b, i: (i, 0, 0)),
        ],
        out_specs=pl.BlockSpec((1, seq, RET_DV), lambda b, i: (b, 0, i)),
        compiler_params=_cparams(("parallel", "parallel")),
        name="retention_core",
    )(proj, proj, proj, proj, gn_w.reshape(1, h * RET_DV), dmat, qdec, kdec, cdec)


def _sb_kernel(q_ref, k_ref, v_ref, o_ref, r_ref, acc_ref):
    t = SB_TILE
    w = 2 * SB_DH
    pairs = SB_HEADS // 2
    qi = pl.program_id(1)
    lane = lax.broadcasted_iota(jnp.int32, (t, w), 1)
    row = lax.broadcasted_iota(jnp.int32, (t, 2 * t), 0)
    col = lax.broadcasted_iota(jnp.int32, (t, 2 * t), 1)
    diag_causal = jnp.where(col >= t, col - t, col) < row
    uj = lax.broadcasted_iota(jnp.int32, (2 * t, 2 * t), 0)
    us = lax.broadcasted_iota(jnp.int32, (2 * t, 2 * t), 1)
    uj = jnp.where(uj >= t, uj - t, uj)
    umat = jnp.where((uj >= us) | (us >= t), -1.0, 0.0).astype(BF16)
    scale = jnp.asarray(SB_DH ** -0.5, BF16)
    qs = [q_ref[0, :, p * w:(p + 1) * w] * scale for p in range(pairs)]

    def split_heads(x):
        zero = jnp.zeros_like(x)
        return jnp.concatenate([jnp.where(lane < SB_DH, x, zero), jnp.where(lane >= SB_DH, x, zero)], axis=0)

    def key_tiles(kbs, causal, first, ps):
        spans = [pl.ds(pl.multiple_of(kb * t, t), t) for kb in kbs]
        tiles = [(j, p) for j in range(len(kbs)) for p in ps]
        zs, his, los, incls, tots = {}, {}, {}, {}, {}
        for j, p in tiles:
            kcat = split_heads(k_ref[0, spans[j], p * w:(p + 1) * w])
            zs[j, p] = lax.dot_general(qs[p], kcat, (((1,), (1,)), ((), ())), preferred_element_type=F32)
        for j, p in tiles:
            z = zs[j, p]
            sp = jnp.maximum(z, 0.0) + jnp.log(1.0 + jnp.exp(-jnp.abs(z)))
            if causal is not None and j == 0:
                sp = jnp.where(causal, sp, 0.0)
            hi = sp.astype(BF16)
            his[j, p] = hi
            los[j, p] = (sp - hi.astype(F32)).astype(BF16)
        for j, p in tiles:
            cs = [jnp.dot(jnp.concatenate([his[j, p][:, h * t:(h + 1) * t], los[j, p][:, h * t:(h + 1) * t]],
                                          axis=1), umat, preferred_element_type=F32) for h in range(2)]
            incls[j, p] = jnp.concatenate([cs[0][:, :t], cs[1][:, :t]], axis=1)
            tots[j, p] = jnp.concatenate([cs[0][:, t:], cs[1][:, t:]], axis=1)
        avs = {}
        rs = {p: None if first else r_ref[p] for p in ps}
        for j, p in tiles:
            e = zs[j, p] + incls[j, p]
            a = jnp.exp(e if rs[p] is None else e + rs[p])
            if causal is not None and j == 0:
                a = jnp.where(causal, a, 0.0)
            avs[j, p] = a.astype(BF16)
            rs[p] = tots[j, p] if rs[p] is None else rs[p] + tots[j, p]
        for p in ps:
            r_ref[p] = rs[p]
        accs = {p: None if first else acc_ref[p] for p in ps}
        for j, p in tiles:
            vcat = split_heads(v_ref[0, spans[j], p * w:(p + 1) * w])
            av = jnp.dot(avs[j, p], vcat, preferred_element_type=F32)
            accs[p] = av if accs[p] is None else accs[p] + av
        for p in ps:
            acc_ref[p] = accs[p]
        return tuple((jnp.max(rs[p]) > SB_LOG_FLOOR).astype(jnp.int32) for p in ps)

    everyone = list(range(pairs))
    head = qi >= 2
    left = lax.cond(head, lambda: key_tiles([qi, qi - 1, qi - 2], diag_causal, True, everyone),
                    lambda: key_tiles([qi], diag_causal, True, everyone))
    kb = jnp.where(head, qi - 3, qi - 1)

    def cond(carry):
        kb, *left = carry
        return jnp.logical_and(kb >= 0, sum(left) > 0)

    def body(carry):
        kb, *left = carry

        def stragglers():
            return tuple(lax.cond(left[p] > 0, lambda p=p: key_tiles([kb], None, False, [p])[0],
                                  lambda: jnp.int32(0)) for p in everyone)

        left = lax.cond(sum(left) > pairs // 2, lambda: key_tiles([kb], None, False, everyone), stragglers)
        return (kb - 1, *left)

    lax.while_loop(cond, body, (kb, *left))
    for p in range(pairs):
        o_ref[0, :, p * w:(p + 1) * w] = acc_ref[p].astype(o_ref.dtype)


def _stick_breaking_core(q, kv, bsz, seq):
    t = SB_TILE
    return pl.pallas_call(
        _sb_kernel,
        out_shape=jax.ShapeDtypeStruct((bsz, seq, D_MODEL), BF16),
        grid=(bsz, seq // t),
        in_specs=[
            pl.BlockSpec((1, t, D_MODEL), lambda b, i: (b, i, 0)),
            pl.BlockSpec((1, seq, D_MODEL), lambda b, i: (b, 0, 0)),
            pl.BlockSpec((1, seq, D_MODEL), lambda b, i: (b, 0, 1)),
        ],
        out_specs=pl.BlockSpec((1, t, D_MODEL), lambda b, i: (b, i, 0)),
        scratch_shapes=[pltpu.VMEM((SB_HEADS // 2, t, 2 * t), F32),
                        pltpu.VMEM((SB_HEADS // 2, t, 2 * SB_DH), F32)],
        compiler_params=_cparams(("parallel", "parallel")),
        name="stick_breaking_core",
    )(q, kv, kv)


def _route_kernel(a_ref, wp_ref, gp_ref, x_ref, g_ref, sh_ref, sc_ref, r_ref,
                  x1_ref, h_ref, meta_ref, meta_t_ref, pc_ref):
    tb = MOE_BLOCK
    rows = x_ref.shape[0]
    strips = [slice(s, s + ROUTE_STRIP) for s in range(0, rows, ROUTE_STRIP)]
    ys = [jnp.dot(a_ref[rs, :], wp_ref[...], preferred_element_type=F32) for rs in strips]
    parts = []
    for rs, y in zip(strips, ys):
        x1 = x_ref[rs, :] + gp_ref[0] * y
        x1_ref[rs, :] = x1
        h32 = _rms_mod(x1, g_ref[...], sh_ref[0], sc_ref[0])
        h1 = h32.astype(BF16)
        h_ref[rs, :] = h1
        e1 = h32 - h1.astype(F32)
        h2 = e1.astype(BF16)
        parts.append((h1, h2, (e1 - h2.astype(F32)).astype(BF16)))
    r3 = r_ref[...]
    s3 = jnp.concatenate([jnp.dot(h1, r3, preferred_element_type=F32) + jnp.dot(h2, r3, preferred_element_type=F32)
                          + jnp.dot(h3, r3, preferred_element_type=F32) for h1, h2, h3 in parts], axis=0)
    logits = s3 + pltpu.roll(s3, LANES - N_EXPERTS, 1) + pltpu.roll(s3, LANES - 2 * N_EXPERTS, 1)
    lane = lax.broadcasted_iota(jnp.int32, (rows, LANES), 1)
    lane_f = lane.astype(F32)
    neg = jnp.float32(-jnp.inf)
    lg = jnp.where(lane < N_EXPERTS, logits, neg)
    m1 = jnp.max(lg, axis=-1, keepdims=True)
    i1 = jnp.min(jnp.where(lg == m1, lane_f, float(LANES)), axis=-1, keepdims=True)
    oh1 = lane_f == i1
    lg2 = jnp.where(oh1, neg, lg)
    m2 = jnp.max(lg2, axis=-1, keepdims=True)
    i2 = jnp.min(jnp.where(lg2 == m2, lane_f, float(LANES)), axis=-1, keepdims=True)
    oh2 = lane_f == i2
    e = jnp.exp(m2 - m1)
    w1 = 1.0 / (1.0 + e)
    w2 = e / (1.0 + e)

    tr = lax.broadcasted_iota(jnp.int32, (tb, tb), 0)
    tc = lax.broadcasted_iota(jnp.int32, (tb, tb), 1)
    lower = jnp.where(tc < tr, 1.0, 0.0).astype(BF16)
    pj = lax.broadcasted_iota(jnp.int32, (LANES, LANES), 0)
    ps = lax.broadcasted_iota(jnp.int32, (LANES, LANES), 1)
    before = jnp.where(pj < ps, 1.0, 0.0)
    oh1f = jnp.where(oh1, 1.0, 0.0)
    oh2f = jnp.where(oh2, 1.0, 0.0)
    blocks = [slice(b * tb, (b + 1) * tb) for b in range(rows // tb)]
    cum1 = [jnp.dot(lower, oh1f[rs].astype(BF16), preferred_element_type=F32) for rs in blocks]
    cum2 = [jnp.dot(lower, oh2f[rs].astype(BF16), preferred_element_type=F32) for rs in blocks]
    cnt1 = [jnp.sum(oh1f[rs], axis=0, keepdims=True) for rs in blocks]
    cnt2 = [jnp.sum(oh2f[rs], axis=0, keepdims=True) for rs in blocks]
    pc = [jnp.floor((c1 + c2 + (MOE_GROUP - 1.0)) / MOE_GROUP) * MOE_GROUP
          for c1, c2 in zip(cnt1, cnt2)]
    off = [jnp.dot(jnp.broadcast_to(c, (8, LANES)), before, preferred_element_type=F32,
                   precision=lax.Precision.HIGHEST)[0:1] for c in pc]
    pos1 = jnp.concatenate([jnp.sum(oh1f[rs] * (off[b] + cum1[b]), axis=-1, keepdims=True)
                            for b, rs in enumerate(blocks)], axis=0)
    pos2 = jnp.concatenate([jnp.sum(oh2f[rs] * (off[b] + cnt1[b] + cum2[b]), axis=-1, keepdims=True)
                            for b, rs in enumerate(blocks)], axis=0)
    meta = jnp.where(lane == 0, pos1,
                     jnp.where(lane == 1, pos2,
                               jnp.where(lane == 2, w1, jnp.where(lane == 3, w2, 0.0))))
    meta_ref[...] = meta
    for b, rs in enumerate(blocks):
        meta_t_ref[b] = jnp.transpose(meta[rs])[0:8]
        pc_ref[b] = jnp.broadcast_to(pc[b], (8, LANES))


def _moe_route(a, w_proj, gate_proj, x, gain, shift, scale, router, seq):
    n, d = x.shape
    k = a.shape[1]
    tb = MOE_BLOCK
    nb = n // tb
    rb = min(ROUTE_BLOCKS, seq // tb)
    rows = rb * tb
    bsz = shift.shape[0]
    r1 = router.astype(BF16)
    r2 = (router - r1.astype(F32)).astype(BF16)
    r3 = (router - r1.astype(F32) - r2.astype(F32)).astype(BF16)
    r_pad = jnp.zeros((d, LANES), BF16).at[:, :3 * N_EXPERTS].set(jnp.concatenate([r1, r2, r3], axis=1))
    vec = lambda i: (i * rows // seq, 0, 0)
    return pl.pallas_call(
        _route_kernel,
        out_shape=(
            jax.ShapeDtypeStruct((n, d), F32),
            jax.ShapeDtypeStruct((n, d), BF16),
            jax.ShapeDtypeStruct((n, LANES), F32),
            jax.ShapeDtypeStruct((nb, 8, tb), F32),
            jax.ShapeDtypeStruct((nb, 8, LANES), F32),
        ),
        grid=(nb // rb,),
        in_specs=[
            pl.BlockSpec((rows, k), lambda i: (i, 0)),
            pl.BlockSpec((k, d), lambda i: (0, 0), pipeline_mode=pl.Buffered(1)),
            pl.BlockSpec((1, 1, d), vec),
            pl.BlockSpec((rows, d), lambda i: (i, 0)),
            pl.BlockSpec((1, d), lambda i: (0, 0)),
            pl.BlockSpec((1, 1, d), vec),
            pl.BlockSpec((1, 1, d), vec),
            pl.BlockSpec((d, LANES), lambda i: (0, 0)),
        ],
        out_specs=(
            pl.BlockSpec((rows, d), lambda i: (i, 0)),
            pl.BlockSpec((rows, d), lambda i: (i, 0)),
            pl.BlockSpec((rows, LANES), lambda i: (i, 0)),
            pl.BlockSpec((rb, 8, tb), lambda i: (i, 0, 0)),
            pl.BlockSpec((rb, 8, LANES), lambda i: (i, 0, 0)),
        ),
        compiler_params=_cparams(("parallel",)),
        name="moe_route",
    )(a, w_proj, gate_proj.reshape(bsz, 1, d), x, gain.reshape(1, d), shift.reshape(bsz, 1, d),
      scale.reshape(bsz, 1, d), r_pad)


def _segment_copies(loc_ref, glb_ref, len_ref, blk, make_copy, wait):
    for e in range(N_EXPERTS):
        idx = blk * N_EXPERTS + e
        loc = loc_ref[idx]
        glb = glb_ref[idx]
        length = len_ref[idx]
        done = jnp.int32(0)
        for k, sz in enumerate(SEG_SIZES):
            take = (length & sz) != 0

            @pl.when(take)
            def _(loc=loc, glb=glb, done=done, sz=sz, k=k):
                cp = make_copy(pl.multiple_of(loc + done, MOE_GROUP),
                               pl.multiple_of(glb + done, MOE_GROUP), sz)
                if wait:
                    cp.wait()
                else:
                    cp.start(priority=(e + k) % 2)

            done = done + jnp.where(take, sz, 0)


def _dispatch_kernel(loc_ref, glb_ref, len_ref, h_ref, meta_t_ref, hs_ref, buf_ref, zero_ref, sem):
    b = pl.program_id(0)
    tb = MOE_BLOCK
    pos1 = meta_t_ref[0, 0:1, :]
    pos2 = meta_t_ref[0, 1:2, :]
    r = lax.broadcasted_iota(jnp.int32, (MOE_CAP, tb), 0).astype(F32)
    perm = jnp.where((r == pos1) | (r == pos2), 1.0, 0.0).astype(BF16)
    sorted_rows = jnp.dot(perm, h_ref[...], preferred_element_type=F32).astype(BF16)

    def make_copy(loc, glb, sz):
        return pltpu.make_async_copy(buf_ref.at[pl.ds(loc, sz)], hs_ref.at[pl.ds(glb, sz)], sem)

    @pl.when(b > 0)
    def _():
        _segment_copies(loc_ref, glb_ref, len_ref, b - 1, make_copy, wait=True)

    buf_ref[...] = sorted_rows
    _segment_copies(loc_ref, glb_ref, len_ref, b, make_copy, wait=False)

    @pl.when(b == pl.num_programs(0) - 1)
    def _():
        def make_zero_copy(_, glb, sz):
            return pltpu.make_async_copy(zero_ref.at[pl.ds(0, sz)], hs_ref.at[pl.ds(glb, sz)], sem)

        zero_ref[...] = jnp.zeros_like(zero_ref)
        gaps = pl.num_programs(0)
        _segment_copies(loc_ref, glb_ref, len_ref, gaps, make_zero_copy, wait=False)

        tail = (gaps + 1) * N_EXPERTS
        pieces = len_ref[tail]

        def tail_copy(i):
            row = pl.multiple_of(glb_ref[tail] + i * zero_ref.shape[0], zero_ref.shape[0])
            return pltpu.make_async_copy(zero_ref, hs_ref.at[pl.ds(row, zero_ref.shape[0])], sem)

        @pl.loop(0, pieces)
        def _(i):
            tail_copy(i).start()

        _segment_copies(loc_ref, glb_ref, len_ref, b, make_copy, wait=True)
        _segment_copies(loc_ref, glb_ref, len_ref, gaps, make_zero_copy, wait=True)

        @pl.loop(0, pieces)
        def _(i):
            tail_copy(i).wait()


def _moe_dispatch(h, meta_t, loc, glb, seg_len, rows_total):
    n, d = h.shape
    tb = MOE_BLOCK
    nb = n // tb
    return pl.pallas_call(
        _dispatch_kernel,
        out_shape=jax.ShapeDtypeStruct((rows_total, d), BF16),
        grid_spec=pltpu.PrefetchScalarGridSpec(
            num_scalar_prefetch=3,
            grid=(nb,),
            in_specs=[
                pl.BlockSpec((tb, d), lambda i, *_: (i, 0)),
                pl.BlockSpec((1, 8, tb), lambda i, *_: (i, 0, 0)),
            ],
            out_specs=pl.BlockSpec(memory_space=pl.ANY),
            scratch_shapes=[pltpu.VMEM((MOE_CAP, d), BF16), pltpu.VMEM((SEG_SIZES[0], d), BF16),
                            pltpu.SemaphoreType.DMA],
        ),
        compiler_params=_cparams(("arbitrary",)),
        name="moe_dispatch",
    )(loc, glb, seg_len, h, meta_t)


def _expert_kernel(te_ref, tx_ref, tv_ref, x_ref, wa_ref, wb_ref, wo_ref, o_ref, acc_ref):
    i = pl.program_id(0)
    f = pl.program_id(1)

    last = pl.num_programs(1) - 1
    valid = tv_ref[i] == 1

    def run(emit):
        _swiglu(x_ref[...], lambda sl: wa_ref[0, :, sl], lambda sl: wb_ref[0, :, sl],
                lambda sl: wo_ref[0, :, sl].astype(BF16), wo_ref.shape[1], emit)

    @pl.when(jnp.logical_and(valid, f == 0))
    def _():
        def emit(cols, y):
            acc_ref[:, cols] = y
        run(emit)

    if FF_EXPERT // FF_EXPERT_TILE > 2:
        @pl.when(jnp.logical_and(valid, jnp.logical_and(f > 0, f < last)))
        def _():
            def emit(cols, y):
                acc_ref[:, cols] += y
            run(emit)

    @pl.when(jnp.logical_and(valid, f == last))
    def _():
        def emit(cols, y):
            o_ref[:, cols] = (acc_ref[:, cols] + y).astype(o_ref.dtype)
        run(emit)

    @pl.when(jnp.logical_and(tv_ref[i] == 0, f == pl.num_programs(1) - 1))
    def _():
        o_ref[...] = jnp.zeros_like(o_ref)


def _moe_experts(hs, w_in, w_out, tile_expert, tile_row, tile_valid):
    rows, d = hs.shape
    tm = ROW_TILE
    tf = FF_EXPERT_TILE
    nf = FF_EXPERT // tf
    nt = rows // tm

    def f_eff(i, f, tv):
        return jnp.where(tv[i] == 1, f, nf - 1)

    return pl.pallas_call(
        _expert_kernel,
        out_shape=jax.ShapeDtypeStruct((rows, d), BF16),
        grid_spec=pltpu.PrefetchScalarGridSpec(
            num_scalar_prefetch=3,
            grid=(nt, nf),
            in_specs=[
                pl.BlockSpec((tm, d), lambda i, f, te, tx, tv: (tx[i], 0)),
                pl.BlockSpec((1, d, tf), lambda i, f, te, tx, tv: (te[i], 0, f_eff(i, f, tv))),
                pl.BlockSpec((1, d, tf), lambda i, f, te, tx, tv: (te[i], 0, nf + f_eff(i, f, tv))),
                pl.BlockSpec((1, tf, d), lambda i, f, te, tx, tv: (te[i], f_eff(i, f, tv), 0)),
            ],
            out_specs=pl.BlockSpec((tm, d), lambda i, f, te, tx, tv: (i, 0)),
            scratch_shapes=[pltpu.VMEM((tm, d), F32)],
        ),
        compiler_params=_cparams(("arbitrary", "arbitrary")),
        name="moe_experts",
    )(tile_expert, tile_row, tile_valid, hs, w_in, w_in, w_out)


def _combine_kernel(loc_ref, glb_ref, len_ref, x_ref, gate_ref, meta_ref, fg_ref, ys_ref, o_ref, buf_ref, sems,
                    *, final_norm):
    b = pl.program_id(0)
    tb = MOE_BLOCK

    def fetch(blk, slot, wait):
        def make_copy(loc, glb, sz):
            return pltpu.make_async_copy(ys_ref.at[pl.ds(glb, sz)], buf_ref.at[slot, pl.ds(loc, sz)],
                                         sems.at[slot])

        _segment_copies(loc_ref, glb_ref, len_ref, blk, make_copy, wait)

    def start(blk, slot):
        buf_ref[slot] = jnp.zeros(buf_ref.shape[1:], buf_ref.dtype)
        fetch(blk, slot, wait=False)

    @pl.when(b == 0)
    def _():
        start(0, 0)

    @pl.when(b + 1 < pl.num_programs(0))
    def _():
        start(b + 1, (b + 1) % 2)

    slot = b % 2
    meta = meta_ref[...]
    pos1 = meta[:, 0:1]
    pos2 = meta[:, 1:2]
    w1 = meta[:, 2:3]
    w2 = meta[:, 3:4]
    r = lax.broadcasted_iota(jnp.int32, (tb, MOE_CAP), 1).astype(F32)
    sel1 = jnp.where(r == pos1, 1.0, 0.0).astype(BF16)
    sel2 = jnp.where(r == pos2, 1.0, 0.0).astype(BF16)
    fetch(b, slot, wait=True)
    ys = buf_ref[slot]
    y = (w1 * jnp.dot(sel1, ys, preferred_element_type=F32)
         + w2 * jnp.dot(sel2, ys, preferred_element_type=F32))
    x = x_ref[...] + gate_ref[0] * y
    if final_norm:
        x = (x * lax.rsqrt(jnp.mean(x * x, axis=-1, keepdims=True) + EPS)) * fg_ref[...]
    o_ref[...] = x


def _moe_combine(x, gate, meta, ys, loc, glb, seg_len, seq, final_gain=None):
    n, d = x.shape
    tb = MOE_BLOCK
    bsz = gate.shape[0]
    fg = jnp.ones((1, d), F32) if final_gain is None else final_gain.reshape(1, d)
    return pl.pallas_call(
        functools.partial(_combine_kernel, final_norm=final_gain is not None),
        out_shape=jax.ShapeDtypeStruct((n, d), F32),
        grid_spec=pltpu.PrefetchScalarGridSpec(
            num_scalar_prefetch=3,
            grid=(n // tb,),
            in_specs=[
                pl.BlockSpec((tb, d), lambda i, *_: (i, 0)),
                pl.BlockSpec((1, 1, d), lambda i, *_: (i * tb // seq, 0, 0)),
                pl.BlockSpec((tb, LANES), lambda i, *_: (i, 0)),
                pl.BlockSpec((1, d), lambda i, *_: (0, 0)),
                pl.BlockSpec(memory_space=pl.ANY),
            ],
            out_specs=pl.BlockSpec((tb, d), lambda i, *_: (i, 0)),
            scratch_shapes=[pltpu.VMEM((2, MOE_CAP, d), BF16), pltpu.SemaphoreType.DMA((2,))],
        ),
        compiler_params=_cparams(("arbitrary",)),
        name="moe_combine",
    )(loc, glb, seg_len, x, gate.reshape(bsz, 1, d), meta, fg, ys)


def _moe_layout(pc):
    nb = pc.shape[0]
    tm = ROW_TILE
    n_tiles = (nb * (2 * MOE_BLOCK + N_EXPERTS * (MOE_GROUP - 1)) + tm - 1) // tm + N_EXPERTS
    loc = jnp.cumsum(pc, axis=1) - pc
    tiles_e = (jnp.sum(pc, axis=0) + tm - 1) // tm
    tile_end = jnp.cumsum(tiles_e)
    base_e = (tile_end - tiles_e) * tm
    glb = base_e[None, :] + jnp.cumsum(pc, axis=0) - pc
    n_valid = tile_end[-1]
    ids = jnp.arange(n_tiles, dtype=jnp.int32)
    tile_row = jnp.maximum(jnp.minimum(ids, n_valid - 1), 0)
    tile_expert = jnp.sum((tile_row[:, None] >= tile_end[None, :]).astype(jnp.int32), axis=1)
    tile_valid = (ids < n_valid).astype(jnp.int32)
    rows_e = jnp.sum(pc, axis=0)
    first = jnp.zeros_like(rows_e).at[0].set(1)
    loc = jnp.concatenate([loc, jnp.zeros_like(rows_e)[None], jnp.zeros_like(rows_e)[None]])
    glb = jnp.concatenate([glb, (base_e + rows_e)[None], (first * n_valid * tm)[None]])
    seg = jnp.concatenate([pc, (tiles_e * tm - rows_e)[None],
                           (first * (n_tiles - n_valid) * (tm // SEG_SIZES[0]))[None]])
    flat = lambda a: a.reshape(-1).astype(jnp.int32)
    return flat(loc), flat(glb), flat(seg), tile_expert.astype(jnp.int32), tile_row.astype(jnp.int32), tile_valid, n_tiles * tm


def _moe_block(a, w_proj, gate_proj, x, gain, shift, scale, gate, router, w_in, w_out, layer, seq,
               final_gain=None):
    x, h, meta, meta_t, pc = _moe_route(a, w_proj, gate_proj, x, gain, shift, scale, router, seq)
    pc_i = pc[:, 0, :N_EXPERTS].astype(jnp.int32)
    loc, glb, seg_len, tile_expert, tile_row, tile_valid, rows_total = _moe_layout(pc_i)
    hs = _moe_dispatch(h, meta_t, loc, glb, seg_len, rows_total)
    ys = _moe_experts(hs, w_in, w_out, tile_expert + layer * N_EXPERTS, tile_row, tile_valid)
    return _moe_combine(x, gate, meta, ys, loc, glb, seg_len, seq, final_gain)


def _final_norm_kernel(x_ref, g_ref, o_ref):
    x = x_ref[...]
    o_ref[...] = (x * lax.rsqrt(jnp.mean(x * x, axis=-1, keepdims=True) + EPS)) * g_ref[...]


def _final_norm(x, gain):
    n, d = x.shape
    tm = min(ROW_TILE, n)
    return pl.pallas_call(
        _final_norm_kernel,
        out_shape=jax.ShapeDtypeStruct((n, d), F32),
        grid=(n // tm,),
        in_specs=[pl.BlockSpec((tm, d), lambda i: (i, 0)), pl.BlockSpec((1, d), lambda i: (0, 0))],
        out_specs=pl.BlockSpec((tm, d), lambda i: (i, 0)),
        compiler_params=_cparams(("parallel",)),
        name="final_norm",
    )(x, gain.reshape(1, d))


def kernel(x, c, positions, ada_w, ada_b, norm_mix, norm_ff, ret_w_in, ret_gn, ret_w_out, kv_ada_w, kv_ada_b, kv_norm, kv_w, sb_w_q, sb_w_out, ff_w_in, ff_w_out, moe_router, moe_w_in, moe_w_out, final_norm):
    bsz, seq, d = x.shape
    n = bsz * seq
    mod = _ada_mod(c, ada_w, ada_b)
    kv_mod = _ada_mod(c, kv_ada_w[None], kv_ada_b[None])[0]
    rope = _rope_tables(positions)
    ret_tables = _retention_tables()
    moe_in = moe_w_in.astype(BF16).reshape(-1, d, 2 * FF_EXPERT)
    moe_out = moe_w_out.reshape(-1, FF_EXPERT, d)
    xf = x.reshape(n, d)
    kv = None
    for i in range(DEPTH):
        sh_m, sc_m, g_m, sh_f, sc_f, g_f = [mod[i, :, p * d:(p + 1) * d] for p in range(6)]
        if i < N_A:
            proj = _norm_mod_matmul(xf, norm_mix[i], sh_m, sc_m, ret_w_in[i].astype(BF16), seq, rope=rope)
            o = _retention_core(proj.reshape(bsz, seq, -1), ret_gn[i], ret_tables, bsz, seq)
            o, w_o = o.reshape(n, -1), ret_w_out[i].astype(BF16)
        else:
            j = i - N_A
            q = _norm_mod_matmul(xf, norm_mix[i], sh_m, sc_m, sb_w_q[j].astype(BF16), seq)
            o = _stick_breaking_core(q.reshape(bsz, seq, d), kv, bsz, seq)
            o, w_o = o.reshape(n, d), sb_w_out[j].astype(BF16)
        if i % 2 == 0:
            xf = _dense_ffn(o, w_o, g_m, xf, norm_ff[i], sh_f, sc_f, g_f, ff_w_in[i // 2].astype(BF16),
                            ff_w_out[i // 2].astype(BF16), seq)
        else:
            xf = _moe_block(o, w_o, g_m, xf, norm_ff[i], sh_f, sc_f, g_f, moe_router[i // 2],
                            moe_in, moe_out, i // 2, seq,
                            final_gain=final_norm if i == DEPTH - 1 else None)
        if i == N_A - 1:
            kv = _norm_mod_matmul(xf, kv_norm, kv_mod[:, :d], kv_mod[:, d:], kv_w.astype(BF16), seq)
            kv = kv.reshape(bsz, seq, 2 * d)
    if (DEPTH - 1) % 2 == 0:
        xf = _final_norm(xf, final_norm)
    return xf.reshape(bsz, seq, d)
```
